```python
import jax, jax.numpy as jnp
from jax import lax
import numpy as np

D_MODEL = 1024
BATCH = 8
SEQ = 16384
DEPTH = 2

DENSE_HEAD_DIM = 128
N_FOX_HEADS = 4
N_SB_HEADS = 4
FOX_W = N_FOX_HEADS * DENSE_HEAD_DIM
SB_W = N_SB_HEADS * DENSE_HEAD_DIM
EVEN_WIDTH = FOX_W + SB_W
EVEN_SIZES = (FOX_W, FOX_W, FOX_W, N_FOX_HEADS, SB_W, SB_W, SB_W, EVEN_WIDTH)
EVEN_IN = sum(EVEN_SIZES)
DIL_HEAD_DIM = 64
DILATED_PAIRS = ((128, 1), (512, 4), (2048, 16))
N_DIL_GROUPS = len(DILATED_PAIRS)
N_DIL_HEADS = 8
DIL_W = N_DIL_GROUPS * N_DIL_HEADS * DIL_HEAD_DIM
ODD_WIDTH = N_DIL_HEADS * DIL_HEAD_DIM
ODD_SIZES = (DIL_W, DIL_W, DIL_W, ODD_WIDTH)
ODD_IN = sum(ODD_SIZES)
Q_BLOCK = 128
RMS_EPS = 1e-6
N_EVEN = (DEPTH + 1) // 2
N_ODD = DEPTH // 2

kernel_name = "hybrid_fox_stickbreak_dilated_gated"


def rmsnorm(x, g):
    xf = x.astype(jnp.float32)
    y = xf * lax.rsqrt(jnp.mean(xf * xf, axis=-1, keepdims=True) + RMS_EPS)
    return (y * g.astype(jnp.float32)).astype(x.dtype)


def split_points(sizes):
    return np.cumsum(np.array(sizes))[:-1].tolist()


def split_heads(t, n, hd):
    b, s, _ = t.shape
    return t.reshape(b, s, n, hd).transpose(0, 2, 1, 3)


def merge_heads(t):
    b, n, s, hd = t.shape
    return t.transpose(0, 2, 1, 3).reshape(b, s, n * hd)


def alibi_slopes(n):
    return jnp.asarray(2.0 ** (-8.0 * np.arange(1, n + 1) / n), dtype=jnp.float32)


def forgetting_attention(q, k, v, log_f):
    s = q.shape[2]
    q = q * jnp.asarray(DENSE_HEAD_DIM ** -0.5, q.dtype)
    cum = jnp.cumsum(log_f, axis=-1)
    outs = []
    for i in range(s // Q_BLOCK):
        start, end = i * Q_BLOCK, (i + 1) * Q_BLOCK
        qpos = start + jnp.arange(Q_BLOCK)
        causal = jnp.arange(end)[None, :] <= qpos[:, None]
        logits = jnp.einsum('bhqd,bhkd->bhqk', q[:, :, start:end], k[:, :, :end]).astype(jnp.float32)
        logits = logits + (cum[:, :, start:end, None] - cum[:, :, None, :end])
        p = jax.nn.softmax(jnp.where(causal, logits, -jnp.inf), axis=-1)
        outs.append(jnp.einsum('bhqk,bhkd->bhqd', p.astype(v.dtype), v[:, :, :end]))
    return jnp.concatenate(outs, axis=2)


def stick_breaking_attention(q, k, v):
    b, h, s, _ = q.shape
    q = q * jnp.asarray(DENSE_HEAD_DIM ** -0.5, q.dtype)
    c = jnp.arange(Q_BLOCK)
    upper_incl = (c[:, None] >= c[None, :]).astype(jnp.float32)
    outs = []
    for i in range(s // Q_BLOCK):
        start, end = i * Q_BLOCK, (i + 1) * Q_BLOCK
        nb = i + 1
        qpos = start + jnp.arange(Q_BLOCK)
        strict = jnp.arange(end)[None, :] < qpos[:, None]
        z = jnp.einsum('bhqd,bhkd->bhqk', q[:, :, start:end], k[:, :, :end]).astype(jnp.float32)
        log_beta = jax.nn.log_sigmoid(z)
        log_one_minus = jnp.where(strict, log_beta - z, 0.0)
        lob = log_one_minus.reshape(b, h, Q_BLOCK, nb, Q_BLOCK)
        incl = jnp.einsum('bhqnc,cd->bhqnd', lob, upper_incl)
        n_idx = jnp.arange(nb)
        later_blocks = (n_idx[:, None] > n_idx[None, :]).astype(jnp.float32)
        off = jnp.einsum('bhqn,nm->bhqm', jnp.sum(lob, axis=-1), later_blocks)
        later = (incl - lob + off[..., None]).reshape(b, h, Q_BLOCK, end)
        w = jnp.where(strict, jnp.exp(log_beta + later), 0.0)
        outs.append(jnp.einsum('bhqk,bhkd->bhqd', w.astype(v.dtype), v[:, :, :end]))
    return jnp.concatenate(outs, axis=2)


def dilated_group(q, k, v, window, dil, slopes):
    b, h, s, hd = q.shape
    length = s // dil
    blk = min(Q_BLOCK, length)
    nblk = length // blk
    span = window // dil

    def residues(t):
        return t.reshape(b, h, length, dil, hd).transpose(0, 1, 3, 2, 4).reshape(b, h, dil, nblk, blk, hd)

    def with_prev(t):
        prev = jnp.pad(t[:, :, :, :-1], ((0, 0), (0, 0), (0, 0), (1, 0), (0, 0), (0, 0)))
        return jnp.concatenate([prev, t], axis=4)

    qb = residues(q) * jnp.asarray(DIL_HEAD_DIM ** -0.5, q.dtype)
    kw = with_prev(residues(k))
    vw = with_prev(residues(v))
    a = jnp.arange(blk)[:, None]
    c = jnp.arange(2 * blk)[None, :]
    dist_sub = a - c + blk
    key_idx = jnp.arange(nblk)[:, None, None] * blk + c[None] - blk
    valid = (dist_sub >= 0) & (dist_sub <= span) & (key_idx >= 0)
    logits = jnp.einsum('bhrnqd,bhrnkd->bhrnqk', qb, kw).astype(jnp.float32)
    logits = logits - slopes[:, None, None, None, None] * (dist_sub * dil).astype(jnp.float32)
    logits = jnp.where(valid, logits, -jnp.inf)
    m = jnp.max(logits, axis=-1, keepdims=True)
    p = jnp.exp(logits - m)
    den = jnp.sum(p, axis=-1)
    o = jnp.einsum('bhrnqk,bhrnkd->bhrnqd', p.astype(vw.dtype), vw).astype(jnp.float32) / den[..., None]

    def back(t):
        extra = t.shape[5:]
        t = t.reshape(b, h, dil, length, *extra)
        t = jnp.moveaxis(t, 2, 3)
        return t.reshape(b, h, s, *extra)

    return back(o), back(m[..., 0]), back(den)


def dilated_window_attention(q, k, v):
    slopes = alibi_slopes(N_DIL_GROUPS * N_DIL_HEADS).reshape(N_DIL_GROUPS, N_DIL_HEADS)
    maxes, dens, outs = [], [], []
    for g, (window, dil) in enumerate(DILATED_PAIRS):
        o, m, den = dilated_group(q[g], k[g], v[g], window, dil, slopes[g])
        maxes.append(m); dens.append(den); outs.append(o)
    m_all = jnp.stack(maxes)
    den_all = jnp.stack(dens)
    o_all = jnp.stack(outs)
    wts = den_all * jnp.exp(m_all - jnp.max(m_all, axis=0))
    wts = wts / jnp.sum(wts, axis=0)
    return jnp.sum(wts[..., None] * o_all, axis=0).astype(v.dtype)


def even_layer(x, g_norm, w_in, b_f, g_q, g_k, w_out):
    h = rmsnorm(x, g_norm)
    proj = h @ w_in
    fq, fk, fv, f_logit, sq, sk, sv, gate = jnp.split(proj, split_points(EVEN_SIZES), axis=-1)
    log_f = jax.nn.log_sigmoid((f_logit + b_f).astype(jnp.float32)).transpose(0, 2, 1)
    fq = rmsnorm(split_heads(fq, N_FOX_HEADS, DENSE_HEAD_DIM), g_q)
    fk = rmsnorm(split_heads(fk, N_FOX_HEADS, DENSE_HEAD_DIM), g_k)
    fox = forgetting_attention(fq, fk, split_heads(fv, N_FOX_HEADS, DENSE_HEAD_DIM), log_f)
    sb = stick_breaking_attention(split_heads(sq, N_SB_HEADS, DENSE_HEAD_DIM),
                                  split_heads(sk, N_SB_HEADS, DENSE_HEAD_DIM),
                                  split_heads(sv, N_SB_HEADS, DENSE_HEAD_DIM))
    mixed = jnp.concatenate([merge_heads(fox), merge_heads(sb)], axis=-1) * jax.nn.silu(gate)
    return x + mixed @ w_out


def odd_layer(x, g_norm, w_in, g_q, g_k, w_out):
    h = rmsnorm(x, g_norm)
    proj = h @ w_in
    q, k, v, gate = jnp.split(proj, split_points(ODD_SIZES), axis=-1)
    b, s, _ = x.shape

    def groups(t):
        return t.reshape(b, s, N_DIL_GROUPS, N_DIL_HEADS, DIL_HEAD_DIM).transpose(2, 0, 3, 1, 4)

    q = rmsnorm(groups(q), g_q)
    k = rmsnorm(groups(k), g_k)
    att = dilated_window_attention(q, k, groups(v))
    mixed = merge_heads(att) * jax.nn.silu(gate)
    return x + mixed @ w_out


def _fwd_setup_inputs(seed: int = 0) -> dict:
    key = jax.random.key(seed)
    ks = jax.random.split(key, 13)
    f32 = jnp.float32
    x = jax.random.normal(ks[0], (BATCH, SEQ, D_MODEL), f32)
    even_norm = 1.0 + 0.02 * jax.random.normal(ks[1], (N_EVEN, D_MODEL), f32)
    even_w_in = jax.random.normal(ks[2], (N_EVEN, D_MODEL, EVEN_IN), f32) * D_MODEL ** -0.5
    even_b_f = (jnp.linspace(1.0, 4.0, N_FOX_HEADS, dtype=f32)[None, :]
                + 0.1 * jax.random.normal(ks[3], (N_EVEN, N_FOX_HEADS), f32))
    even_q_gain = 1.0 + 0.02 * jax.random.normal(ks[4], (N_EVEN, DENSE_HEAD_DIM), f32)
    even_k_gain = 1.0 + 0.02 * jax.random.normal(ks[5], (N_EVEN, DENSE_HEAD_DIM), f32)
    even_w_out = jax.random.normal(ks[6], (N_EVEN, EVEN_WIDTH, D_MODEL), f32) * EVEN_WIDTH ** -0.5
    odd_norm = 1.0 + 0.02 * jax.random.normal(ks[7], (N_ODD, D_MODEL), f32)
    odd_w_in = jax.random.normal(ks[8], (N_ODD, D_MODEL, ODD_IN), f32) * D_MODEL ** -0.5
    odd_q_gain = 1.0 + 0.02 * jax.random.normal(ks[9], (N_ODD, DIL_HEAD_DIM), f32)
    odd_k_gain = 1.0 + 0.02 * jax.random.normal(ks[10], (N_ODD, DIL_HEAD_DIM), f32)
    odd_w_out = jax.random.normal(ks[11], (N_ODD, ODD_WIDTH, D_MODEL), f32) * ODD_WIDTH ** -0.5
    return {"x": x, "even_norm": even_norm, "even_w_in": even_w_in, "even_b_f": even_b_f,
            "even_q_gain": even_q_gain, "even_k_gain": even_k_gain, "even_w_out": even_w_out,
            "odd_norm": odd_norm, "odd_w_in": odd_w_in, "odd_q_gain": odd_q_gain,
            "odd_k_gain": odd_k_gain, "odd_w_out": odd_w_out}


def _fwd_reference(x, even_norm, even_w_in, even_b_f, even_q_gain, even_k_gain, even_w_out,
              odd_norm, odd_w_in, odd_q_gain, odd_k_gain, odd_w_out):
    h = x
    for layer in range(DEPTH):
        i = layer // 2
        if layer % 2 == 0:
            h = even_layer(h, even_norm[i], even_w_in[i], even_b_f[i], even_q_gain[i],
                           even_k_gain[i], even_w_out[i])
        else:
            h = odd_layer(h, odd_norm[i], odd_w_in[i], odd_q_gain[i], odd_k_gain[i], odd_w_out[i])
    return h


import jax as _jax
import jax.numpy as _jnp

TWIN_FORMAT = 'train_step'
FWD_PARAMS = ['x', 'even_norm', 'even_w_in', 'even_b_f', 'even_q_gain', 'even_k_gain', 'even_w_out', 'odd_norm', 'odd_w_in', 'odd_q_gain', 'odd_k_gain', 'odd_w_out']
TWIN_WEIGHTS = ['even_norm', 'even_w_in', 'even_b_f', 'even_q_gain', 'even_k_gain', 'even_w_out', 'odd_norm', 'odd_w_in', 'odd_q_gain', 'odd_k_gain', 'odd_w_out']
TWIN_DIFF_INPUT = 'x'
TWIN_INPUTS = ['x', 'even_norm', 'even_w_in', 'even_b_f', 'even_q_gain', 'even_k_gain', 'even_w_out', 'odd_norm', 'odd_w_in', 'odd_q_gain', 'odd_k_gain', 'odd_w_out', 'loss_target', 'm_even_norm', 'm_even_w_in', 'm_even_b_f', 'm_even_q_gain', 'm_even_k_gain', 'm_even_w_out', 'm_odd_norm', 'm_odd_w_in', 'm_odd_q_gain', 'm_odd_k_gain', 'm_odd_w_out', 'v_even_norm', 'v_even_w_in', 'v_even_b_f', 'v_even_q_gain', 'v_even_k_gain', 'v_even_w_out', 'v_odd_norm', 'v_odd_w_in', 'v_odd_q_gain', 'v_odd_k_gain', 'v_odd_w_out']
TWIN_OUTPUTS = ['loss', 'grad_x', 'grad_even_norm', 'grad_even_w_in', 'grad_even_b_f', 'grad_even_q_gain', 'grad_even_k_gain', 'grad_even_w_out', 'grad_odd_norm', 'grad_odd_w_in', 'grad_odd_q_gain', 'grad_odd_k_gain', 'grad_odd_w_out', 'delta_even_norm', 'delta_even_w_in', 'delta_even_b_f', 'delta_even_q_gain', 'delta_even_k_gain', 'delta_even_w_out', 'delta_odd_norm', 'delta_odd_w_in', 'delta_odd_q_gain', 'delta_odd_k_gain', 'delta_odd_w_out', 'new_m_even_norm', 'new_m_even_w_in', 'new_m_even_b_f', 'new_m_even_q_gain', 'new_m_even_k_gain', 'new_m_even_w_out', 'new_m_odd_norm', 'new_m_odd_w_in', 'new_m_odd_q_gain', 'new_m_odd_k_gain', 'new_m_odd_w_out', 'new_v_even_norm', 'new_v_even_w_in', 'new_v_even_b_f', 'new_v_even_q_gain', 'new_v_even_k_gain', 'new_v_even_w_out', 'new_v_odd_norm', 'new_v_odd_w_in', 'new_v_odd_q_gain', 'new_v_odd_k_gain', 'new_v_odd_w_out']
TWIN_LEAF_KINDS = {'loss': 'loss', 'grad_x': 'grad_x', 'grad_even_norm': 'grad_w', 'grad_even_w_in': 'grad_w', 'grad_even_b_f': 'grad_w', 'grad_even_q_gain': 'grad_w', 'grad_even_k_gain': 'grad_w', 'grad_even_w_out': 'grad_w', 'grad_odd_norm': 'grad_w', 'grad_odd_w_in': 'grad_w', 'grad_odd_q_gain': 'grad_w', 'grad_odd_k_gain': 'grad_w', 'grad_odd_w_out': 'grad_w', 'delta_even_norm': 'delta_w', 'delta_even_w_in': 'delta_w', 'delta_even_b_f': 'delta_w', 'delta_even_q_gain': 'delta_w', 'delta_even_k_gain': 'delta_w', 'delta_even_w_out': 'delta_w', 'delta_odd_norm': 'delta_w', 'delta_odd_w_in': 'delta_w', 'delta_odd_q_gain': 'delta_w', 'delta_odd_k_gain': 'delta_w', 'delta_odd_w_out': 'delta_w', 'new_m_even_norm': 'new_m', 'new_m_even_w_in': 'new_m', 'new_m_even_b_f': 'new_m', 'new_m_even_q_gain': 'new_m', 'new_m_even_k_gain': 'new_m', 'new_m_even_w_out': 'new_m', 'new_m_odd_norm': 'new_m', 'new_m_odd_w_in': 'new_m', 'new_m_odd_q_gain': 'new_m', 'new_m_odd_k_gain': 'new_m', 'new_m_odd_w_out': 'new_m', 'new_v_even_norm': 'new_v', 'new_v_even_w_in': 'new_v', 'new_v_even_b_f': 'new_v', 'new_v_even_q_gain': 'new_v', 'new_v_even_k_gain': 'new_v', 'new_v_even_w_out': 'new_v', 'new_v_odd_norm': 'new_v', 'new_v_odd_w_in': 'new_v', 'new_v_odd_q_gain': 'new_v', 'new_v_odd_k_gain': 'new_v', 'new_v_odd_w_out': 'new_v'}


def _forward(args):
    return _fwd_reference(*[args[k] for k in FWD_PARAMS])


def _output_shape():
    def fwd():
        inp = _fwd_setup_inputs(0)
        return _fwd_reference(*[inp[k] for k in FWD_PARAMS])
    out = _jax.eval_shape(fwd)
    return out.shape, out.dtype

N_MICROBATCH = 1
ADAM_LR = 0.001
ADAM_B1 = 0.9
ADAM_B2 = 0.999
ADAM_EPS = 1e-08
ADAM_WD = 0.01
ADAM_STEP = 10
PER_EXAMPLE_BATCH_AXIS = {'x': 0, 'loss_target': 0}
SHARED_INPUTS = []
_WEIGHT_DTYPES = {'even_norm': _jnp.float32, 'even_w_in': _jnp.float32, 'even_b_f': _jnp.float32, 'even_q_gain': _jnp.float32, 'even_k_gain': _jnp.float32, 'even_w_out': _jnp.float32, 'odd_norm': _jnp.float32, 'odd_w_in': _jnp.float32, 'odd_q_gain': _jnp.float32, 'odd_k_gain': _jnp.float32, 'odd_w_out': _jnp.float32}
MOMENT_SCALE = {'even_norm': 3.036568e+01, 'even_w_in': 2.552421e-01, 'even_b_f': 2.577653e+02, 'even_q_gain': 8.321313e+00, 'even_k_gain': 8.324579e+00, 'even_w_out': 2.576704e-01, 'odd_norm': 5.444423e+00, 'odd_w_in': 1.115909e-01, 'odd_q_gain': 2.523184e+01, 'odd_k_gain': 2.533049e+01, 'odd_w_out': 1.099150e-01}


def _to_microbatches(a, axis):
    t = _jnp.moveaxis(a, axis, 0)
    t = t.reshape((N_MICROBATCH, t.shape[0] // N_MICROBATCH) + t.shape[1:])
    return _jnp.moveaxis(t, 1, axis + 1)


def setup_inputs(seed: int = 0) -> dict:
    inp = _fwd_setup_inputs(seed)
    key = _jax.random.fold_in(_jax.random.key(seed), 7919)
    shape, _ = _output_shape()
    out = dict(inp)
    out["loss_target"] = _jax.random.normal(_jax.random.fold_in(key, 0), shape, _jnp.float32)
    for i, name in enumerate(TWIN_WEIGHTS):
        w = inp[name].astype(_jnp.float32)
        if MOMENT_SCALE is None:
            s = _jnp.sqrt(_jnp.mean(_jnp.square(w)) + 1e-30)
        else:
            s = MOMENT_SCALE[name]
        km, kv = _jax.random.split(_jax.random.fold_in(key, i + 1))
        out[name] = w
        out["m_" + name] = s * _jax.random.normal(km, w.shape, _jnp.float32)
        out["v_" + name] = (s * s) * _jax.random.uniform(kv, w.shape, _jnp.float32, 0.5, 1.5)
    if N_MICROBATCH > 1:
        for name, axis in PER_EXAMPLE_BATCH_AXIS.items():
            out[name] = _to_microbatches(out[name], axis)
    return {'x': out['x'], 'even_norm': out['even_norm'], 'even_w_in': out['even_w_in'], 'even_b_f': out['even_b_f'], 'even_q_gain': out['even_q_gain'], 'even_k_gain': out['even_k_gain'], 'even_w_out': out['even_w_out'], 'odd_norm': out['odd_norm'], 'odd_w_in': out['odd_w_in'], 'odd_q_gain': out['odd_q_gain'], 'odd_k_gain': out['odd_k_gain'], 'odd_w_out': out['odd_w_out'], 'loss_target': out['loss_target'], 'm_even_norm': out['m_even_norm'], 'm_even_w_in': out['m_even_w_in'], 'm_even_b_f': out['m_even_b_f'], 'm_even_q_gain': out['m_even_q_gain'], 'm_even_k_gain': out['m_even_k_gain'], 'm_even_w_out': out['m_even_w_out'], 'm_odd_norm': out['m_odd_norm'], 'm_odd_w_in': out['m_odd_w_in'], 'm_odd_q_gain': out['m_odd_q_gain'], 'm_odd_k_gain': out['m_odd_k_gain'], 'm_odd_w_out': out['m_odd_w_out'], 'v_even_norm': out['v_even_norm'], 'v_even_w_in': out['v_even_w_in'], 'v_even_b_f': out['v_even_b_f'], 'v_even_q_gain': out['v_even_q_gain'], 'v_even_k_gain': out['v_even_k_gain'], 'v_even_w_out': out['v_even_w_out'], 'v_odd_norm': out['v_odd_norm'], 'v_odd_w_in': out['v_odd_w_in'], 'v_odd_q_gain': out['v_odd_q_gain'], 'v_odd_k_gain': out['v_odd_k_gain'], 'v_odd_w_out': out['v_odd_w_out']}


def _loss(weights, diff, rest, loss_target):
    with _jax.named_scope("forward"):
        args = {**rest, TWIN_DIFF_INPUT: diff, **{k: w.astype(_WEIGHT_DTYPES[k]) for k, w in weights.items()}}
        y = _forward(args)
    with _jax.named_scope("loss_head"):
        err = _jnp.square(y.astype(_jnp.float32) - loss_target)
        return 0.5 * _jnp.sum(_jnp.mean(err, axis=-1)) if err.ndim else 0.5 * err


def _adamw(w, g, m, v):
    m = ADAM_B1 * m + (1.0 - ADAM_B1) * g
    v = ADAM_B2 * v + (1.0 - ADAM_B2) * _jnp.square(g)
    m_hat = m / (1.0 - ADAM_B1 ** ADAM_STEP)
    v_hat = v / (1.0 - ADAM_B2 ** ADAM_STEP)
    delta = -ADAM_LR * (m_hat / (_jnp.sqrt(v_hat) + ADAM_EPS) + ADAM_WD * w)
    return delta, m, v


def reference(x, even_norm, even_w_in, even_b_f, even_q_gain, even_k_gain, even_w_out, odd_norm, odd_w_in, odd_q_gain, odd_k_gain, odd_w_out, loss_target, m_even_norm, m_even_w_in, m_even_b_f, m_even_q_gain, m_even_k_gain, m_even_w_out, m_odd_norm, m_odd_w_in, m_odd_q_gain, m_odd_k_gain, m_odd_w_out, v_even_norm, v_even_w_in, v_even_b_f, v_even_q_gain, v_even_k_gain, v_even_w_out, v_odd_norm, v_odd_w_in, v_odd_q_gain, v_odd_k_gain, v_odd_w_out):
    given = dict(x=x, even_norm=even_norm, even_w_in=even_w_in, even_b_f=even_b_f, even_q_gain=even_q_gain, even_k_gain=even_k_gain, even_w_out=even_w_out, odd_norm=odd_norm, odd_w_in=odd_w_in, odd_q_gain=odd_q_gain, odd_k_gain=odd_k_gain, odd_w_out=odd_w_out, loss_target=loss_target, m_even_norm=m_even_norm, m_even_w_in=m_even_w_in, m_even_b_f=m_even_b_f, m_even_q_gain=m_even_q_gain, m_even_k_gain=m_even_k_gain, m_even_w_out=m_even_w_out, m_odd_norm=m_odd_norm, m_odd_w_in=m_odd_w_in, m_odd_q_gain=m_odd_q_gain, m_odd_k_gain=m_odd_k_gain, m_odd_w_out=m_odd_w_out, v_even_norm=v_even_norm, v_even_w_in=v_even_w_in, v_even_b_f=v_even_b_f, v_even_q_gain=v_even_q_gain, v_even_k_gain=v_even_k_gain, v_even_w_out=v_even_w_out, v_odd_norm=v_odd_norm, v_odd_w_in=v_odd_w_in, v_odd_q_gain=v_odd_q_gain, v_odd_k_gain=v_odd_k_gain, v_odd_w_out=v_odd_w_out)
    weights = {n: given[n] for n in TWIN_WEIGHTS}
    shared = {n: given[n] for n in SHARED_INPUTS}
    per_example = {n: given[n] for n in ['x']}
    grad_fn = _jax.value_and_grad(_loss, argnums=(0, 1))

    def one_microbatch(ex, loss_target):
        ex = dict(ex)
        diff = ex.pop(TWIN_DIFF_INPUT)
        return grad_fn(weights, diff, {**shared, **ex}, loss_target)

    if N_MICROBATCH == 1:
        loss, (grad_w, grad_x) = one_microbatch(per_example, given["loss_target"])
    else:
        def body(carry, xs):
            loss_sum, grad_sum = carry
            l_k, (gw_k, gx_k) = one_microbatch(xs[0], xs[1])
            with _jax.named_scope("update"):
                return (loss_sum + l_k, _jax.tree.map(_jnp.add, grad_sum, gw_k)), gx_k

        init = (_jnp.zeros((), _jnp.float32), _jax.tree.map(_jnp.zeros_like, weights))
        (loss, grad_w), grad_x = _jax.lax.scan(body, init, (per_example, given["loss_target"]))
    with _jax.named_scope("update"):
        delta_w, new_m, new_v = {}, {}, {}
        for n in TWIN_WEIGHTS:
            delta_w[n], new_m[n], new_v[n] = _adamw(weights[n], grad_w[n], given["m_" + n], given["v_" + n])
    return (loss, grad_x, *[grad_w[n] for n in TWIN_WEIGHTS], *[delta_w[n] for n in TWIN_WEIGHTS],
            *[new_m[n] for n in TWIN_WEIGHTS], *[new_v[n] for n in TWIN_WEIGHTS])
```

```python
import functools

import jax
import jax.numpy as jnp
import numpy as np
from jax import lax
from jax.experimental import pallas as pl
from jax.experimental.pallas import tpu as pltpu

F32 = jnp.float32
BF16 = jnp.bfloat16

D_MODEL = 1024
HD = 128
NH = 4
HD2 = 64
NG = 3
NH2 = 8
DILATIONS = (1, 4, 16)
SPAN = 128
EVEN_W = 1024
ODD_W = 512
EVEN_IN = 4100
EVEN_PACK = 4224
FL_OFF = 1536
ODD_IN = 5120
RMS_EPS = 1e-6
SCALE_E = HD ** -0.5
SCALE_O = HD2 ** -0.5
ADAM_LR, ADAM_B1, ADAM_B2, ADAM_EPS, ADAM_WD, ADAM_STEP = 0.001, 0.9, 0.999, 1e-08, 0.01, 10

VMEM_CAP = 64 * 1024 * 1024
LANES = 128
MESH = pl.DeviceIdType.MESH


def _cparams(sem, vmem_mb):
    return pltpu.CompilerParams(dimension_semantics=sem, vmem_limit_bytes=min(vmem_mb << 20, VMEM_CAP - (6 << 20)))


def _silu(g):
    return g / (1.0 + jnp.exp(-g))


def _dsilu(g):
    s = 1.0 / (1.0 + jnp.exp(-g))
    return s * (1.0 + g * (1.0 - s))


def _split2(x):
    hi = x.astype(BF16)
    lo = (x - hi.astype(F32)).astype(BF16)
    return hi, lo


def _split3(x):
    hi = x.astype(BF16)
    r = x - hi.astype(F32)
    mid = r.astype(BF16)
    lo = (r - mid.astype(F32)).astype(BF16)
    return hi, mid, lo


def _dotf(a, b):
    return jnp.dot(a, b, preferred_element_type=F32)


def _dot_nt(a, b):
    return lax.dot_general(a, b, (((1,), (1,)), ((), ())), preferred_element_type=F32)


def _dot_tn(a, b):
    return lax.dot_general(a, b, (((0,), (0,)), ((), ())), preferred_element_type=F32)


def _segsum(x, hd):
    r = lax.broadcasted_iota(jnp.int32, (LANES, LANES), 0) // hd
    c = lax.broadcasted_iota(jnp.int32, (LANES, LANES), 1) // hd
    ones = (r == c).astype(BF16)
    outs = []
    for ch in range(x.shape[1] // LANES):
        hi, lo = _split2(x[:, ch * LANES:(ch + 1) * LANES])
        outs.append(_dotf(hi, ones) + _dotf(lo, ones))
    return outs[0] if len(outs) == 1 else jnp.concatenate(outs, axis=1)


def _suffix_matrix(n):
    r = lax.broadcasted_iota(jnp.int32, (n, n), 0)
    c = lax.broadcasted_iota(jnp.int32, (n, n), 1)
    return (r >= c).astype(BF16)


def _suffix(x, m):
    hi, lo = _split2(x)
    return _dotf(hi, m) + _dotf(lo, m)


def fox_fwd(qs, kn, v, nc, bq, bk):
    nh, s, _ = qs.shape
    nq = s // bq
    per = bq // bk

    def body(q_ref, k_ref, v_ref, nc_ref, o_ref, lse_ref):
        i = pl.program_id(1)
        q = q_ref[...]

        def tile(j, carry, masked):
            m, l, acc = carry
            off = pl.multiple_of(j * bk, bk)
            ks = k_ref[pl.ds(off, bk), :]
            vs = v_ref[pl.ds(off, bk), :]
            sc = _dot_nt(q, ks) + nc_ref[:, pl.ds(off, bk)]
            if masked:
                row = i * bq + lax.broadcasted_iota(jnp.int32, (bq, bk), 0)
                col = off + lax.broadcasted_iota(jnp.int32, (bq, bk), 1)
                sc = jnp.where(col <= row, sc, -jnp.inf)
            m_new = jnp.maximum(m, jnp.max(sc, axis=1, keepdims=True))
            alpha = jnp.exp(m - m_new)
            p = jnp.exp(sc - m_new)
            l = alpha * l + jnp.sum(p, axis=1, keepdims=True)
            acc = alpha * acc + _dotf(p.astype(BF16), vs)
            return m_new, l, acc

        init = (jnp.full((bq, 1), -jnp.inf, F32), jnp.zeros((bq, 1), F32), jnp.zeros((bq, HD), F32))
        carry = lax.fori_loop(0, i * per, lambda j, c: tile(j, c, False), init)
        for d in range(per):
            carry = tile(i * per + d, carry, True)
        m, l, acc = carry
        o_ref[...] = (acc / l).astype(o_ref.dtype)
        lse_ref[...] = m + jnp.log(l)

    return pl.pallas_call(
        body, name="fox_fwd",
        grid=(nh, nq),
        in_specs=[pl.BlockSpec((None, bq, HD), lambda h, i: (h, i, 0)),
                  pl.BlockSpec((None, s, HD), lambda h, i: (h, 0, 0)),
                  pl.BlockSpec((None, s, HD), lambda h, i: (h, 0, 0)),
                  pl.BlockSpec((None, 1, s), lambda h, i: (h, 0, 0))],
        out_specs=[pl.BlockSpec((None, bq, HD), lambda h, i: (h, i, 0)),
                   pl.BlockSpec((None, bq, 1), lambda h, i: (h, i, 0))],
        out_shape=[jax.ShapeDtypeStruct((nh, s, HD), BF16), jax.ShapeDtypeStruct((nh, s, 1), F32)],
        compiler_params=_cparams(("arbitrary", "arbitrary"), 40),
    )(qs, kn, v, nc)


def _sb_logits(q, ks, row0, col0, masked):
    z = _dot_nt(q, ks)
    sp = jnp.log(1.0 + jnp.exp(-jnp.abs(z)))
    logb = jnp.minimum(z, 0.0) - sp
    l = -jnp.maximum(z, 0.0) - sp
    strict = None
    if masked:
        row = row0 + lax.broadcasted_iota(jnp.int32, z.shape, 0)
        col = col0 + lax.broadcasted_iota(jnp.int32, z.shape, 1)
        strict = col < row
        l = jnp.where(strict, l, 0.0)
    return logb, l, strict


def sb_fwd(qs, k, v, bq, bk):
    nh, s, _ = qs.shape
    nq = s // bq
    per = bq // bk

    def body(q_ref, k_ref, v_ref, o_ref):
        i = pl.program_id(1)
        q = q_ref[...]
        tri = _suffix_matrix(bk)

        def tile(j, carry, masked):
            lrun, acc = carry
            off = pl.multiple_of(j * bk, bk)
            ks = k_ref[pl.ds(off, bk), :]
            vs = v_ref[pl.ds(off, bk), :]
            logb, l, strict = _sb_logits(q, ks, i * bq, off, masked)
            incl = _suffix(l, tri)
            w = jnp.exp(logb + (incl - l + lrun))
            if masked:
                w = jnp.where(strict, w, 0.0)
            acc = acc + _dotf(w.astype(BF16), vs)
            return lrun + incl[:, 0:1], acc

        carry = (jnp.zeros((bq, 1), F32), jnp.zeros((bq, HD), F32))
        for d in reversed(range(per)):
            carry = tile(i * per + d, carry, True)
        n_full = i * per
        carry = lax.fori_loop(0, n_full, lambda t, c: tile(n_full - 1 - t, c, False), carry)
        o_ref[...] = carry[1]

    return pl.pallas_call(
        body, name="sb_fwd",
        grid=(nh, nq),
        in_specs=[pl.BlockSpec((None, bq, HD), lambda h, i: (h, i, 0)),
                  pl.BlockSpec((None, s, HD), lambda h, i: (h, 0, 0)),
                  pl.BlockSpec((None, s, HD), lambda h, i: (h, 0, 0))],
        out_specs=pl.BlockSpec((None, bq, HD), lambda h, i: (h, i, 0)),
        out_shape=jax.ShapeDtypeStruct((nh, s, HD), F32),
        compiler_params=_cparams(("arbitrary", "arbitrary"), 40),
    )(qs, k, v)


def _attn_bwd_call(name, body, s, bq, ins, in_specs, extra_out_specs, extra_out_shapes):
    nh = NH
    nq = s // bq
    return pl.pallas_call(
        body, name=name,
        grid=(nh, nq),
        in_specs=in_specs,
        out_specs=[pl.BlockSpec((None, bq, HD), lambda h, i: (h, i, 0)),
                   pl.BlockSpec(memory_space=pl.ANY), pl.BlockSpec(memory_space=pl.ANY)] + extra_out_specs,
        out_shape=[jax.ShapeDtypeStruct((nh, s, HD), F32), jax.ShapeDtypeStruct((nh, s, HD), F32),
                   jax.ShapeDtypeStruct((nh, s, HD), F32)] + extra_out_shapes,
        scratch_shapes=[pltpu.VMEM((s, HD), F32), pltpu.VMEM((s, HD), F32), pltpu.SemaphoreType.DMA((2,))],
        compiler_params=_cparams(("arbitrary", "arbitrary"), 52),
    )(*ins)


def _flush_dkv(i, nq, h, dk_acc, dv_acc, dk_hbm, dv_hbm, sems):
    @pl.when(i == nq - 1)
    def _():
        ck = pltpu.make_async_copy(dk_acc, dk_hbm.at[h], sems.at[0])
        cv = pltpu.make_async_copy(dv_acc, dv_hbm.at[h], sems.at[1])
        ck.start()
        cv.start()
        ck.wait()
        cv.wait()


def fox_bwd(qs, kn, v, nc, lse, do, delta, bq, bk):
    nh, s, _ = qs.shape
    nq = s // bq
    per = bq // bk

    def body(q_ref, k_ref, v_ref, nc_ref, lse_ref, do_ref, dl_ref, dq_ref, dk_hbm, dv_hbm, dnc_ref, drow_ref, dk_acc, dv_acc, sems):
        h, i = pl.program_id(0), pl.program_id(1)

        @pl.when(i == 0)
        def _():
            dk_acc[...] = jnp.zeros_like(dk_acc)
            dv_acc[...] = jnp.zeros_like(dv_acc)
            dnc_ref[...] = jnp.zeros_like(dnc_ref)

        q = q_ref[...]
        do_t = do_ref[...]
        lse_t = lse_ref[...]
        dl_t = dl_ref[...]

        def tile(j, carry, masked):
            dq, drow = carry
            off = pl.multiple_of(j * bk, bk)
            ks = k_ref[pl.ds(off, bk), :]
            vs = v_ref[pl.ds(off, bk), :]
            sc = _dot_nt(q, ks) + nc_ref[:, pl.ds(off, bk)]
            p = jnp.exp(sc - lse_t)
            if masked:
                row = i * bq + lax.broadcasted_iota(jnp.int32, (bq, bk), 0)
                col = off + lax.broadcasted_iota(jnp.int32, (bq, bk), 1)
                p = jnp.where(col <= row, p, 0.0)
            dp = _dot_nt(do_t, vs)
            ds = p * (dp - dl_t)
            dsb = ds.astype(BF16)
            dk_acc[pl.ds(off, bk), :] += _dot_tn(dsb, q)
            dv_acc[pl.ds(off, bk), :] += _dot_tn(p.astype(BF16), do_t)
            dnc_ref[:, pl.ds(off, bk)] += jnp.sum(ds, axis=0, keepdims=True)
            return dq + _dotf(dsb, ks), drow + jnp.sum(ds, axis=1, keepdims=True)

        carry = (jnp.zeros((bq, HD), F32), jnp.zeros((bq, 1), F32))
        carry = lax.fori_loop(0, i * per, lambda j, c: tile(j, c, False), carry)
        for d in range(per):
            carry = tile(i * per + d, carry, True)
        dq_ref[...] = carry[0]
        drow_ref[...] = carry[1]
        _flush_dkv(i, nq, h, dk_acc, dv_acc, dk_hbm, dv_hbm, sems)

    tile_spec = pl.BlockSpec((None, bq, HD), lambda h, i: (h, i, 0))
    col_spec = pl.BlockSpec((None, bq, 1), lambda h, i: (h, i, 0))
    full_spec = pl.BlockSpec((None, s, HD), lambda h, i: (h, 0, 0))
    row_spec = pl.BlockSpec((None, 1, s), lambda h, i: (h, 0, 0))
    return _attn_bwd_call("fox_bwd", body, s, bq, (qs, kn, v, nc, lse, do, delta),
                          [tile_spec, full_spec, full_spec, row_spec, col_spec, tile_spec, col_spec],
                          [row_spec, col_spec],
                          [jax.ShapeDtypeStruct((nh, 1, s), F32), jax.ShapeDtypeStruct((nh, s, 1), F32)])


def sb_bwd(qs, k, v, do, delta, bq, bk):
    nh, s, _ = qs.shape
    nq = s // bq
    per = bq // bk

    def body(q_ref, k_ref, v_ref, do_ref, dl_ref, dq_ref, dk_hbm, dv_hbm, dk_acc, dv_acc, sems):
        h, i = pl.program_id(0), pl.program_id(1)

        @pl.when(i == 0)
        def _():
            dk_acc[...] = jnp.zeros_like(dk_acc)
            dv_acc[...] = jnp.zeros_like(dv_acc)

        q = q_ref[...]
        do_t = do_ref[...]
        dl_t = dl_ref[...]
        tri = _suffix_matrix(bk)

        def tile(j, carry, masked):
            lrun, erun, dq = carry
            off = pl.multiple_of(j * bk, bk)
            ks = k_ref[pl.ds(off, bk), :]
            vs = v_ref[pl.ds(off, bk), :]
            logb, l, strict = _sb_logits(q, ks, i * bq, off, masked)
            incl = _suffix(l, tri)
            w = jnp.exp(logb + (incl - l + lrun))
            if masked:
                w = jnp.where(strict, w, 0.0)
            wb = w.astype(BF16)
            e = _dot_nt(do_t, vs) * wb.astype(F32)
            einc = _suffix(e, tri)
            prefix = dl_t - (einc + erun)
            beta = jnp.exp(logb)
            dz = e - beta * (e + prefix)
            if masked:
                dz = jnp.where(strict, dz, 0.0)
            dzb = dz.astype(BF16)
            dk_acc[pl.ds(off, bk), :] += _dot_tn(dzb, q)
            dv_acc[pl.ds(off, bk), :] += _dot_tn(wb, do_t)
            return lrun + incl[:, 0:1], erun + einc[:, 0:1], dq + _dotf(dzb, ks)

        carry = (jnp.zeros((bq, 1), F32), jnp.zeros((bq, 1), F32), jnp.zeros((bq, HD), F32))
        for d in reversed(range(per)):
            carry = tile(i * per + d, carry, True)
        n_full = i * per
        carry = lax.fori_loop(0, n_full, lambda t, c: tile(n_full - 1 - t, c, False), carry)
        dq_ref[...] = carry[2]
        _flush_dkv(i, nq, h, dk_acc, dv_acc, dk_hbm, dv_hbm, sems)

    tile_spec = pl.BlockSpec((None, bq, HD), lambda h, i: (h, i, 0))
    col_spec = pl.BlockSpec((None, bq, 1), lambda h, i: (h, i, 0))
    full_spec = pl.BlockSpec((None, s, HD), lambda h, i: (h, 0, 0))
    return _attn_bwd_call("sb_bwd", body, s, bq, (qs, k, v, do, delta),
                          [tile_spec, full_spec, full_spec, tile_spec, col_spec], [], [])


BI = SPAN
PAIR = 2 * HD2


def _slopes(g):
    return [float(2.0 ** (-8.0 * (g * NH2 + h + 1) / (NG * NH2))) for h in range(NH2)]


def _head_lanes(hh):
    return (lax.broadcasted_iota(jnp.int32, (1, PAIR), 1) // HD2) == hh


def dil_fwd(q2s, k2n, v2, g):
    d = DILATIONS[g]
    _, s, _ = q2s.shape
    sub = s // d
    nblk = sub // BI
    slopes = _slopes(g)
    view = lambda t: t.reshape(NG, sub, d * ODD_W)

    def body(q_ref, kc_ref, kp_ref, vc_ref, vp_ref, o_ref, lse_ref):
        n = pl.program_id(1)
        a = lax.broadcasted_iota(jnp.int32, (BI, 2 * BI), 0)
        c = lax.broadcasted_iota(jnp.int32, (BI, 2 * BI), 1)
        dist = a - c + BI
        valid = (dist >= 0) & (dist <= SPAN) & ((c >= BI) | (n > 0))
        distf = (dist * d).astype(F32)
        for hp in range(NH2 // 2):
            sl = slice(hp * PAIR, (hp + 1) * PAIR)
            qp = q_ref[:, sl]
            kcat = jnp.concatenate([kp_ref[:, sl], kc_ref[:, sl]], axis=0)
            vcat = jnp.concatenate([vp_ref[:, sl], vc_ref[:, sl]], axis=0)
            o_pair = jnp.zeros((BI, PAIR), F32)
            lse_pair = jnp.zeros((BI, PAIR), F32)
            for hh in range(2):
                lm = _head_lanes(hh)
                qm = jnp.where(lm, qp, jnp.zeros_like(qp))
                logits = _dot_nt(qm, kcat) - slopes[hp * 2 + hh] * distf
                logits = jnp.where(valid, logits, -jnp.inf)
                m = jnp.max(logits, axis=1, keepdims=True)
                p = jnp.exp(logits - m)
                den = jnp.sum(p, axis=1, keepdims=True)
                oh = _dotf(p.astype(BF16), vcat) / den
                o_pair = jnp.where(lm, oh, o_pair)
                lse_pair = jnp.where(lm, m + jnp.log(den), lse_pair)
            o_ref[:, sl] = o_pair
            lse_ref[:, sl] = lse_pair

    cur = pl.BlockSpec((None, BI, ODD_W), lambda r, n: (g, n, r))
    prev = pl.BlockSpec((None, BI, ODD_W), lambda r, n: (g, jnp.maximum(n - 1, 0), r))
    out = pl.BlockSpec((BI, ODD_W), lambda r, n: (n, r))
    o, lse = pl.pallas_call(
        body, name=f"dil_fwd_{g}",
        grid=(d, nblk),
        in_specs=[cur, cur, prev, cur, prev],
        out_specs=[out, out],
        out_shape=[jax.ShapeDtypeStruct((sub, d * ODD_W), F32), jax.ShapeDtypeStruct((sub, d * ODD_W), F32)],
        compiler_params=_cparams(("parallel", "parallel"), 32),
    )(view(q2s), view(k2n), view(k2n), view(v2), view(v2))
    return o.reshape(s, ODD_W), lse.reshape(s, ODD_W)


def dil_bwd(q2s, k2n, v2, do, lse, delta, g):
    d = DILATIONS[g]
    _, s, _ = q2s.shape
    sub = s // d
    nblk = sub // BI
    slopes = _slopes(g)
    view3 = lambda t: t.reshape(NG, sub, d * ODD_W)
    view2 = lambda t: t.reshape(sub, d * ODD_W)

    def body(qc_ref, qn_ref, kc_ref, kp_ref, vc_ref, vp_ref, doc_ref, don_ref, lc_ref, ln_ref, dc_ref, dn_ref,
             dq_ref, dk_ref, dv_ref):
        n = pl.program_id(1)
        a = lax.broadcasted_iota(jnp.int32, (BI, 2 * BI), 0)
        c = lax.broadcasted_iota(jnp.int32, (BI, 2 * BI), 1)
        dist = a - c + BI
        valid = (dist >= 0) & (dist <= SPAN) & ((c >= BI) | (n > 0))
        distf = (dist * d).astype(F32)
        a2 = lax.broadcasted_iota(jnp.int32, (2 * BI, BI), 0)
        c2 = lax.broadcasted_iota(jnp.int32, (2 * BI, BI), 1)
        dist2 = a2 - c2
        valid2 = (dist2 >= 0) & (dist2 <= SPAN) & ((a2 < BI) | (n < nblk - 1))
        dist2f = (dist2 * d).astype(F32)
        for hp in range(NH2 // 2):
            sl = slice(hp * PAIR, (hp + 1) * PAIR)
            qc = qc_ref[:, sl]
            doc = doc_ref[:, sl]
            qcat = jnp.concatenate([qc, qn_ref[:, sl]], axis=0)
            docat = jnp.concatenate([doc, don_ref[:, sl]], axis=0)
            kc = kc_ref[:, sl]
            vc = vc_ref[:, sl]
            kcat = jnp.concatenate([kp_ref[:, sl], kc], axis=0)
            vcat = jnp.concatenate([vp_ref[:, sl], vc], axis=0)
            dq_pair = jnp.zeros((BI, PAIR), F32)
            dk_pair = jnp.zeros((BI, PAIR), F32)
            dv_pair = jnp.zeros((BI, PAIR), F32)
            for hh in range(2):
                h = hp * 2 + hh
                col = slice(h * HD2, h * HD2 + 1)
                lm = _head_lanes(hh)
                zq = jnp.zeros_like(qc)
                lse_c = lc_ref[:, col]
                dl_c = dc_ref[:, col]
                logits = _dot_nt(jnp.where(lm, qc, zq), kcat) - slopes[h] * distf
                p = jnp.exp(jnp.where(valid, logits, -jnp.inf) - lse_c)
                dp = _dot_nt(jnp.where(lm, doc, zq), vcat)
                ds = (p * (dp - dl_c)).astype(BF16)
                dq_pair = jnp.where(lm, _dotf(ds, kcat), dq_pair)

                zcat = jnp.zeros_like(qcat)
                qm = jnp.where(lm, qcat, zcat)
                dom = jnp.where(lm, docat, zcat)
                lse2 = jnp.concatenate([lse_c, ln_ref[:, col]], axis=0)
                dl2 = jnp.concatenate([dl_c, dn_ref[:, col]], axis=0)
                logits2 = _dot_nt(qm, kc) - slopes[h] * dist2f
                p2 = jnp.exp(jnp.where(valid2, logits2, -jnp.inf) - lse2)
                dp2 = _dot_nt(dom, vc)
                ds2 = (p2 * (dp2 - dl2)).astype(BF16)
                dk_pair = dk_pair + _dot_tn(ds2, qm)
                dv_pair = dv_pair + _dot_tn(p2.astype(BF16), dom)
            dq_ref[:, sl] = dq_pair
            dk_ref[:, sl] = dk_pair
            dv_ref[:, sl] = dv_pair.astype(dv_ref.dtype)

    nxt_idx = lambda n: jnp.minimum(n + 1, nblk - 1)
    prv_idx = lambda n: jnp.maximum(n - 1, 0)
    cur3 = pl.BlockSpec((None, BI, ODD_W), lambda r, n: (g, n, r))
    nxt3 = pl.BlockSpec((None, BI, ODD_W), lambda r, n: (g, nxt_idx(n), r))
    prv3 = pl.BlockSpec((None, BI, ODD_W), lambda r, n: (g, prv_idx(n), r))
    cur2 = pl.BlockSpec((BI, ODD_W), lambda r, n: (n, r))
    nxt2 = pl.BlockSpec((BI, ODD_W), lambda r, n: (nxt_idx(n), r))
    shape = lambda dt: jax.ShapeDtypeStruct((sub, d * ODD_W), dt)
    dq, dk, dv = pl.pallas_call(
        body, name=f"dil_bwd_{g}",
        grid=(d, nblk),
        in_specs=[cur3, nxt3, cur3, prv3, cur3, prv3, cur2, nxt2, cur2, nxt2, cur2, nxt2],
        out_specs=[cur2, cur2, cur2],
        out_shape=[shape(F32), shape(F32), shape(BF16)],
        compiler_params=_cparams(("parallel", "parallel"), 32),
    )(view3(q2s), view3(q2s), view3(k2n), view3(k2n), view3(v2), view3(v2),
      view2(do), view2(do), view2(lse), view2(lse), view2(delta), view2(delta))
    return dq.reshape(s, ODD_W), dk.reshape(s, ODD_W), dv.reshape(s, ODD_W)


TM = 256


def _rows(tm, w):
    return pl.BlockSpec((tm, w), lambda i: (i, 0))


def _whole(shape):
    return pl.BlockSpec(shape, lambda i: (0,) * len(shape))


def _heads(tm):
    return pl.BlockSpec((NH, tm, HD), lambda i: (0, i, 0))


def _groups(tm):
    return pl.BlockSpec((NG, tm, ODD_W), lambda i: (0, i, 0))


def _rms(x):
    return lax.rsqrt(jnp.mean(x * x, axis=1, keepdims=True) + RMS_EPS)


def _seg_rms(q, hd):
    return lax.rsqrt(_segsum(q * q, hd) * (1.0 / hd) + RMS_EPS)


def _seg_rms_bwd(q_raw, dqs, gain, scale, hd):
    q = q_raw.astype(F32)
    r = _seg_rms(q, hd)
    qhat = q * r
    u = dqs * (gain * scale)
    dq = r * (u - qhat * (_segsum(u * qhat, hd) * (1.0 / hd)))
    return dq, jnp.sum(dqs * qhat, axis=0, keepdims=True) * scale


def _rms_bwd(x, dh, gain):
    r = _rms(x)
    xhat = x * r
    u = dh * gain
    dx = r * (u - xhat * jnp.mean(u * xhat, axis=1, keepdims=True))
    return dx, jnp.sum(dh * xhat, axis=0, keepdims=True)


def even_in_fwd(x, gnorm, w_pack, bf_pad, gq, gk):
    s = x.shape[0]

    def body(x_ref, g_ref, w_ref, bf_ref, gq_ref, gk_ref,
             h_ref, fqs_ref, fkn_ref, fv_ref, fqr_ref, fkr_ref, flog_ref, sqs_ref, sk_ref, sv_ref, gate_ref):
        xt = x_ref[...]
        h = (xt * _rms(xt) * g_ref[...]).astype(BF16)
        h_ref[...] = h
        proj = _dotf(h, w_ref[...])
        fq = proj[:, 0:512]
        fk = proj[:, 512:1024]
        fqs = fq * _seg_rms(fq, HD) * (gq_ref[...] * SCALE_E)
        fkn = fk * _seg_rms(fk, HD) * gk_ref[...]
        flog_ref[...] = proj[:, FL_OFF:FL_OFF + LANES] + bf_ref[...]
        o = FL_OFF + LANES
        for hh in range(NH):
            sl = slice(hh * HD, (hh + 1) * HD)
            fqs_ref[hh] = fqs[:, sl].astype(BF16)
            fkn_ref[hh] = fkn[:, sl].astype(BF16)
            fqr_ref[hh] = fq[:, sl].astype(BF16)
            fkr_ref[hh] = fk[:, sl].astype(BF16)
            fv_ref[hh] = proj[:, 1024 + hh * HD:1024 + (hh + 1) * HD].astype(BF16)
            sqs_ref[hh] = (proj[:, o + hh * HD:o + (hh + 1) * HD] * SCALE_E).astype(BF16)
            sk_ref[hh] = proj[:, o + 512 + hh * HD:o + 512 + (hh + 1) * HD].astype(BF16)
            sv_ref[hh] = proj[:, o + 1024 + hh * HD:o + 1024 + (hh + 1) * HD].astype(BF16)
        gate_ref[...] = proj[:, o + 1536:o + 2560].astype(BF16)

    hs = jax.ShapeDtypeStruct((NH, s, HD), BF16)
    return pl.pallas_call(
        body, name="even_in_fwd",
        grid=(s // TM,),
        in_specs=[_rows(TM, D_MODEL), _whole((1, D_MODEL)), _whole((D_MODEL, EVEN_PACK)), _whole((1, LANES)),
                  _whole((1, 512)), _whole((1, 512))],
        out_specs=[_rows(TM, D_MODEL)] + [_heads(TM)] * 5 + [_rows(TM, LANES)] + [_heads(TM)] * 3 + [_rows(TM, EVEN_W)],
        out_shape=[jax.ShapeDtypeStruct((s, D_MODEL), BF16)] + [hs] * 5 + [jax.ShapeDtypeStruct((s, LANES), F32)]
        + [hs] * 3 + [jax.ShapeDtypeStruct((s, EVEN_W), BF16)],
        compiler_params=_cparams(("parallel",), 52),
    )(x, gnorm, w_pack, bf_pad, gq, gk)


def _prefix_matrices(r):
    a = lax.broadcasted_iota(jnp.int32, (LANES, LANES), 0)
    b = lax.broadcasted_iota(jnp.int32, (LANES, LANES), 1)
    ra = lax.broadcasted_iota(jnp.int32, (r, r), 0)
    rb = lax.broadcasted_iota(jnp.int32, (r, r), 1)
    return a, b, ra, rb


def _dot3_right(x, m):
    a, b, c = _split3(x)
    return _dotf(a, m) + _dotf(b, m) + _dotf(c, m)


def _dot3_left(m, x):
    a, b, c = _split3(x)
    return _dotf(m, a) + _dotf(m, b) + _dotf(m, c)


def fox_cum(flog4):
    nh, r, _ = flog4.shape

    def body(f_ref, nc_ref):
        z = f_ref[...]
        lf = jnp.minimum(z, 0.0) - jnp.log(1.0 + jnp.exp(-jnp.abs(z)))
        a, b, ra, rb = _prefix_matrices(r)
        within = _dot3_right(lf, (a <= b).astype(BF16))
        tot = jnp.broadcast_to(within[:, LANES - 1:LANES], (r, LANES))
        nc_ref[...] = -(within + _dot3_left((rb < ra).astype(BF16), tot))

    return pl.pallas_call(
        body, name="fox_cum", grid=(nh,),
        in_specs=[pl.BlockSpec((None, r, LANES), lambda h: (h, 0, 0))],
        out_specs=pl.BlockSpec((None, r, LANES), lambda h: (h, 0, 0)),
        out_shape=jax.ShapeDtypeStruct((nh, r, LANES), F32),
        compiler_params=_cparams(("parallel",), 16),
    )(flog4)


def fox_cum_bwd(dcum4, flog4):
    nh, r, _ = flog4.shape

    def body(d_ref, f_ref, o_ref):
        a, b, ra, rb = _prefix_matrices(r)
        dc = d_ref[...]
        within = _dot3_right(dc, (a >= b).astype(BF16))
        tot = jnp.broadcast_to(within[:, 0:1], (r, LANES))
        dlf = within + _dot3_left((rb > ra).astype(BF16), tot)
        o_ref[...] = dlf / (1.0 + jnp.exp(f_ref[...]))

    spec = pl.BlockSpec((None, r, LANES), lambda h: (h, 0, 0))
    return pl.pallas_call(
        body, name="fox_cum_bwd", grid=(nh,),
        in_specs=[spec, spec], out_specs=spec,
        out_shape=jax.ShapeDtypeStruct((nh, r, LANES), F32),
        compiler_params=_cparams(("parallel",), 16),
    )(dcum4, flog4)


def even_out_fwd(fo, so, gate, x, w_out):
    s = x.shape[0]
    tm = 2 * TM

    def body(fo_ref, so_ref, g_ref, x_ref, w_ref, y_ref):
        sg = _silu(g_ref[...].astype(F32))
        acc = x_ref[...]
        for hh in range(NH):
            mf = (fo_ref[hh].astype(F32) * sg[:, hh * HD:(hh + 1) * HD]).astype(BF16)
            ms = (so_ref[hh] * sg[:, 512 + hh * HD:512 + (hh + 1) * HD]).astype(BF16)
            acc = acc + _dotf(mf, w_ref[hh * HD:(hh + 1) * HD, :]) + _dotf(ms, w_ref[512 + hh * HD:512 + (hh + 1) * HD, :])
        y_ref[...] = acc

    return pl.pallas_call(
        body, name="even_out_fwd", grid=(s // tm,),
        in_specs=[_heads(tm), _heads(tm), _rows(tm, EVEN_W), _rows(tm, D_MODEL), _whole((EVEN_W, D_MODEL))],
        out_specs=_rows(tm, D_MODEL),
        out_shape=jax.ShapeDtypeStruct((s, D_MODEL), F32),
        compiler_params=_cparams(("parallel",), 40),
    )(fo, so, gate, x, w_out)


def odd_in_fwd(y1, gnorm, w2, gq, gk):
    s = y1.shape[0]

    def body(x_ref, g_ref, w_ref, gq_ref, gk_ref, h_ref, qs_ref, kn_ref, v_ref, qr_ref, kr_ref, gate_ref):
        xt = x_ref[...]
        h = (xt * _rms(xt) * g_ref[...]).astype(BF16)
        h_ref[...] = h
        proj = _dotf(h, w_ref[...])
        for g in range(NG):
            q = proj[:, g * ODD_W:(g + 1) * ODD_W]
            k = proj[:, 1536 + g * ODD_W:1536 + (g + 1) * ODD_W]
            qs_ref[g] = (q * _seg_rms(q, HD2) * (gq_ref[...] * SCALE_O)).astype(BF16)
            kn_ref[g] = (k * _seg_rms(k, HD2) * gk_ref[...]).astype(BF16)
            qr_ref[g] = q.astype(BF16)
            kr_ref[g] = k.astype(BF16)
            v_ref[g] = proj[:, 3072 + g * ODD_W:3072 + (g + 1) * ODD_W].astype(BF16)
        gate_ref[...] = proj[:, 4608:5120].astype(BF16)

    gs = jax.ShapeDtypeStruct((NG, s, ODD_W), BF16)
    return pl.pallas_call(
        body, name="odd_in_fwd", grid=(s // TM,),
        in_specs=[_rows(TM, D_MODEL), _whole((1, D_MODEL)), _whole((D_MODEL, ODD_IN)), _whole((1, ODD_W)), _whole((1, ODD_W))],
        out_specs=[_rows(TM, D_MODEL)] + [_groups(TM)] * 5 + [_rows(TM, ODD_W)],
        out_shape=[jax.ShapeDtypeStruct((s, D_MODEL), BF16)] + [gs] * 5 + [jax.ShapeDtypeStruct((s, ODD_W), BF16)],
        compiler_params=_cparams(("parallel",), 52),
    )(y1, gnorm, w2, gq, gk)


def odd_out_fwd(o0, l0, o1, l1, o2, l2, gate2, y1, target, w_out2):
    s = y1.shape[0]
    tm = 2 * TM
    nt = s // tm

    def body(o0_ref, l0_ref, o1_ref, l1_ref, o2_ref, l2_ref, g_ref, y1_ref, t_ref, w_ref,
             att_ref, lse_ref, dy_ref, loss_ref):
        l0t, l1t, l2t = l0_ref[...], l1_ref[...], l2_ref[...]
        m = jnp.maximum(jnp.maximum(l0t, l1t), l2t)
        e0, e1, e2 = jnp.exp(l0t - m), jnp.exp(l1t - m), jnp.exp(l2t - m)
        den = e0 + e1 + e2
        att = (e0 * o0_ref[...] + e1 * o1_ref[...] + e2 * o2_ref[...]) / den
        att_ref[...] = att.astype(BF16)
        lse_ref[...] = m + jnp.log(den)
        mixed = (att * _silu(g_ref[...].astype(F32))).astype(BF16)
        diff = y1_ref[...] + _dotf(mixed, w_ref[...]) - t_ref[...]
        dy_ref[...] = diff * (1.0 / D_MODEL)
        loss_ref[...] = jnp.full((1, 1, LANES), 0.5 / D_MODEL, F32) * jnp.sum(diff * diff)

    big = jax.ShapeDtypeStruct((s, ODD_W), F32)
    return pl.pallas_call(
        body, name="odd_out_fwd", grid=(nt,),
        in_specs=[_rows(tm, ODD_W)] * 7 + [_rows(tm, D_MODEL), _rows(tm, D_MODEL), _whole((ODD_W, D_MODEL))],
        out_specs=[_rows(tm, ODD_W), _rows(tm, ODD_W), _rows(tm, D_MODEL), pl.BlockSpec((1, 1, LANES), lambda i: (i, 0, 0))],
        out_shape=[jax.ShapeDtypeStruct((s, ODD_W), BF16), big, jax.ShapeDtypeStruct((s, D_MODEL), F32),
                   jax.ShapeDtypeStruct((nt, 1, LANES), F32)],
        compiler_params=_cparams(("parallel",), 40),
    )(o0, l0, o1, l1, o2, l2, gate2, y1, target, w_out2)


def odd_out_bwd(dy2, w_out2_t, att, gate2):
    s = dy2.shape[0]
    tm = 2 * TM

    def body(dy_ref, wt_ref, att_ref, g_ref, datt_ref, dgate_ref, delta_ref, dw_ref):
        @pl.when(pl.program_id(0) == 0)
        def _():
            dw_ref[...] = jnp.zeros_like(dw_ref)

        dyb = dy_ref[...].astype(BF16)
        dmixed = _dotf(dyb, wt_ref[...])
        g = g_ref[...].astype(F32)
        att_t = att_ref[...].astype(F32)
        sg = _silu(g)
        datt = (dmixed * sg).astype(BF16)
        datt_ref[...] = datt
        dgate_ref[...] = (dmixed * att_t * _dsilu(g)).astype(BF16)
        delta_ref[...] = _segsum(datt.astype(F32) * att_t, HD2)
        dw_ref[...] += _dot_tn((att_t * sg).astype(BF16), dyb)

    return pl.pallas_call(
        body, name="odd_out_bwd", grid=(s // tm,),
        in_specs=[_rows(tm, D_MODEL), _whole((D_MODEL, ODD_W)), _rows(tm, ODD_W), _rows(tm, ODD_W)],
        out_specs=[_rows(tm, ODD_W), _rows(tm, ODD_W), _rows(tm, ODD_W), _whole((ODD_W, D_MODEL))],
        out_shape=[jax.ShapeDtypeStruct((s, ODD_W), BF16), jax.ShapeDtypeStruct((s, ODD_W), BF16),
                   jax.ShapeDtypeStruct((s, ODD_W), F32), jax.ShapeDtypeStruct((ODD_W, D_MODEL), F32)],
        compiler_params=_cparams(("arbitrary",), 40),
    )(dy2, w_out2_t, att, gate2)


def odd_in_bwd(dqs, dks, dvs, dgate2, q2r, k2r, gq, gk, w2_t, y1, dy2, gnorm):
    s = y1.shape[0]

    def body(dq0, dq1, dq2, dk0, dk1, dk2, dv0, dv1, dv2, dg_ref, qr_ref, kr_ref, gq_ref, gk_ref, wt_ref, y1_ref, dy_ref, gn_ref,
             dproj_ref, dy1_ref, dgn_ref, dgain_ref):
        @pl.when(pl.program_id(0) == 0)
        def _():
            dgn_ref[...] = jnp.zeros_like(dgn_ref)
            dgain_ref[...] = jnp.zeros_like(dgain_ref)

        for g, (dq_ref, dk_ref, dv_ref) in enumerate(((dq0, dk0, dv0), (dq1, dk1, dv1), (dq2, dk2, dv2))):
            dq, gq_row = _seg_rms_bwd(qr_ref[g], dq_ref[...], gq_ref[...], SCALE_O, HD2)
            dk, gk_row = _seg_rms_bwd(kr_ref[g], dk_ref[...], gk_ref[...], 1.0, HD2)
            dproj_ref[:, g * ODD_W:(g + 1) * ODD_W] = dq.astype(BF16)
            dproj_ref[:, 1536 + g * ODD_W:1536 + (g + 1) * ODD_W] = dk.astype(BF16)
            dproj_ref[:, 3072 + g * ODD_W:3072 + (g + 1) * ODD_W] = dv_ref[...]
            dgain_ref[g:g + 1, :] += gq_row
            dgain_ref[NG + g:NG + g + 1, :] += gk_row
        dproj_ref[:, 4608:5120] = dg_ref[...]
        dh = _dotf(dproj_ref[...], wt_ref[...])
        dx, gn_row = _rms_bwd(y1_ref[...], dh, gn_ref[...])
        dy1_ref[...] = dy_ref[...] + dx
        dgn_ref[...] += gn_row

    f32r, bf16r = _rows(TM, ODD_W), _rows(TM, ODD_W)
    return pl.pallas_call(
        body, name="odd_in_bwd", grid=(s // TM,),
        in_specs=[f32r] * 6 + [bf16r] * 4 + [_groups(TM), _groups(TM), _whole((1, ODD_W)), _whole((1, ODD_W)),
                                             _whole((ODD_IN, D_MODEL)), _rows(TM, D_MODEL), _rows(TM, D_MODEL), _whole((1, D_MODEL))],
        out_specs=[_rows(TM, ODD_IN), _rows(TM, D_MODEL), _whole((1, D_MODEL)), _whole((8, ODD_W))],
        out_shape=[jax.ShapeDtypeStruct((s, ODD_IN), BF16), jax.ShapeDtypeStruct((s, D_MODEL), F32),
                   jax.ShapeDtypeStruct((1, D_MODEL), F32), jax.ShapeDtypeStruct((8, ODD_W), F32)],
        compiler_params=_cparams(("arbitrary",), 52),
    )(*dqs, *dks, *dvs, dgate2, q2r, k2r, gq, gk, w2_t, y1, dy2, gnorm)


def even_out_bwd(dy1, w_out_t, fo, so, gate):
    s = dy1.shape[0]
    tm = 2 * TM

    def body(dy_ref, wt_ref, fo_ref, so_ref, g_ref, dfo_ref, dso_ref, dgate_ref, delf_ref, dels_ref, dw_ref):
        @pl.when(pl.program_id(0) == 0)
        def _():
            dw_ref[...] = jnp.zeros_like(dw_ref)

        dyb = dy_ref[...].astype(BF16)
        dmixed = _dotf(dyb, wt_ref[...])
        g = g_ref[...].astype(F32)
        sg, dsg = _silu(g), _dsilu(g)
        for hh in range(NH):
            for base, o_ref, do_ref, del_ref in ((0, fo_ref, dfo_ref, delf_ref), (512, so_ref, dso_ref, dels_ref)):
                sl = slice(base + hh * HD, base + (hh + 1) * HD)
                o = o_ref[hh].astype(F32)
                do = (dmixed[:, sl] * sg[:, sl]).astype(BF16)
                do_ref[hh] = do
                del_ref[hh] = jnp.sum(do.astype(F32) * o, axis=1, keepdims=True)
                dgate_ref[:, sl] = (dmixed[:, sl] * o * dsg[:, sl]).astype(BF16)
                dw_ref[sl, :] += _dot_tn((o * sg[:, sl]).astype(BF16), dyb)

    cols = pl.BlockSpec((NH, tm, 1), lambda i: (0, i, 0))
    hs = jax.ShapeDtypeStruct((NH, s, HD), BF16)
    cs = jax.ShapeDtypeStruct((NH, s, 1), F32)
    return pl.pallas_call(
        body, name="even_out_bwd", grid=(s // tm,),
        in_specs=[_rows(tm, D_MODEL), _whole((D_MODEL, EVEN_W)), _heads(tm), _heads(tm), _rows(tm, EVEN_W)],
        out_specs=[_heads(tm), _heads(tm), _rows(tm, EVEN_W), cols, cols, _whole((EVEN_W, D_MODEL))],
        out_shape=[hs, hs, jax.ShapeDtypeStruct((s, EVEN_W), BF16), cs, cs, jax.ShapeDtypeStruct((EVEN_W, D_MODEL), F32)],
        compiler_params=_cparams(("arbitrary",), 48),
    )(dy1, w_out_t, fo, so, gate)


def even_in_bwd(dfqs, dfkn, dfv, dsqs, dsk, dsv, dgate, dflog, fqr, fkr, gq, gk, w_pack_t, x, dy1, gnorm):
    s = x.shape[0]

    def body(dfq_ref, dfk_ref, dfv_ref, dsq_ref, dsk_ref, dsv_ref, dg_ref, dfl_ref, qr_ref, kr_ref, gq_ref, gk_ref,
             wt_ref, x_ref, dy_ref, gn_ref, dproj_ref, dx_ref, dgn_ref, dgain_ref, dbf_ref):
        @pl.when(pl.program_id(0) == 0)
        def _():
            dgn_ref[...] = jnp.zeros_like(dgn_ref)
            dgain_ref[...] = jnp.zeros_like(dgain_ref)
            dbf_ref[...] = jnp.zeros_like(dbf_ref)

        o = FL_OFF + LANES
        for hh in range(NH):
            sl = slice(hh * HD, (hh + 1) * HD)
            dq, gq_row = _seg_rms_bwd(qr_ref[hh], dfq_ref[hh], gq_ref[:, sl], SCALE_E, HD)
            dk, gk_row = _seg_rms_bwd(kr_ref[hh], dfk_ref[hh], gk_ref[:, sl], 1.0, HD)
            dproj_ref[:, sl] = dq.astype(BF16)
            dproj_ref[:, 512 + hh * HD:512 + (hh + 1) * HD] = dk.astype(BF16)
            dproj_ref[:, 1024 + hh * HD:1024 + (hh + 1) * HD] = dfv_ref[hh].astype(BF16)
            dproj_ref[:, o + hh * HD:o + (hh + 1) * HD] = (dsq_ref[hh] * SCALE_E).astype(BF16)
            dproj_ref[:, o + 512 + hh * HD:o + 512 + (hh + 1) * HD] = dsk_ref[hh].astype(BF16)
            dproj_ref[:, o + 1024 + hh * HD:o + 1024 + (hh + 1) * HD] = dsv_ref[hh].astype(BF16)
            dgain_ref[0:1, sl] += gq_row
            dgain_ref[1:2, sl] += gk_row
        dfl = dfl_ref[...]
        dproj_ref[:, FL_OFF:FL_OFF + LANES] = dfl.astype(BF16)
        dbf_ref[...] += jnp.sum(dfl, axis=0, keepdims=True)
        dproj_ref[:, o + 1536:o + 2560] = dg_ref[...]
        dh = _dotf(dproj_ref[...], wt_ref[...])
        dx, gn_row = _rms_bwd(x_ref[...], dh, gn_ref[...])
        dx_ref[...] = dy_ref[...] + dx
        dgn_ref[...] += gn_row

    return pl.pallas_call(
        body, name="even_in_bwd", grid=(s // TM,),
        in_specs=[_heads(TM)] * 6 + [_rows(TM, EVEN_W), _rows(TM, LANES), _heads(TM), _heads(TM), _whole((1, 512)), _whole((1, 512)),
                                     _whole((EVEN_PACK, D_MODEL)), _rows(TM, D_MODEL), _rows(TM, D_MODEL), _whole((1, D_MODEL))],
        out_specs=[_rows(TM, EVEN_PACK), _rows(TM, D_MODEL), _whole((1, D_MODEL)), _whole((8, 512)), _whole((1, LANES))],
        out_shape=[jax.ShapeDtypeStruct((s, EVEN_PACK), BF16), jax.ShapeDtypeStruct((s, D_MODEL), F32),
                   jax.ShapeDtypeStruct((1, D_MODEL), F32), jax.ShapeDtypeStruct((8, 512), F32), jax.ShapeDtypeStruct((1, LANES), F32)],
        compiler_params=_cparams(("arbitrary",), 52),
    )(dfqs, dfkn, dfv, dsqs, dsk, dsv, dgate, dflog, fqr, fkr, gq, gk, w_pack_t, x, dy1, gnorm)


def matmul_tn(a, b, tn, name):
    s, m = a.shape
    n = b.shape[1]
    tk = 2 * TM
    nk = s // tk

    def body(a_ref, b_ref, o_ref):
        @pl.when(pl.program_id(1) == 0)
        def _():
            o_ref[...] = jnp.zeros_like(o_ref)

        o_ref[...] += _dot_tn(a_ref[...], b_ref[...])

    return pl.pallas_call(
        body, name=name, grid=(n // tn, nk),
        in_specs=[pl.BlockSpec((tk, m), lambda j, k: (k, 0)), pl.BlockSpec((tk, tn), lambda j, k: (k, j))],
        out_specs=pl.BlockSpec((m, tn), lambda j, k: (0, j)),
        out_shape=jax.ShapeDtypeStruct((m, n), F32),
        compiler_params=_cparams(("parallel", "arbitrary"), 32),
    )(a, b)


def _tile_gain(g, reps):
    return jnp.tile(g.reshape(1, -1), (1, reps))


def local_step(x, target, w_in_e, b_f, gq_e, gk_e, gn_e, w_out_e, gn_o, w_in_o, gq_o, gk_o, w_out_o, bq, bk_fox, bk_sb):
    s = x.shape[0]
    r = s // LANES
    w_pack = jnp.concatenate([w_in_e[:, :FL_OFF + NH], jnp.zeros((D_MODEL, LANES - NH), BF16), w_in_e[:, FL_OFF + NH:]], axis=1)
    bf_pad = jnp.pad(b_f.reshape(1, NH), ((0, 0), (0, LANES - NH)))
    gq512, gk512 = _tile_gain(gq_e, NH), _tile_gain(gk_e, NH)
    gq2, gk2 = _tile_gain(gq_o, NH2), _tile_gain(gk_o, NH2)
    gn_e, gn_o = gn_e.reshape(1, D_MODEL), gn_o.reshape(1, D_MODEL)

    h, fqs, fkn, fv, fqr, fkr, flog, sqs, sk, sv, gate = even_in_fwd(x, gn_e, w_pack, bf_pad, gq512, gk512)
    flog4 = flog[:, :NH].T.reshape(NH, r, LANES)
    nc = fox_cum(flog4).reshape(NH, 1, s)
    fo, lse = fox_fwd(fqs, fkn, fv, nc, bq, bk_fox)
    so = sb_fwd(sqs, sk, sv, bq, bk_sb)
    y1 = even_out_fwd(fo, so, gate, x, w_out_e)
    h2, q2s, k2n, v2, q2r, k2r, gate2 = odd_in_fwd(y1, gn_o, w_in_o, gq2, gk2)
    ol = [dil_fwd(q2s, k2n, v2, g) for g in range(NG)]
    att, lse2, dy2, loss_parts = odd_out_fwd(ol[0][0], ol[0][1], ol[1][0], ol[1][1], ol[2][0], ol[2][1], gate2, y1, target, w_out_o)
    loss = jnp.sum(loss_parts[:, 0, 0])
    datt, dgate2, delta2, d_w_out_o = odd_out_bwd(dy2, w_out_o.T, att, gate2)
    dqkv = [dil_bwd(q2s, k2n, v2, datt, lse2, delta2, g) for g in range(NG)]
    dproj2, dy1, d_gn_o, dgain_o = odd_in_bwd([t[0] for t in dqkv], [t[1] for t in dqkv], [t[2] for t in dqkv], dgate2,
                                              q2r, k2r, gq2, gk2, w_in_o.T, y1, dy2, gn_o)
    d_w_in_o = matmul_tn(h2, dproj2, 512, "dw_in_odd")
    dfo, dso, dgate, delta_f, delta_s, d_w_out_e = even_out_bwd(dy1, w_out_e.T, fo, so, gate)
    dfqs, dfkn, dfv, dnc, drow = fox_bwd(fqs, fkn, fv, nc, lse, dfo, delta_f, bq, bk_fox)
    dsqs, dsk, dsv = sb_bwd(sqs, sk, sv, dso, delta_s, bq, bk_sb)
    dcum4 = (drow.reshape(NH, s) - dnc.reshape(NH, s)).reshape(NH, r, LANES)
    dflog4 = fox_cum_bwd(dcum4, flog4)
    dflog = jnp.pad(dflog4.reshape(NH, s).T, ((0, 0), (0, LANES - NH)))
    dproj, grad_x, d_gn_e, dgain_e, d_bf = even_in_bwd(dfqs, dfkn, dfv, dsqs, dsk, dsv, dgate, dflog, fqr, fkr, gq512, gk512,
                                                       w_pack.T, x, dy1, gn_e)
    d_w_pack = matmul_tn(h, dproj, 384, "dw_in_even")
    d_w_in_e = jnp.concatenate([d_w_pack[:, :FL_OFF + NH], d_w_pack[:, FL_OFF + LANES:]], axis=1)
    grads = dict(
        even_norm=d_gn_e.reshape(-1), even_w_in=d_w_in_e, even_b_f=d_bf[0, :NH],
        even_q_gain=dgain_e[0].reshape(NH, HD).sum(0), even_k_gain=dgain_e[1].reshape(NH, HD).sum(0),
        even_w_out=d_w_out_e, odd_norm=d_gn_o.reshape(-1), odd_w_in=d_w_in_o,
        odd_q_gain=dgain_o[:NG].reshape(NG * NH2, HD2).sum(0), odd_k_gain=dgain_o[NG:2 * NG].reshape(NG * NH2, HD2).sum(0),
        odd_w_out=d_w_out_o)
    return loss, grad_x, grads


SHARDED = (("even_w_in", (D_MODEL, EVEN_IN // 4)), ("even_w_out", (EVEN_W // 4, D_MODEL)), ("odd_norm", (D_MODEL // 4,)),
           ("odd_norm_lo", (D_MODEL // 4,)), ("odd_w_in", (D_MODEL, ODD_IN // 4)), ("odd_w_out", (ODD_W, D_MODEL // 4)))
REPLICATED = (("even_norm", (D_MODEL,)), ("even_b_f", (NH,)), ("even_q_gain", (HD,)), ("even_k_gain", (HD,)),
              ("odd_q_gain", (HD2,)), ("odd_k_gain", (HD2,)))
PACK_ELEMS = sum(int(np.prod(shape)) for _, shape in SHARDED + REPLICATED)
PACK_ROWS = -(-PACK_ELEMS // (D_MODEL * 32)) * 32
HALF = PACK_ROWS // 2
HBM = pl.BlockSpec(memory_space=pl.ANY)


def _pack(parts):
    flat = [parts[n].reshape(-1) for n, _ in SHARDED + REPLICATED]
    flat.append(jnp.zeros((PACK_ROWS * D_MODEL - PACK_ELEMS,), flat[0].dtype))
    return jnp.concatenate(flat).reshape(PACK_ROWS, D_MODEL)


def _unpack(buf):
    flat = buf.reshape(-1)
    out, off = {}, 0
    for n, shape in SHARDED + REPLICATED:
        size = int(np.prod(shape))
        out[n] = flat[off:off + size].reshape(shape)
        off += size
    return out


def _place():
    x, y, c = lax.axis_index("x"), lax.axis_index("y"), lax.axis_index("c")
    return x, y, c, [(1 - x, y), (x, 1 - y), (1 - x, 1 - y)]


def all_gather_shards(mine):
    def body(src_ref, out_ref, send_sems, recv_sems, local_sem):
        x, y, c, chips = _place()
        me = 2 * x + y
        half = lambda cc: pl.ds(cc * HALF, HALF)

        def copy(k, j, cc, to, src=None):
            dst = out_ref.at[j, half(cc)]
            return pltpu.make_async_remote_copy(src_ref=dst if src is None else src, dst_ref=dst,
                                                send_sem=send_sems.at[k], recv_sem=recv_sems.at[k],
                                                device_id=to, device_id_type=MESH)

        local = pltpu.make_async_copy(src_ref, out_ref.at[me], local_sem)
        local.start()
        first = [copy(k, me, c, (cx, cy, c), src=src_ref.at[half(c)]) for k, (cx, cy) in enumerate(chips)]
        for cp in first:
            cp.start()
        passed = [copy(3 + k, 2 * cx + cy, c, (x, y, 1 - c)) for k, (cx, cy) in enumerate(chips)]
        for k, (cx, cy) in enumerate(chips):
            copy(k, 2 * cx + cy, c, (x, y, c)).wait_recv()
            passed[k].start()
        for k, (cx, cy) in enumerate(chips):
            copy(3 + k, 2 * cx + cy, 1 - c, (x, y, c)).wait_recv()
        for cp in first + passed:
            cp.wait_send()
        local.wait()

    return pl.pallas_call(
        body, name="all_gather_shards",
        in_specs=[HBM], out_specs=HBM,
        out_shape=jax.ShapeDtypeStruct((4, PACK_ROWS, D_MODEL), mine.dtype),
        scratch_shapes=[pltpu.SemaphoreType.DMA((6,)), pltpu.SemaphoreType.DMA((6,)), pltpu.SemaphoreType.DMA],
    )(mine)


def sibling_swap_halves(g):
    def body(g_ref, a_ref, send_sem, recv_sem):
        x, y, c, _ = _place()
        cp = pltpu.make_async_remote_copy(src_ref=g_ref.at[:, pl.ds((1 - c) * HALF, HALF)], dst_ref=a_ref,
                                          send_sem=send_sem, recv_sem=recv_sem, device_id=(x, y, 1 - c), device_id_type=MESH)
        cp.start()
        cp.wait()

    return pl.pallas_call(
        body, name="sibling_swap_halves",
        in_specs=[HBM], out_specs=HBM,
        out_shape=jax.ShapeDtypeStruct((4, HALF, D_MODEL), g.dtype),
        scratch_shapes=[pltpu.SemaphoreType.DMA, pltpu.SemaphoreType.DMA],
    )(g)


def chip_exchange(p):
    def body(p_ref, b_ref, send_sems, recv_sems, local_sem):
        x, y, c, chips = _place()
        me = 2 * x + y
        local = pltpu.make_async_copy(p_ref.at[me], b_ref.at[me], local_sem)
        local.start()
        sends = [pltpu.make_async_remote_copy(src_ref=p_ref.at[2 * cx + cy], dst_ref=b_ref.at[me],
                                              send_sem=send_sems.at[k], recv_sem=recv_sems.at[k],
                                              device_id=(cx, cy, c), device_id_type=MESH)
                 for k, (cx, cy) in enumerate(chips)]
        for cp in sends:
            cp.start()
        for k, (cx, cy) in enumerate(chips):
            pltpu.make_async_remote_copy(src_ref=p_ref.at[me], dst_ref=b_ref.at[2 * cx + cy],
                                         send_sem=send_sems.at[k], recv_sem=recv_sems.at[k],
                                         device_id=(cx, cy, c), device_id_type=MESH).wait_recv()
        for cp in sends:
            cp.wait_send()
        local.wait()

    return pl.pallas_call(
        body, name="chip_exchange",
        in_specs=[HBM], out_specs=HBM,
        out_shape=jax.ShapeDtypeStruct((4, HALF, D_MODEL), p.dtype),
        scratch_shapes=[pltpu.SemaphoreType.DMA((3,)), pltpu.SemaphoreType.DMA((3,)), pltpu.SemaphoreType.DMA],
    )(p)


def sibling_join_halves(mine):
    def body(h_ref, out_ref, send_sem, recv_sem, local_sem):
        x, y, c, _ = _place()
        local = pltpu.make_async_copy(h_ref, out_ref.at[pl.ds(c * HALF, HALF)], local_sem)
        local.start()
        cp = pltpu.make_async_remote_copy(src_ref=h_ref, dst_ref=out_ref.at[pl.ds(c * HALF, HALF)],
                                          send_sem=send_sem, recv_sem=recv_sem, device_id=(x, y, 1 - c), device_id_type=MESH)
        cp.start()
        cp.wait_send()
        pltpu.make_async_remote_copy(src_ref=h_ref, dst_ref=out_ref.at[pl.ds((1 - c) * HALF, HALF)],
                                     send_sem=send_sem, recv_sem=recv_sem, device_id=(x, y, 1 - c), device_id_type=MESH).wait_recv()
        local.wait()

    return pl.pallas_call(
        body, name="sibling_join_halves",
        in_specs=[HBM], out_specs=HBM,
        out_shape=jax.ShapeDtypeStruct((PACK_ROWS, D_MODEL), mine.dtype),
        scratch_shapes=[pltpu.SemaphoreType.DMA, pltpu.SemaphoreType.DMA, pltpu.SemaphoreType.DMA],
    )(mine)


def _sum_call(name, arrays, rows):
    tr = rows // 5 if rows % 40 == 0 else rows

    def body(*refs):
        acc = refs[0][...]
        for r in refs[1:-1]:
            acc = acc + r[...]
        refs[-1][...] = acc

    spec = pl.BlockSpec((tr, D_MODEL), lambda i: (i, 0))
    return pl.pallas_call(
        body, name=name, grid=(rows // tr,),
        in_specs=[spec] * len(arrays), out_specs=spec,
        out_shape=jax.ShapeDtypeStruct((rows, D_MODEL), F32),
        compiler_params=_cparams(("parallel",), 40),
    )(*arrays)


def adamw(w, g, m, v):
    tr = PACK_ROWS // 5

    def body(w_ref, g_ref, m_ref, v_ref, d_ref, nm_ref, nv_ref):
        gt = g_ref[...]
        nm = ADAM_B1 * m_ref[...] + (1.0 - ADAM_B1) * gt
        nv = ADAM_B2 * v_ref[...] + (1.0 - ADAM_B2) * (gt * gt)
        m_hat = nm / (1.0 - ADAM_B1 ** ADAM_STEP)
        v_hat = nv / (1.0 - ADAM_B2 ** ADAM_STEP)
        d_ref[...] = -ADAM_LR * (m_hat / (jnp.sqrt(v_hat) + ADAM_EPS) + ADAM_WD * w_ref[...])
        nm_ref[...] = nm
        nv_ref[...] = nv

    spec = pl.BlockSpec((tr, D_MODEL), lambda i: (i, 0))
    shape = jax.ShapeDtypeStruct((PACK_ROWS, D_MODEL), F32)
    return pl.pallas_call(
        body, name="adamw", grid=(PACK_ROWS // tr,),
        in_specs=[spec] * 4, out_specs=[spec] * 3, out_shape=[shape] * 3,
        compiler_params=_cparams(("parallel",), 40),
    )(w, g, m, v)


def kernel(x, even_norm, even_w_in, even_b_f, even_q_gain, even_k_gain, even_w_out, odd_norm, odd_w_in, odd_q_gain, odd_k_gain, odd_w_out, loss_target, m_even_norm, m_even_w_in, m_even_b_f, m_even_q_gain, m_even_k_gain, m_even_w_out, m_odd_norm, m_odd_w_in, m_odd_q_gain, m_odd_k_gain, m_odd_w_out, v_even_norm, v_even_w_in, v_even_b_f, v_even_q_gain, v_even_k_gain, v_even_w_out, v_odd_norm, v_odd_w_in, v_odd_q_gain, v_odd_k_gain, v_odd_w_out):
    names = [n for n, _ in SHARDED + REPLICATED if n != "odd_norm_lo"]
    w = dict(even_norm=even_norm, even_w_in=even_w_in, even_b_f=even_b_f, even_q_gain=even_q_gain, even_k_gain=even_k_gain,
             even_w_out=even_w_out, odd_norm=odd_norm, odd_w_in=odd_w_in, odd_q_gain=odd_q_gain, odd_k_gain=odd_k_gain,
             odd_w_out=odd_w_out)
    m = dict(even_norm=m_even_norm, even_w_in=m_even_w_in, even_b_f=m_even_b_f, even_q_gain=m_even_q_gain,
             even_k_gain=m_even_k_gain, even_w_out=m_even_w_out, odd_norm=m_odd_norm, odd_w_in=m_odd_w_in,
             odd_q_gain=m_odd_q_gain, odd_k_gain=m_odd_k_gain, odd_w_out=m_odd_w_out)
    v = dict(even_norm=v_even_norm, even_w_in=v_even_w_in, even_b_f=v_even_b_f, even_q_gain=v_even_q_gain,
             even_k_gain=v_even_k_gain, even_w_out=v_even_w_out, odd_norm=v_odd_norm, odd_w_in=v_odd_w_in,
             odd_q_gain=v_odd_q_gain, odd_k_gain=v_odd_k_gain, odd_w_out=v_odd_w_out)
    spare = jnp.zeros((D_MODEL // 4,), F32)

    on = odd_norm.reshape(-1)
    on_hi = on.astype(BF16)
    wire = {n: w[n].astype(BF16) for n in names}
    wire["odd_norm"] = on_hi
    wire["odd_norm_lo"] = (on - on_hi.astype(F32)).astype(BF16)
    gathered = all_gather_shards(_pack(wire))
    sh = [_unpack(gathered[j]) for j in range(4)]
    cat = lambda n, axis: jnp.concatenate([t[n] for t in sh], axis=axis)
    gn_o = cat("odd_norm", 0).astype(F32) + cat("odd_norm_lo", 0).astype(F32)

    s = x.shape[1]
    bq = min(512, s)
    loss_local, grad_x, g = local_step(
        x[0], loss_target[0], cat("even_w_in", 1), even_b_f[0], even_q_gain[0], even_k_gain[0], even_norm[0],
        cat("even_w_out", 0), gn_o, cat("odd_w_in", 1), odd_q_gain[0], odd_k_gain[0], cat("odd_w_out", 1),
        bq, bq, min(256, s))
    loss = lax.psum(loss_local, ("x", "y", "c"))

    def grad_parts(j):
        parts = {n: g[n] for n, _ in REPLICATED}
        parts["even_w_in"] = g["even_w_in"][:, j * (EVEN_IN // 4):(j + 1) * (EVEN_IN // 4)]
        parts["even_w_out"] = g["even_w_out"][j * (EVEN_W // 4):(j + 1) * (EVEN_W // 4), :]
        parts["odd_norm"] = g["odd_norm"][j * (D_MODEL // 4):(j + 1) * (D_MODEL // 4)]
        parts["odd_norm_lo"] = spare
        parts["odd_w_in"] = g["odd_w_in"][:, j * (ODD_IN // 4):(j + 1) * (ODD_IN // 4)]
        parts["odd_w_out"] = g["odd_w_out"][:, j * (D_MODEL // 4):(j + 1) * (D_MODEL // 4)]
        return parts

    g_all = jnp.stack([_pack(grad_parts(j)) for j in range(4)])
    from_sibling = sibling_swap_halves(g_all)
    c = lax.axis_index("c")
    g_mine = lax.dynamic_slice_in_dim(g_all, c * HALF, HALF, axis=1)
    pair = _sum_call("pair_sum", [g_mine.reshape(4 * HALF, D_MODEL), from_sibling.reshape(4 * HALF, D_MODEL)], 4 * HALF)
    by_chip = chip_exchange(pair.reshape(4, HALF, D_MODEL))
    half_sum = _sum_call("chip_sum", [by_chip[0], by_chip[1], by_chip[2], by_chip[3]], HALF)
    g_buf = sibling_join_halves(half_sum)

    local = lambda d: _pack({**{n: d[n].astype(F32) for n in names}, "odd_norm_lo": spare})
    delta_buf, m_buf, v_buf = adamw(local(w), g_buf, local(m), local(v))
    outs = [loss.reshape(()), grad_x.reshape(x.shape)]
    order = ["even_norm", "even_w_in", "even_b_f", "even_q_gain", "even_k_gain", "even_w_out", "odd_norm", "odd_w_in",
             "odd_q_gain", "odd_k_gain", "odd_w_out"]
    for buf in (g_buf, delta_buf, m_buf, v_buf):
        parts = _unpack(buf)
        outs += [parts[n].reshape(w[n].shape) for n in order]
    return tuple(outs)
```

```python
import jax
import jax.numpy as jnp
import numpy as np
from jax import lax
from jax.experimental import pallas as pl
from jax.experimental.pallas import tpu as pltpu

F32 = jnp.float32
BF16 = jnp.bfloat16

D_MODEL = 1024
HD = 128
NH = 4
HD2 = 64
NG = 3
NH2 = 8
DILATIONS = (1, 4, 16)
SPAN = 128
EVEN_W = 1024
ODD_W = 512
EVEN_IN = 4100
EVEN_PACK = 4224
FL_OFF = 1536
ODD_IN = 5120
RMS_EPS = 1e-6
SCALE_E = HD ** -0.5
SCALE_O = HD2 ** -0.5
ADAM_LR, ADAM_B1, ADAM_B2, ADAM_EPS, ADAM_WD, ADAM_STEP = 0.001, 0.9, 0.999, 1e-08, 0.01, 10

VMEM_CAP = 64 * 1024 * 1024
LANES = 128
MESH = pl.DeviceIdType.MESH


def _cparams(sem, vmem_mb):
    return pltpu.CompilerParams(dimension_semantics=sem, vmem_limit_bytes=min(vmem_mb << 20, VMEM_CAP - (6 << 20)))


def _silu(g):
    return g / (1.0 + jnp.exp(-g))


def _dsilu(g):
    s = 1.0 / (1.0 + jnp.exp(-g))
    return s * (1.0 + g * (1.0 - s))


def _split2(x):
    hi = x.astype(BF16)
    lo = (x - hi.astype(F32)).astype(BF16)
    return hi, lo


def _split3(x):
    hi = x.astype(BF16)
    r = x - hi.astype(F32)
    mid = r.astype(BF16)
    lo = (r - mid.astype(F32)).astype(BF16)
    return hi, mid, lo


def _dotf(a, b):
    return jnp.dot(a, b, preferred_element_type=F32)


def _dot_nt(a, b):
    return lax.dot_general(a, b, (((1,), (1,)), ((), ())), preferred_element_type=F32)


def _dot_tn(a, b):
    return lax.dot_general(a, b, (((0,), (0,)), ((), ())), preferred_element_type=F32)


def _segsum(x, hd):
    r = lax.broadcasted_iota(jnp.int32, (LANES, LANES), 0) // hd
    c = lax.broadcasted_iota(jnp.int32, (LANES, LANES), 1) // hd
    ones = (r == c).astype(BF16)
    outs = []
    for ch in range(x.shape[1] // LANES):
        hi, lo = _split2(x[:, ch * LANES:(ch + 1) * LANES])
        outs.append(_dotf(hi, ones) + _dotf(lo, ones))
    return outs[0] if len(outs) == 1 else jnp.concatenate(outs, axis=1)


def _suffix_matrix(n):
    r = lax.broadcasted_iota(jnp.int32, (n, n), 0)
    c = lax.broadcasted_iota(jnp.int32, (n, n), 1)
    return (r >= c).astype(BF16)


def _walk_up_staged(i, per, stages):
    assert per % 2 == 0
    n_full = i * per

    def pair(first, masked):
        states = [{"j": first}, {"j": first + 1}]
        for stage in stages:
            for st in states:
                stage(st, masked)

    def full_trip(t, c):
        pair(2 * t, False)
        return c

    def masked_trip(d, c):
        pair(n_full + 2 * d, True)
        return c

    lax.fori_loop(0, n_full // 2, full_trip, 0)
    lax.fori_loop(0, per // 2, masked_trip, 0)


def fox_fwd(qs, kn, v, nc, bq, bk):
    nh, s, _ = qs.shape
    nq = s // bq
    per = bq // bk

    def body(q_ref, k_ref, v_ref, nc_ref, o_ref, lse_ref):
        i = pl.program_id(1)
        q = q_ref[...]

        def tile(j, carry, masked):
            m, l, acc = carry
            off = pl.multiple_of(j * bk, bk)
            sc = _dot_nt(q, k_ref[pl.ds(off, bk), :]) + nc_ref[:, pl.ds(off, bk)]
            if masked:
                row = i * bq + lax.broadcasted_iota(jnp.int32, (bq, bk), 0)
                col = off + lax.broadcasted_iota(jnp.int32, (bq, bk), 1)
                sc = jnp.where(col <= row, sc, -jnp.inf)
            m_new = jnp.maximum(m, jnp.max(sc, axis=1, keepdims=True))
            alpha = jnp.exp2(m - m_new)
            p = jnp.exp2(sc - m_new)
            l = alpha * l + jnp.sum(p, axis=1, keepdims=True)
            acc = alpha * acc + _dotf(p.astype(BF16), v_ref[pl.ds(off, bk), :])
            return m_new, l, acc

        init = (jnp.full((bq, 1), -jnp.inf, F32), jnp.zeros((bq, 1), F32), jnp.zeros((bq, HD), F32))
        m, l, acc = _walk_up(i, per, tile, init)
        o_ref[...] = (acc / l).astype(o_ref.dtype)
        lse_ref[...] = m + jnp.log2(l)

    return pl.pallas_call(
        body, name="fox_fwd",
        grid=(nh, nq),
        in_specs=[pl.BlockSpec((None, bq, HD), lambda h, i: (h, i, 0)),
                  pl.BlockSpec((None, s, HD), lambda h, i: (h, 0, 0)),
                  pl.BlockSpec((None, s, HD), lambda h, i: (h, 0, 0)),
                  pl.BlockSpec((None, 1, s), lambda h, i: (h, 0, 0))],
        out_specs=[pl.BlockSpec((None, bq, HD), lambda h, i: (h, i, 0)),
                   pl.BlockSpec((None, bq, 1), lambda h, i: (h, i, 0))],
        out_shape=[jax.ShapeDtypeStruct((nh, s, HD), BF16), jax.ShapeDtypeStruct((nh, s, 1), F32)],
        compiler_params=_cparams(("arbitrary", "arbitrary"), 40),
    )(qs, kn, v, nc)


LOG2E = 1.4426950408889634
LN2 = 0.6931471805599453


def _neg_abs(z):
    sign = jnp.uint32(0x80000000)
    return lax.bitcast_convert_type(lax.bitcast_convert_type(z, jnp.uint32) | sign, F32)


def _sb_softplus2(z, row0, col0, masked):
    u = jnp.maximum(z, 0.0) + jnp.log2(1.0 + jnp.exp2(_neg_abs(z)))
    strict = None
    if masked:
        row = row0 + lax.broadcasted_iota(jnp.int32, z.shape, 0)
        col = col0 + lax.broadcasted_iota(jnp.int32, z.shape, 1)
        strict = col < row
        u = jnp.where(strict, u, 0.0)
    return u, strict


def _walk_down_staged(i, per, stages):
    assert per % 2 == 0
    n_full = i * per

    def pair(top, masked):
        states = [{"j": top}, {"j": top - 1}]
        for stage in stages:
            for st in states:
                stage(st, masked)

    def masked_trip(d, c):
        pair(n_full + per - 1 - 2 * d, True)
        return c

    def full_trip(t, c):
        pair(n_full - 1 - 2 * t, False)
        return c

    lax.fori_loop(0, per // 2, masked_trip, 0)
    lax.fori_loop(0, n_full // 2, full_trip, 0)


def _suffix2(x, m2):
    hi, lo = _split2(x)
    return _dotf(jnp.concatenate([hi, lo], axis=1), m2)


def _lanes(col, n):
    return jnp.broadcast_to(col, (col.shape[0], n))


def _walk_up(i, per, tile, carry):
    n_full = i * per
    carry = lax.fori_loop(0, n_full, lambda j, c: tile(j, c, False), carry)
    return lax.fori_loop(0, per, lambda d, c: tile(n_full + d, c, True), carry)


def sb_fwd(qs, k, v, bq, bk):
    nh, s, _ = qs.shape
    nq = s // bq
    per = bq // bk

    def body(q_ref, k_ref, v_ref, o_ref, lrun_ref):
        i = pl.program_id(1)
        q = q_ref[...]
        tri = _suffix_matrix(bk)
        o_ref[...] = jnp.zeros_like(o_ref)
        lrun_ref[...] = jnp.zeros_like(lrun_ref)

        def logits(st, masked):
            st["off"] = pl.multiple_of(st["j"] * bk, bk)
            st["z"] = _dot_nt(q, k_ref[pl.ds(st["off"], bk), :])

        def suffix(st, masked):
            u, st["strict"] = _sb_softplus2(st["z"], i * bq, st["off"], masked)
            st["incl"] = _dotf(u.astype(BF16), tri)

        def weigh(st, masked):
            lrun = lrun_ref[...]
            w = jnp.exp2(st["z"] - st["incl"] + jnp.tile(lrun, (1, bk // LANES)))
            if masked:
                w = jnp.where(st["strict"], w, 0.0)
            o_ref[...] += _dotf(w.astype(BF16), v_ref[pl.ds(st["off"], bk), :])
            lrun_ref[...] = lrun - _lanes(st["incl"][:, 0:1], LANES)

        _walk_down_staged(i, per, [logits, suffix, weigh])

    return pl.pallas_call(
        body, name="sb_fwd",
        grid=(nh, nq),
        in_specs=[pl.BlockSpec((None, bq, HD), lambda h, i: (h, i, 0)),
                  pl.BlockSpec((None, s, HD), lambda h, i: (h, 0, 0)),
                  pl.BlockSpec((None, s, HD), lambda h, i: (h, 0, 0))],
        out_specs=pl.BlockSpec((None, bq, HD), lambda h, i: (h, i, 0)),
        out_shape=jax.ShapeDtypeStruct((nh, s, HD), F32),
        scratch_shapes=[pltpu.VMEM((bq, LANES), F32)],
        compiler_params=_cparams(("arbitrary", "arbitrary"), 40),
    )(qs, k, v)


def _attn_bwd_call(name, body, s, bq, ins, in_specs, extra_out_specs, extra_out_shapes, extra_scratch=()):
    nh = NH
    nq = s // bq
    return pl.pallas_call(
        body, name=name,
        grid=(nh, nq),
        in_specs=in_specs,
        out_specs=[pl.BlockSpec((None, bq, HD), lambda h, i: (h, i, 0)),
                   pl.BlockSpec(memory_space=pl.ANY), pl.BlockSpec(memory_space=pl.ANY)] + extra_out_specs,
        out_shape=[jax.ShapeDtypeStruct((nh, s, HD), F32), jax.ShapeDtypeStruct((nh, s, HD), F32),
                   jax.ShapeDtypeStruct((nh, s, HD), F32)] + extra_out_shapes,
        scratch_shapes=[pltpu.VMEM((s, HD), F32), pltpu.VMEM((s, HD), F32), pltpu.SemaphoreType.DMA((2,))] + list(extra_scratch),
        compiler_params=_cparams(("arbitrary", "arbitrary"), 52),
    )(*ins)


def _flush_dkv(i, nq, h, dk_acc, dv_acc, dk_hbm, dv_hbm, sems):
    @pl.when(i == nq - 1)
    def _():
        ck = pltpu.make_async_copy(dk_acc, dk_hbm.at[h], sems.at[0])
        cv = pltpu.make_async_copy(dv_acc, dv_hbm.at[h], sems.at[1])
        ck.start()
        cv.start()
        ck.wait()
        cv.wait()


def fox_bwd(qs, kn, v, nc, lse, do, delta, bq, bk):
    nh, s, _ = qs.shape
    nq = s // bq
    per = bq // bk

    def body(q_ref, k_ref, v_ref, nc_ref, lse_ref, do_ref, dl_ref, dq_ref, dk_hbm, dv_hbm, dnc_ref, drow_ref, dk_acc, dv_acc, sems):
        h, i = pl.program_id(0), pl.program_id(1)

        @pl.when(i == 0)
        def _():
            dk_acc[...] = jnp.zeros_like(dk_acc)
            dv_acc[...] = jnp.zeros_like(dv_acc)
            dnc_ref[...] = jnp.zeros_like(dnc_ref)

        q = q_ref[...]
        do_t = do_ref[...]
        lse_t = lse_ref[...]
        dl_t = dl_ref[...]

        dq_ref[...] = jnp.zeros_like(dq_ref)
        drow_ref[...] = jnp.zeros_like(drow_ref)

        def logits(st, masked):
            st["off"] = pl.multiple_of(st["j"] * bk, bk)
            st["sc"] = _dot_nt(q, k_ref[pl.ds(st["off"], bk), :]) + nc_ref[:, pl.ds(st["off"], bk)]
            st["dp"] = _dot_nt(do_t, v_ref[pl.ds(st["off"], bk), :])

        def grads(st, masked):
            off = st["off"]
            p = jnp.exp2(st["sc"] - lse_t)
            if masked:
                row = i * bq + lax.broadcasted_iota(jnp.int32, (bq, bk), 0)
                col = off + lax.broadcasted_iota(jnp.int32, (bq, bk), 1)
                p = jnp.where(col <= row, p, 0.0)
            ds = p * (st["dp"] - dl_t)
            dsb = ds.astype(BF16)
            dq_ref[...] += _dotf(dsb, k_ref[pl.ds(off, bk), :])
            dk_acc[pl.ds(off, bk), :] += _dot_tn(dsb, q)
            dv_acc[pl.ds(off, bk), :] += _dot_tn(p.astype(BF16), do_t)
            dnc_ref[:, pl.ds(off, bk)] += jnp.sum(ds, axis=0, keepdims=True)
            drow_ref[...] += jnp.sum(ds, axis=1, keepdims=True)

        _walk_up_staged(i, per, [logits, grads])
        _flush_dkv(i, nq, h, dk_acc, dv_acc, dk_hbm, dv_hbm, sems)

    tile_spec = pl.BlockSpec((None, bq, HD), lambda h, i: (h, i, 0))
    col_spec = pl.BlockSpec((None, bq, 1), lambda h, i: (h, i, 0))
    full_spec = pl.BlockSpec((None, s, HD), lambda h, i: (h, 0, 0))
    row_spec = pl.BlockSpec((None, 1, s), lambda h, i: (h, 0, 0))
    return _attn_bwd_call("fox_bwd", body, s, bq, (qs, kn, v, nc, lse, do, delta),
                          [tile_spec, full_spec, full_spec, row_spec, col_spec, tile_spec, col_spec],
                          [row_spec, col_spec],
                          [jax.ShapeDtypeStruct((nh, 1, s), F32), jax.ShapeDtypeStruct((nh, s, 1), F32)])


def sb_bwd(qs, k, v, do, delta, bq, bk):
    nh, s, _ = qs.shape
    nq = s // bq
    per = bq // bk

    def body(q_ref, k_ref, v_ref, do_ref, dl_ref, dq_ref, dk_hbm, dv_hbm, dk_acc, dv_acc, sems, lrun_ref, crun_ref):
        h, i = pl.program_id(0), pl.program_id(1)

        @pl.when(i == 0)
        def _():
            dk_acc[...] = jnp.zeros_like(dk_acc)
            dv_acc[...] = jnp.zeros_like(dv_acc)

        q = q_ref[...]
        do_t = do_ref[...]
        tri = _suffix_matrix(bk)
        tri2 = jnp.concatenate([tri, tri], axis=0)
        dq_ref[...] = jnp.zeros_like(dq_ref)
        lrun_ref[...] = jnp.zeros_like(lrun_ref)
        crun_ref[...] = _lanes(dl_ref[...], LANES)

        def logits(st, masked):
            st["off"] = pl.multiple_of(st["j"] * bk, bk)
            st["z"] = _dot_nt(q, k_ref[pl.ds(st["off"], bk), :])
            st["dw"] = _dot_nt(do_t, v_ref[pl.ds(st["off"], bk), :])

        def suffix(st, masked):
            st["u"], st["strict"] = _sb_softplus2(st["z"], i * bq, st["off"], masked)
            st["incl"] = _dotf(st["u"].astype(BF16), tri)

        def weigh(st, masked):
            lrun = lrun_ref[...]
            w = jnp.exp2(st["z"] - st["incl"] + jnp.tile(lrun, (1, bk // LANES)))
            if masked:
                w = jnp.where(st["strict"], w, 0.0)
            st["wb"] = w.astype(BF16)
            st["e"] = st["dw"] * st["wb"].astype(F32)
            st["einc"] = _suffix2(st["e"], tri2)
            lrun_ref[...] = lrun - _lanes(st["incl"][:, 0:1], LANES)

        def grads(st, masked):
            crun = crun_ref[...]
            prefix = jnp.tile(crun, (1, bk // LANES)) - st["einc"]
            dz = st["e"] - jnp.exp2(st["z"] - st["u"]) * (st["e"] + prefix)
            if masked:
                dz = jnp.where(st["strict"], dz, 0.0)
            dzb = dz.astype(BF16)
            dq_ref[...] += _dotf(dzb, k_ref[pl.ds(st["off"], bk), :])
            dk_acc[pl.ds(st["off"], bk), :] += _dot_tn(dzb, q)
            dv_acc[pl.ds(st["off"], bk), :] += _dot_tn(st["wb"], do_t)
            crun_ref[...] = crun - _lanes(st["einc"][:, 0:1], LANES)

        _walk_down_staged(i, per, [logits, suffix, weigh, grads])
        _flush_dkv(i, nq, h, dk_acc, dv_acc, dk_hbm, dv_hbm, sems)

    tile_spec = pl.BlockSpec((None, bq, HD), lambda h, i: (h, i, 0))
    col_spec = pl.BlockSpec((None, bq, 1), lambda h, i: (h, i, 0))
    full_spec = pl.BlockSpec((None, s, HD), lambda h, i: (h, 0, 0))
    return _attn_bwd_call("sb_bwd", body, s, bq, (qs, k, v, do, delta),
                          [tile_spec, full_spec, full_spec, tile_spec, col_spec], [], [],
                          [pltpu.VMEM((bq, LANES), F32), pltpu.VMEM((bq, LANES), F32)])


BI = SPAN
PAIR = 2 * HD2


def _slopes(g):
    return [float(2.0 ** (-8.0 * (g * NH2 + h + 1) / (NG * NH2))) for h in range(NH2)]


def _head_lanes(hh):
    return (lax.broadcasted_iota(jnp.int32, (1, PAIR), 1) // HD2) == hh


def dil_fwd(q2s, k2n, v2, g):
    d = DILATIONS[g]
    _, s, _ = q2s.shape
    sub = s // d
    nblk = sub // BI
    slopes = _slopes(g)
    view = lambda t: t.reshape(NG, sub, d * ODD_W)

    def body(q_ref, kc_ref, kp_ref, vc_ref, vp_ref, o_ref, lse_ref):
        n = pl.program_id(1)
        a = lax.broadcasted_iota(jnp.int32, (BI, 2 * BI), 0)
        c = lax.broadcasted_iota(jnp.int32, (BI, 2 * BI), 1)
        dist = a - c + BI
        valid = (dist >= 0) & (dist <= SPAN) & ((c >= BI) | (n > 0))
        distf = (dist * d).astype(F32)
        for hp in range(NH2 // 2):
            sl = slice(hp * PAIR, (hp + 1) * PAIR)
            qp = q_ref[:, sl]
            kcat = jnp.concatenate([kp_ref[:, sl], kc_ref[:, sl]], axis=0)
            vcat = jnp.concatenate([vp_ref[:, sl], vc_ref[:, sl]], axis=0)
            o_pair = jnp.zeros((BI, PAIR), F32)
            lse_pair = jnp.zeros((BI, PAIR), F32)
            for hh in range(2):
                lm = _head_lanes(hh)
                qm = jnp.where(lm, qp, jnp.zeros_like(qp))
                logits = _dot_nt(qm, kcat) - slopes[hp * 2 + hh] * distf
                logits = jnp.where(valid, logits, -jnp.inf)
                m = jnp.max(logits, axis=1, keepdims=True)
                p = jnp.exp(logits - m)
                den = jnp.sum(p, axis=1, keepdims=True)
                oh = _dotf(p.astype(BF16), vcat) / den
                o_pair = jnp.where(lm, oh, o_pair)
                lse_pair = jnp.where(lm, m + jnp.log(den), lse_pair)
            o_ref[:, sl] = o_pair
            lse_ref[:, sl] = lse_pair

    cur = pl.BlockSpec((None, BI, ODD_W), lambda r, n: (g, n, r))
    prev = pl.BlockSpec((None, BI, ODD_W), lambda r, n: (g, jnp.maximum(n - 1, 0), r))
    out = pl.BlockSpec((BI, ODD_W), lambda r, n: (n, r))
    o, lse = pl.pallas_call(
        body, name=f"dil_fwd_{g}",
        grid=(d, nblk),
        in_specs=[cur, cur, prev, cur, prev],
        out_specs=[out, out],
        out_shape=[jax.ShapeDtypeStruct((sub, d * ODD_W), F32), jax.ShapeDtypeStruct((sub, d * ODD_W), F32)],
        compiler_params=_cparams(("parallel", "parallel"), 32),
    )(view(q2s), view(k2n), view(k2n), view(v2), view(v2))
    return o.reshape(s, ODD_W), lse.reshape(s, ODD_W)


def dil_bwd(q2s, k2n, v2, do, lse, delta, g):
    d = DILATIONS[g]
    _, s, _ = q2s.shape
    sub = s // d
    nblk = sub // BI
    slopes = _slopes(g)
    view3 = lambda t: t.reshape(NG, sub, d * ODD_W)
    view2 = lambda t: t.reshape(sub, d * ODD_W)

    def body(qc_ref, qn_ref, kc_ref, kp_ref, vc_ref, vp_ref, doc_ref, don_ref, lc_ref, ln_ref, dc_ref, dn_ref,
             dq_ref, dk_ref, dv_ref):
        n = pl.program_id(1)
        a = lax.broadcasted_iota(jnp.int32, (BI, 2 * BI), 0)
        c = lax.broadcasted_iota(jnp.int32, (BI, 2 * BI), 1)
        dist = a - c + BI
        valid = (dist >= 0) & (dist <= SPAN) & ((c >= BI) | (n > 0))
        distf = (dist * d).astype(F32)
        a2 = lax.broadcasted_iota(jnp.int32, (2 * BI, BI), 0)
        c2 = lax.broadcasted_iota(jnp.int32, (2 * BI, BI), 1)
        dist2 = a2 - c2
        valid2 = (dist2 >= 0) & (dist2 <= SPAN) & ((a2 < BI) | (n < nblk - 1))
        dist2f = (dist2 * d).astype(F32)
        for hp in range(NH2 // 2):
            sl = slice(hp * PAIR, (hp + 1) * PAIR)
            qc = qc_ref[:, sl]
            doc = doc_ref[:, sl]
            qcat = jnp.concatenate([qc, qn_ref[:, sl]], axis=0)
            docat = jnp.concatenate([doc, don_ref[:, sl]], axis=0)
            kc = kc_ref[:, sl]
            vc = vc_ref[:, sl]
            kcat = jnp.concatenate([kp_ref[:, sl], kc], axis=0)
            vcat = jnp.concatenate([vp_ref[:, sl], vc], axis=0)
            dq_pair = jnp.zeros((BI, PAIR), F32)
            dk_pair = jnp.zeros((BI, PAIR), F32)
            dv_pair = jnp.zeros((BI, PAIR), F32)
            for hh in range(2):
                h = hp * 2 + hh
                col = slice(h * HD2, h * HD2 + 1)
                lm = _head_lanes(hh)
                zq = jnp.zeros_like(qc)
                lse_c = lc_ref[:, col]
                dl_c = dc_ref[:, col]
                logits = _dot_nt(jnp.where(lm, qc, zq), kcat) - slopes[h] * distf
                p = jnp.exp(jnp.where(valid, logits, -jnp.inf) - lse_c)
                dp = _dot_nt(jnp.where(lm, doc, zq), vcat)
                ds = (p * (dp - dl_c)).astype(BF16)
                dq_pair = jnp.where(lm, _dotf(ds, kcat), dq_pair)

                zcat = jnp.zeros_like(qcat)
                qm = jnp.where(lm, qcat, zcat)
                dom = jnp.where(lm, docat, zcat)
                lse2 = jnp.concatenate([lse_c, ln_ref[:, col]], axis=0)
                dl2 = jnp.concatenate([dl_c, dn_ref[:, col]], axis=0)
                logits2 = _dot_nt(qm, kc) - slopes[h] * dist2f
                p2 = jnp.exp(jnp.where(valid2, logits2, -jnp.inf) - lse2)
                dp2 = _dot_nt(dom, vc)
                ds2 = (p2 * (dp2 - dl2)).astype(BF16)
                dk_pair = dk_pair + _dot_tn(ds2, qm)
                dv_pair = dv_pair + _dot_tn(p2.astype(BF16), dom)
            dq_ref[:, sl] = dq_pair
            dk_ref[:, sl] = dk_pair
            dv_ref[:, sl] = dv_pair.astype(dv_ref.dtype)

    nxt_idx = lambda n: jnp.minimum(n + 1, nblk - 1)
    prv_idx = lambda n: jnp.maximum(n - 1, 0)
    cur3 = pl.BlockSpec((None, BI, ODD_W), lambda r, n: (g, n, r))
    nxt3 = pl.BlockSpec((None, BI, ODD_W), lambda r, n: (g, nxt_idx(n), r))
    prv3 = pl.BlockSpec((None, BI, ODD_W), lambda r, n: (g, prv_idx(n), r))
    cur2 = pl.BlockSpec((BI, ODD_W), lambda r, n: (n, r))
    nxt2 = pl.BlockSpec((BI, ODD_W), lambda r, n: (nxt_idx(n), r))
    shape = lambda dt: jax.ShapeDtypeStruct((sub, d * ODD_W), dt)
    dq, dk, dv = pl.pallas_call(
        body, name=f"dil_bwd_{g}",
        grid=(d, nblk),
        in_specs=[cur3, nxt3, cur3, prv3, cur3, prv3, cur2, nxt2, cur2, nxt2, cur2, nxt2],
        out_specs=[cur2, cur2, cur2],
        out_shape=[shape(F32), shape(F32), shape(BF16)],
        compiler_params=_cparams(("parallel", "parallel"), 32),
    )(view3(q2s), view3(q2s), view3(k2n), view3(k2n), view3(v2), view3(v2),
      view2(do), view2(do), view2(lse), view2(lse), view2(delta), view2(delta))
    return dq.reshape(s, ODD_W), dk.reshape(s, ODD_W), dv.reshape(s, ODD_W)


TM = 256


def _rows(tm, w):
    return pl.BlockSpec((tm, w), lambda i: (i, 0))


def _whole(shape):
    return pl.BlockSpec(shape, lambda i: (0,) * len(shape))


def _heads(tm):
    return pl.BlockSpec((NH, tm, HD), lambda i: (0, i, 0))


def _groups(tm):
    return pl.BlockSpec((NG, tm, ODD_W), lambda i: (0, i, 0))


def _rms(x):
    return lax.rsqrt(jnp.mean(x * x, axis=1, keepdims=True) + RMS_EPS)


def _seg_rms(q, hd):
    return lax.rsqrt(_segsum(q * q, hd) * (1.0 / hd) + RMS_EPS)


def _seg_rms_bwd(q_raw, dqs, gain, scale, hd):
    q = q_raw.astype(F32)
    r = _seg_rms(q, hd)
    qhat = q * r
    u = dqs * (gain * scale)
    dq = r * (u - qhat * (_segsum(u * qhat, hd) * (1.0 / hd)))
    return dq, jnp.sum(dqs * qhat, axis=0, keepdims=True) * scale


def _rms_bwd(x, dh, gain):
    r = _rms(x)
    xhat = x * r
    u = dh * gain
    dx = r * (u - xhat * jnp.mean(u * xhat, axis=1, keepdims=True))
    return dx, jnp.sum(dh * xhat, axis=0, keepdims=True)


def even_in_fwd(x, gnorm, w_pack, bf_pad, gq, gk):
    s = x.shape[0]

    def body(x_ref, g_ref, w_ref, bf_ref, gq_ref, gk_ref,
             h_ref, fqs_ref, fkn_ref, fv_ref, fqr_ref, fkr_ref, flog_ref, sqs_ref, sk_ref, sv_ref, gate_ref):
        xt = x_ref[...]
        h = (xt * _rms(xt) * g_ref[...]).astype(BF16)
        h_ref[...] = h
        proj = _dotf(h, w_ref[...])
        fq = proj[:, 0:512]
        fk = proj[:, 512:1024]
        fqs = fq * _seg_rms(fq, HD) * (gq_ref[...] * (SCALE_E * LOG2E))
        fkn = fk * _seg_rms(fk, HD) * gk_ref[...]
        flog_ref[...] = proj[:, FL_OFF:FL_OFF + LANES] + bf_ref[...]
        o = FL_OFF + LANES
        for hh in range(NH):
            sl = slice(hh * HD, (hh + 1) * HD)
            fqs_ref[hh] = fqs[:, sl].astype(BF16)
            fkn_ref[hh] = fkn[:, sl].astype(BF16)
            fqr_ref[hh] = fq[:, sl].astype(BF16)
            fkr_ref[hh] = fk[:, sl].astype(BF16)
            fv_ref[hh] = proj[:, 1024 + hh * HD:1024 + (hh + 1) * HD].astype(BF16)
            sqs_ref[hh] = (proj[:, o + hh * HD:o + (hh + 1) * HD] * (SCALE_E * LOG2E)).astype(BF16)
            sk_ref[hh] = proj[:, o + 512 + hh * HD:o + 512 + (hh + 1) * HD].astype(BF16)
            sv_ref[hh] = proj[:, o + 1024 + hh * HD:o + 1024 + (hh + 1) * HD].astype(BF16)
        gate_ref[...] = proj[:, o + 1536:o + 2560].astype(BF16)

    hs = jax.ShapeDtypeStruct((NH, s, HD), BF16)
    return pl.pallas_call(
        body, name="even_in_fwd",
        grid=(s // TM,),
        in_specs=[_rows(TM, D_MODEL), _whole((1, D_MODEL)), _whole((D_MODEL, EVEN_PACK)), _whole((1, LANES)),
                  _whole((1, 512)), _whole((1, 512))],
        out_specs=[_rows(TM, D_MODEL)] + [_heads(TM)] * 5 + [_rows(TM, LANES)] + [_heads(TM)] * 3 + [_rows(TM, EVEN_W)],
        out_shape=[jax.ShapeDtypeStruct((s, D_MODEL), BF16)] + [hs] * 5 + [jax.ShapeDtypeStruct((s, LANES), F32)]
        + [hs] * 3 + [jax.ShapeDtypeStruct((s, EVEN_W), BF16)],
        compiler_params=_cparams(("parallel",), 52),
    )(x, gnorm, w_pack, bf_pad, gq, gk)


def _prefix_matrices(r):
    a = lax.broadcasted_iota(jnp.int32, (LANES, LANES), 0)
    b = lax.broadcasted_iota(jnp.int32, (LANES, LANES), 1)
    ra = lax.broadcasted_iota(jnp.int32, (r, r), 0)
    rb = lax.broadcasted_iota(jnp.int32, (r, r), 1)
    return a, b, ra, rb


def _dot3_right(x, m):
    a, b, c = _split3(x)
    return _dotf(a, m) + _dotf(b, m) + _dotf(c, m)


def _dot3_left(m, x):
    a, b, c = _split3(x)
    return _dotf(m, a) + _dotf(m, b) + _dotf(m, c)


def fox_cum(flog4):
    nh, r, _ = flog4.shape

    def body(f_ref, nc_ref):
        z = f_ref[...]
        lf = jnp.minimum(z, 0.0) - jnp.log(1.0 + jnp.exp(-jnp.abs(z)))
        a, b, ra, rb = _prefix_matrices(r)
        within = _dot3_right(lf, (a <= b).astype(BF16))
        tot = jnp.broadcast_to(within[:, LANES - 1:LANES], (r, LANES))
        nc_ref[...] = (within + _dot3_left((rb < ra).astype(BF16), tot)) * (-LOG2E)

    return pl.pallas_call(
        body, name="fox_cum", grid=(nh,),
        in_specs=[pl.BlockSpec((None, r, LANES), lambda h: (h, 0, 0))],
        out_specs=pl.BlockSpec((None, r, LANES), lambda h: (h, 0, 0)),
        out_shape=jax.ShapeDtypeStruct((nh, r, LANES), F32),
        compiler_params=_cparams(("parallel",), 16),
    )(flog4)


def fox_cum_bwd(dcum4, flog4):
    nh, r, _ = flog4.shape

    def body(d_ref, f_ref, o_ref):
        a, b, ra, rb = _prefix_matrices(r)
        dc = d_ref[...]
        within = _dot3_right(dc, (a >= b).astype(BF16))
        tot = jnp.broadcast_to(within[:, 0:1], (r, LANES))
        dlf = within + _dot3_left((rb > ra).astype(BF16), tot)
        o_ref[...] = dlf / (1.0 + jnp.exp(f_ref[...]))

    spec = pl.BlockSpec((None, r, LANES), lambda h: (h, 0, 0))
    return pl.pallas_call(
        body, name="fox_cum_bwd", grid=(nh,),
        in_specs=[spec, spec], out_specs=spec,
        out_shape=jax.ShapeDtypeStruct((nh, r, LANES), F32),
        compiler_params=_cparams(("parallel",), 16),
    )(dcum4, flog4)


def even_out_fwd(fo, so, gate, x, w_out):
    s = x.shape[0]
    tm = 2 * TM

    def body(fo_ref, so_ref, g_ref, x_ref, w_ref, y_ref):
        sg = _silu(g_ref[...].astype(F32))
        acc = x_ref[...]
        for hh in range(NH):
            mf = (fo_ref[hh].astype(F32) * sg[:, hh * HD:(hh + 1) * HD]).astype(BF16)
            ms = (so_ref[hh] * sg[:, 512 + hh * HD:512 + (hh + 1) * HD]).astype(BF16)
            acc = acc + _dotf(mf, w_ref[hh * HD:(hh + 1) * HD, :]) + _dotf(ms, w_ref[512 + hh * HD:512 + (hh + 1) * HD, :])
        y_ref[...] = acc

    return pl.pallas_call(
        body, name="even_out_fwd", grid=(s // tm,),
        in_specs=[_heads(tm), _heads(tm), _rows(tm, EVEN_W), _rows(tm, D_MODEL), _whole((EVEN_W, D_MODEL))],
        out_specs=_rows(tm, D_MODEL),
        out_shape=jax.ShapeDtypeStruct((s, D_MODEL), F32),
        compiler_params=_cparams(("parallel",), 40),
    )(fo, so, gate, x, w_out)


def odd_in_fwd(y1, gnorm, w2, gq, gk):
    s = y1.shape[0]

    def body(x_ref, g_ref, w_ref, gq_ref, gk_ref, h_ref, qs_ref, kn_ref, v_ref, qr_ref, kr_ref, gate_ref):
        xt = x_ref[...]
        h = (xt * _rms(xt) * g_ref[...]).astype(BF16)
        h_ref[...] = h
        proj = _dotf(h, w_ref[...])
        for g in range(NG):
            q = proj[:, g * ODD_W:(g + 1) * ODD_W]
            k = proj[:, 1536 + g * ODD_W:1536 + (g + 1) * ODD_W]
            qs_ref[g] = (q * _seg_rms(q, HD2) * (gq_ref[...] * SCALE_O)).astype(BF16)
            kn_ref[g] = (k * _seg_rms(k, HD2) * gk_ref[...]).astype(BF16)
            qr_ref[g] = q.astype(BF16)
            kr_ref[g] = k.astype(BF16)
            v_ref[g] = proj[:, 3072 + g * ODD_W:3072 + (g + 1) * ODD_W].astype(BF16)
        gate_ref[...] = proj[:, 4608:5120].astype(BF16)

    gs = jax.ShapeDtypeStruct((NG, s, ODD_W), BF16)
    return pl.pallas_call(
        body, name="odd_in_fwd", grid=(s // TM,),
        in_specs=[_rows(TM, D_MODEL), _whole((1, D_MODEL)), _whole((D_MODEL, ODD_IN)), _whole((1, ODD_W)), _whole((1, ODD_W))],
        out_specs=[_rows(TM, D_MODEL)] + [_groups(TM)] * 5 + [_rows(TM, ODD_W)],
        out_shape=[jax.ShapeDtypeStruct((s, D_MODEL), BF16)] + [gs] * 5 + [jax.ShapeDtypeStruct((s, ODD_W), BF16)],
        compiler_params=_cparams(("parallel",), 52),
    )(y1, gnorm, w2, gq, gk)


def odd_out_fwd(o0, l0, o1, l1, o2, l2, gate2, y1, target, w_out2):
    s = y1.shape[0]
    tm = 2 * TM
    nt = s // tm

    def body(o0_ref, l0_ref, o1_ref, l1_ref, o2_ref, l2_ref, g_ref, y1_ref, t_ref, w_ref,
             att_ref, lse_ref, dy_ref, loss_ref):
        l0t, l1t, l2t = l0_ref[...], l1_ref[...], l2_ref[...]
        m = jnp.maximum(jnp.maximum(l0t, l1t), l2t)
        e0, e1, e2 = jnp.exp(l0t - m), jnp.exp(l1t - m), jnp.exp(l2t - m)
        den = e0 + e1 + e2
        att = (e0 * o0_ref[...] + e1 * o1_ref[...] + e2 * o2_ref[...]) / den
        att_ref[...] = att.astype(BF16)
        lse_ref[...] = m + jnp.log(den)
        mixed = (att * _silu(g_ref[...].astype(F32))).astype(BF16)
        diff = y1_ref[...] + _dotf(mixed, w_ref[...]) - t_ref[...]
        dy_ref[...] = diff * (1.0 / D_MODEL)
        loss_ref[...] = jnp.full((1, 1, LANES), 0.5 / D_MODEL, F32) * jnp.sum(diff * diff)

    big = jax.ShapeDtypeStruct((s, ODD_W), F32)
    return pl.pallas_call(
        body, name="odd_out_fwd", grid=(nt,),
        in_specs=[_rows(tm, ODD_W)] * 7 + [_rows(tm, D_MODEL), _rows(tm, D_MODEL), _whole((ODD_W, D_MODEL))],
        out_specs=[_rows(tm, ODD_W), _rows(tm, ODD_W), _rows(tm, D_MODEL), pl.BlockSpec((1, 1, LANES), lambda i: (i, 0, 0))],
        out_shape=[jax.ShapeDtypeStruct((s, ODD_W), BF16), big, jax.ShapeDtypeStruct((s, D_MODEL), F32),
                   jax.ShapeDtypeStruct((nt, 1, LANES), F32)],
        compiler_params=_cparams(("parallel",), 40),
    )(o0, l0, o1, l1, o2, l2, gate2, y1, target, w_out2)


def odd_out_bwd(dy2, w_out2_t, att, gate2):
    s = dy2.shape[0]
    tm = 2 * TM

    def body(dy_ref, wt_ref, att_ref, g_ref, datt_ref, dgate_ref, delta_ref, dw_ref):
        @pl.when(pl.program_id(0) == 0)
        def _():
            dw_ref[...] = jnp.zeros_like(dw_ref)

        dyb = dy_ref[...].astype(BF16)
        dmixed = _dotf(dyb, wt_ref[...])
        g = g_ref[...].astype(F32)
        att_t = att_ref[...].astype(F32)
        sg = _silu(g)
        datt = (dmixed * sg).astype(BF16)
        datt_ref[...] = datt
        dgate_ref[...] = (dmixed * att_t * _dsilu(g)).astype(BF16)
        delta_ref[...] = _segsum(datt.astype(F32) * att_t, HD2)
        dw_ref[...] += _dot_tn((att_t * sg).astype(BF16), dyb)

    return pl.pallas_call(
        body, name="odd_out_bwd", grid=(s // tm,),
        in_specs=[_rows(tm, D_MODEL), _whole((D_MODEL, ODD_W)), _rows(tm, ODD_W), _rows(tm, ODD_W)],
        out_specs=[_rows(tm, ODD_W), _rows(tm, ODD_W), _rows(tm, ODD_W), _whole((ODD_W, D_MODEL))],
        out_shape=[jax.ShapeDtypeStruct((s, ODD_W), BF16), jax.ShapeDtypeStruct((s, ODD_W), BF16),
                   jax.ShapeDtypeStruct((s, ODD_W), F32), jax.ShapeDtypeStruct((ODD_W, D_MODEL), F32)],
        compiler_params=_cparams(("arbitrary",), 40),
    )(dy2, w_out2_t, att, gate2)


def odd_in_bwd(dqs, dks, dvs, dgate2, q2r, k2r, gq, gk, w2_t, y1, dy2, gnorm):
    s = y1.shape[0]

    def body(dq0, dq1, dq2, dk0, dk1, dk2, dv0, dv1, dv2, dg_ref, qr_ref, kr_ref, gq_ref, gk_ref, wt_ref, y1_ref, dy_ref, gn_ref,
             dproj_ref, dy1_ref, dgn_ref, dgain_ref):
        @pl.when(pl.program_id(0) == 0)
        def _():
            dgn_ref[...] = jnp.zeros_like(dgn_ref)
            dgain_ref[...] = jnp.zeros_like(dgain_ref)

        for g, (dq_ref, dk_ref, dv_ref) in enumerate(((dq0, dk0, dv0), (dq1, dk1, dv1), (dq2, dk2, dv2))):
            dq, gq_row = _seg_rms_bwd(qr_ref[g], dq_ref[...], gq_ref[...], SCALE_O, HD2)
            dk, gk_row = _seg_rms_bwd(kr_ref[g], dk_ref[...], gk_ref[...], 1.0, HD2)
            dproj_ref[:, g * ODD_W:(g + 1) * ODD_W] = dq.astype(BF16)
            dproj_ref[:, 1536 + g * ODD_W:1536 + (g + 1) * ODD_W] = dk.astype(BF16)
            dproj_ref[:, 3072 + g * ODD_W:3072 + (g + 1) * ODD_W] = dv_ref[...]
            dgain_ref[g:g + 1, :] += gq_row
            dgain_ref[NG + g:NG + g + 1, :] += gk_row
        dproj_ref[:, 4608:5120] = dg_ref[...]
        dh = _dotf(dproj_ref[...], wt_ref[...])
        dx, gn_row = _rms_bwd(y1_ref[...], dh, gn_ref[...])
        dy1_ref[...] = dy_ref[...] + dx
        dgn_ref[...] += gn_row

    f32r, bf16r = _rows(TM, ODD_W), _rows(TM, ODD_W)
    return pl.pallas_call(
        body, name="odd_in_bwd", grid=(s // TM,),
        in_specs=[f32r] * 6 + [bf16r] * 4 + [_groups(TM), _groups(TM), _whole((1, ODD_W)), _whole((1, ODD_W)),
                                             _whole((ODD_IN, D_MODEL)), _rows(TM, D_MODEL), _rows(TM, D_MODEL), _whole((1, D_MODEL))],
        out_specs=[_rows(TM, ODD_IN), _rows(TM, D_MODEL), _whole((1, D_MODEL)), _whole((8, ODD_W))],
        out_shape=[jax.ShapeDtypeStruct((s, ODD_IN), BF16), jax.ShapeDtypeStruct((s, D_MODEL), F32),
                   jax.ShapeDtypeStruct((1, D_MODEL), F32), jax.ShapeDtypeStruct((8, ODD_W), F32)],
        compiler_params=_cparams(("arbitrary",), 52),
    )(*dqs, *dks, *dvs, dgate2, q2r, k2r, gq, gk, w2_t, y1, dy2, gnorm)


def even_out_bwd(dy1, w_out_t, fo, so, gate):
    s = dy1.shape[0]
    tm = 2 * TM

    def body(dy_ref, wt_ref, fo_ref, so_ref, g_ref, dfo_ref, dso_ref, dgate_ref, delf_ref, dels_ref, dw_ref):
        @pl.when(pl.program_id(0) == 0)
        def _():
            dw_ref[...] = jnp.zeros_like(dw_ref)

        dyb = dy_ref[...].astype(BF16)
        dmixed = _dotf(dyb, wt_ref[...])
        g = g_ref[...].astype(F32)
        sg, dsg = _silu(g), _dsilu(g)
        for hh in range(NH):
            for base, o_ref, do_ref, del_ref in ((0, fo_ref, dfo_ref, delf_ref), (512, so_ref, dso_ref, dels_ref)):
                sl = slice(base + hh * HD, base + (hh + 1) * HD)
                o = o_ref[hh].astype(F32)
                do = (dmixed[:, sl] * sg[:, sl]).astype(BF16)
                do_ref[hh] = do
                del_ref[hh] = jnp.sum(do.astype(F32) * o, axis=1, keepdims=True)
                dgate_ref[:, sl] = (dmixed[:, sl] * o * dsg[:, sl]).astype(BF16)
                dw_ref[sl, :] += _dot_tn((o * sg[:, sl]).astype(BF16), dyb)

    cols = pl.BlockSpec((NH, tm, 1), lambda i: (0, i, 0))
    hs = jax.ShapeDtypeStruct((NH, s, HD), BF16)
    cs = jax.ShapeDtypeStruct((NH, s, 1), F32)
    return pl.pallas_call(
        body, name="even_out_bwd", grid=(s // tm,),
        in_specs=[_rows(tm, D_MODEL), _whole((D_MODEL, EVEN_W)), _heads(tm), _heads(tm), _rows(tm, EVEN_W)],
        out_specs=[_heads(tm), _heads(tm), _rows(tm, EVEN_W), cols, cols, _whole((EVEN_W, D_MODEL))],
        out_shape=[hs, hs, jax.ShapeDtypeStruct((s, EVEN_W), BF16), cs, cs, jax.ShapeDtypeStruct((EVEN_W, D_MODEL), F32)],
        compiler_params=_cparams(("arbitrary",), 48),
    )(dy1, w_out_t, fo, so, gate)


def even_in_bwd(dfqs, dfkn, dfv, dsqs, dsk, dsv, dgate, dflog, fqr, fkr, gq, gk, w_pack_t, x, dy1, gnorm):
    s = x.shape[0]

    def body(dfq_ref, dfk_ref, dfv_ref, dsq_ref, dsk_ref, dsv_ref, dg_ref, dfl_ref, qr_ref, kr_ref, gq_ref, gk_ref,
             wt_ref, x_ref, dy_ref, gn_ref, dproj_ref, dx_ref, dgn_ref, dgain_ref, dbf_ref):
        @pl.when(pl.program_id(0) == 0)
        def _():
            dgn_ref[...] = jnp.zeros_like(dgn_ref)
            dgain_ref[...] = jnp.zeros_like(dgain_ref)
            dbf_ref[...] = jnp.zeros_like(dbf_ref)

        o = FL_OFF + LANES
        for hh in range(NH):
            sl = slice(hh * HD, (hh + 1) * HD)
            dq, gq_row = _seg_rms_bwd(qr_ref[hh], dfq_ref[hh], gq_ref[:, sl], SCALE_E, HD)
            dk, gk_row = _seg_rms_bwd(kr_ref[hh], dfk_ref[hh] * LN2, gk_ref[:, sl], 1.0, HD)
            dproj_ref[:, sl] = dq.astype(BF16)
            dproj_ref[:, 512 + hh * HD:512 + (hh + 1) * HD] = dk.astype(BF16)
            dproj_ref[:, 1024 + hh * HD:1024 + (hh + 1) * HD] = dfv_ref[hh].astype(BF16)
            dproj_ref[:, o + hh * HD:o + (hh + 1) * HD] = (dsq_ref[hh] * SCALE_E).astype(BF16)
            dproj_ref[:, o + 512 + hh * HD:o + 512 + (hh + 1) * HD] = (dsk_ref[hh] * LN2).astype(BF16)
            dproj_ref[:, o + 1024 + hh * HD:o + 1024 + (hh + 1) * HD] = dsv_ref[hh].astype(BF16)
            dgain_ref[0:1, sl] += gq_row
            dgain_ref[1:2, sl] += gk_row
        dfl = dfl_ref[...]
        dproj_ref[:, FL_OFF:FL_OFF + LANES] = dfl.astype(BF16)
        dbf_ref[...] += jnp.sum(dfl, axis=0, keepdims=True)
        dproj_ref[:, o + 1536:o + 2560] = dg_ref[...]
        dh = _dotf(dproj_ref[...], wt_ref[...])
        dx, gn_row = _rms_bwd(x_ref[...], dh, gn_ref[...])
        dx_ref[...] = dy_ref[...] + dx
        dgn_ref[...] += gn_row

    return pl.pallas_call(
        body, name="even_in_bwd", grid=(s // TM,),
        in_specs=[_heads(TM)] * 6 + [_rows(TM, EVEN_W), _rows(TM, LANES), _heads(TM), _heads(TM), _whole((1, 512)), _whole((1, 512)),
                                     _whole((EVEN_PACK, D_MODEL)), _rows(TM, D_MODEL), _rows(TM, D_MODEL), _whole((1, D_MODEL))],
        out_specs=[_rows(TM, EVEN_PACK), _rows(TM, D_MODEL), _whole((1, D_MODEL)), _whole((8, 512)), _whole((1, LANES))],
        out_shape=[jax.ShapeDtypeStruct((s, EVEN_PACK), BF16), jax.ShapeDtypeStruct((s, D_MODEL), F32),
                   jax.ShapeDtypeStruct((1, D_MODEL), F32), jax.ShapeDtypeStruct((8, 512), F32), jax.ShapeDtypeStruct((1, LANES), F32)],
        compiler_params=_cparams(("arbitrary",), 52),
    )(dfqs, dfkn, dfv, dsqs, dsk, dsv, dgate, dflog, fqr, fkr, gq, gk, w_pack_t, x, dy1, gnorm)


def matmul_tn(a, b, tn, name):
    s, m = a.shape
    n = b.shape[1]
    tk = 2 * TM
    nk = s // tk

    def body(a_ref, b_ref, o_ref):
        @pl.when(pl.program_id(1) == 0)
        def _():
            o_ref[...] = jnp.zeros_like(o_ref)

        o_ref[...] += _dot_tn(a_ref[...], b_ref[...])

    return pl.pallas_call(
        body, name=name, grid=(n // tn, nk),
        in_specs=[pl.BlockSpec((tk, m), lambda j, k: (k, 0)), pl.BlockSpec((tk, tn), lambda j, k: (k, j))],
        out_specs=pl.BlockSpec((m, tn), lambda j, k: (0, j)),
        out_shape=jax.ShapeDtypeStruct((m, n), F32),
        compiler_params=_cparams(("parallel", "arbitrary"), 32),
    )(a, b)


def _tile_gain(g, reps):
    return jnp.tile(g.reshape(1, -1), (1, reps))


def local_step(x, target, w_in_e, b_f, gq_e, gk_e, gn_e, w_out_e, gn_o, w_in_o, gq_o, gk_o, w_out_o, fox_blocks, sb_blocks):
    s = x.shape[0]
    r = s // LANES
    w_pack = jnp.concatenate([w_in_e[:, :FL_OFF + NH], jnp.zeros((D_MODEL, LANES - NH), BF16), w_in_e[:, FL_OFF + NH:]], axis=1)
    bf_pad = jnp.pad(b_f.reshape(1, NH), ((0, 0), (0, LANES - NH)))
    gq512, gk512 = _tile_gain(gq_e, NH), _tile_gain(gk_e, NH)
    gq2, gk2 = _tile_gain(gq_o, NH2), _tile_gain(gk_o, NH2)
    gn_e, gn_o = gn_e.reshape(1, D_MODEL), gn_o.reshape(1, D_MODEL)

    h, fqs, fkn, fv, fqr, fkr, flog, sqs, sk, sv, gate = even_in_fwd(x, gn_e, w_pack, bf_pad, gq512, gk512)
    flog4 = flog[:, :NH].T.reshape(NH, r, LANES)
    nc = fox_cum(flog4).reshape(NH, 1, s)
    fo, lse = fox_fwd(fqs, fkn, fv, nc, *fox_blocks)
    so = sb_fwd(sqs, sk, sv, *sb_blocks)
    y1 = even_out_fwd(fo, so, gate, x, w_out_e)
    h2, q2s, k2n, v2, q2r, k2r, gate2 = odd_in_fwd(y1, gn_o, w_in_o, gq2, gk2)
    ol = [dil_fwd(q2s, k2n, v2, g) for g in range(NG)]
    att, lse2, dy2, loss_parts = odd_out_fwd(ol[0][0], ol[0][1], ol[1][0], ol[1][1], ol[2][0], ol[2][1], gate2, y1, target, w_out_o)
    loss = jnp.sum(loss_parts[:, 0, 0])
    datt, dgate2, delta2, d_w_out_o = odd_out_bwd(dy2, w_out_o.T, att, gate2)
    dqkv = [dil_bwd(q2s, k2n, v2, datt, lse2, delta2, g) for g in range(NG)]
    dproj2, dy1, d_gn_o, dgain_o = odd_in_bwd([t[0] for t in dqkv], [t[1] for t in dqkv], [t[2] for t in dqkv], dgate2,
                                              q2r, k2r, gq2, gk2, w_in_o.T, y1, dy2, gn_o)
    d_w_in_o = matmul_tn(h2, dproj2, 512, "dw_in_odd")
    dfo, dso, dgate, delta_f, delta_s, d_w_out_e = even_out_bwd(dy1, w_out_e.T, fo, so, gate)
    dfqs, dfkn, dfv, dnc, drow = fox_bwd(fqs, fkn, fv, nc, lse, dfo, delta_f, fox_blocks[0], fox_blocks[1] // 2)
    dsqs, dsk, dsv = sb_bwd(sqs, sk, sv, dso, delta_s, *sb_blocks)
    dcum4 = (drow.reshape(NH, s) - dnc.reshape(NH, s)).reshape(NH, r, LANES)
    dflog4 = fox_cum_bwd(dcum4, flog4)
    dflog = jnp.pad(dflog4.reshape(NH, s).T, ((0, 0), (0, LANES - NH)))
    dproj, grad_x, d_gn_e, dgain_e, d_bf = even_in_bwd(dfqs, dfkn, dfv, dsqs, dsk, dsv, dgate, dflog, fqr, fkr, gq512, gk512,
                                                       w_pack.T, x, dy1, gn_e)
    d_w_pack = matmul_tn(h, dproj, 384, "dw_in_even")
    d_w_in_e = jnp.concatenate([d_w_pack[:, :FL_OFF + NH], d_w_pack[:, FL_OFF + LANES:]], axis=1)
    grads = dict(
        even_norm=d_gn_e.reshape(-1), even_w_in=d_w_in_e, even_b_f=d_bf[0, :NH],
        even_q_gain=dgain_e[0].reshape(NH, HD).sum(0), even_k_gain=dgain_e[1].reshape(NH, HD).sum(0),
        even_w_out=d_w_out_e, odd_norm=d_gn_o.reshape(-1), odd_w_in=d_w_in_o,
        odd_q_gain=dgain_o[:NG].reshape(NG * NH2, HD2).sum(0), odd_k_gain=dgain_o[NG:2 * NG].reshape(NG * NH2, HD2).sum(0),
        odd_w_out=d_w_out_o)
    return loss, grad_x, grads


SHARDED = (("even_w_in", (D_MODEL, EVEN_IN // 4)), ("even_w_out", (EVEN_W // 4, D_MODEL)), ("odd_norm", (D_MODEL // 4,)),
           ("odd_norm_lo", (D_MODEL // 4,)), ("odd_w_in", (D_MODEL, ODD_IN // 4)), ("odd_w_out", (ODD_W, D_MODEL // 4)))
REPLICATED = (("even_norm", (D_MODEL,)), ("even_b_f", (NH,)), ("even_q_gain", (HD,)), ("even_k_gain", (HD,)),
              ("odd_q_gain", (HD2,)), ("odd_k_gain", (HD2,)))
PACK_ELEMS = sum(int(np.prod(shape)) for _, shape in SHARDED + REPLICATED)
PACK_ROWS = -(-PACK_ELEMS // (D_MODEL * 32)) * 32
HALF = PACK_ROWS // 2
HBM = pl.BlockSpec(memory_space=pl.ANY)


def _pack(parts):
    flat = [parts[n].reshape(-1) for n, _ in SHARDED + REPLICATED]
    flat.append(jnp.zeros((PACK_ROWS * D_MODEL - PACK_ELEMS,), flat[0].dtype))
    return jnp.concatenate(flat).reshape(PACK_ROWS, D_MODEL)


def _unpack(buf):
    flat = buf.reshape(-1)
    out, off = {}, 0
    for n, shape in SHARDED + REPLICATED:
        size = int(np.prod(shape))
        out[n] = flat[off:off + size].reshape(shape)
        off += size
    return out


def _place():
    x, y, c = lax.axis_index("x"), lax.axis_index("y"), lax.axis_index("c")
    return x, y, c, [(1 - x, y), (x, 1 - y), (1 - x, 1 - y)]


def all_gather_shards(mine):
    def body(src_ref, out_ref, send_sems, recv_sems, local_sem):
        x, y, c, chips = _place()
        me = 2 * x + y
        half = lambda cc: pl.ds(cc * HALF, HALF)

        def copy(k, j, cc, to, src=None):
            dst = out_ref.at[j, half(cc)]
            return pltpu.make_async_remote_copy(src_ref=dst if src is None else src, dst_ref=dst,
                                                send_sem=send_sems.at[k], recv_sem=recv_sems.at[k],
                                                device_id=to, device_id_type=MESH)

        local = pltpu.make_async_copy(src_ref, out_ref.at[me], local_sem)
        local.start()
        first = [copy(k, me, c, (cx, cy, c), src=src_ref.at[half(c)]) for k, (cx, cy) in enumerate(chips)]
        for cp in first:
            cp.start()
        passed = [copy(3 + k, 2 * cx + cy, c, (x, y, 1 - c)) for k, (cx, cy) in enumerate(chips)]
        for k, (cx, cy) in enumerate(chips):
            copy(k, 2 * cx + cy, c, (x, y, c)).wait_recv()
            passed[k].start()
        for k, (cx, cy) in enumerate(chips):
            copy(3 + k, 2 * cx + cy, 1 - c, (x, y, c)).wait_recv()
        for cp in first + passed:
            cp.wait_send()
        local.wait()

    return pl.pallas_call(
        body, name="all_gather_shards",
        in_specs=[HBM], out_specs=HBM,
        out_shape=jax.ShapeDtypeStruct((4, PACK_ROWS, D_MODEL), mine.dtype),
        scratch_shapes=[pltpu.SemaphoreType.DMA((6,)), pltpu.SemaphoreType.DMA((6,)), pltpu.SemaphoreType.DMA],
    )(mine)


def sibling_swap_halves(g):
    def body(g_ref, a_ref, send_sem, recv_sem):
        x, y, c, _ = _place()
        cp = pltpu.make_async_remote_copy(src_ref=g_ref.at[:, pl.ds((1 - c) * HALF, HALF)], dst_ref=a_ref,
                                          send_sem=send_sem, recv_sem=recv_sem, device_id=(x, y, 1 - c), device_id_type=MESH)
        cp.start()
        cp.wait()

    return pl.pallas_call(
        body, name="sibling_swap_halves",
        in_specs=[HBM], out_specs=HBM,
        out_shape=jax.ShapeDtypeStruct((4, HALF, D_MODEL), g.dtype),
        scratch_shapes=[pltpu.SemaphoreType.DMA, pltpu.SemaphoreType.DMA],
    )(g)


def chip_exchange(p):
    def body(p_ref, b_ref, send_sems, recv_sems, local_sem):
        x, y, c, chips = _place()
        me = 2 * x + y
        local = pltpu.make_async_copy(p_ref.at[me], b_ref.at[me], local_sem)
        local.start()
        sends = [pltpu.make_async_remote_copy(src_ref=p_ref.at[2 * cx + cy], dst_ref=b_ref.at[me],
                                              send_sem=send_sems.at[k], recv_sem=recv_sems.at[k],
                                              device_id=(cx, cy, c), device_id_type=MESH)
                 for k, (cx, cy) in enumerate(chips)]
        for cp in sends:
            cp.start()
        for k, (cx, cy) in enumerate(chips):
            pltpu.make_async_remote_copy(src_ref=p_ref.at[me], dst_ref=b_ref.at[2 * cx + cy],
                                         send_sem=send_sems.at[k], recv_sem=recv_sems.at[k],
                                         device_id=(cx, cy, c), device_id_type=MESH).wait_recv()
        for cp in sends:
            cp.wait_send()
        local.wait()

    return pl.pallas_call(
        body, name="chip_exchange",
        in_specs=[HBM], out_specs=HBM,
        out_shape=jax.ShapeDtypeStruct((4, HALF, D_MODEL), p.dtype),
        scratch_shapes=[pltpu.SemaphoreType.DMA((3,)), pltpu.SemaphoreType.DMA((3,)), pltpu.SemaphoreType.DMA],
    )(p)


def sibling_join_halves(mine):
    def body(h_ref, out_ref, send_sem, recv_sem, local_sem):
        x, y, c, _ = _place()
        local = pltpu.make_async_copy(h_ref, out_ref.at[pl.ds(c * HALF, HALF)], local_sem)
        local.start()
        cp = pltpu.make_async_remote_copy(src_ref=h_ref, dst_ref=out_ref.at[pl.ds(c * HALF, HALF)],
                                          send_sem=send_sem, recv_sem=recv_sem, device_id=(x, y, 1 - c), device_id_type=MESH)
        cp.start()
        cp.wait_send()
        pltpu.make_async_remote_copy(src_ref=h_ref, dst_ref=out_ref.at[pl.ds((1 - c) * HALF, HALF)],
                                     send_sem=send_sem, recv_sem=recv_sem, device_id=(x, y, 1 - c), device_id_type=MESH).wait_recv()
        local.wait()

    return pl.pallas_call(
        body, name="sibling_join_halves",
        in_specs=[HBM], out_specs=HBM,
        out_shape=jax.ShapeDtypeStruct((PACK_ROWS, D_MODEL), mine.dtype),
        scratch_shapes=[pltpu.SemaphoreType.DMA, pltpu.SemaphoreType.DMA, pltpu.SemaphoreType.DMA],
    )(mine)


def _sum_call(name, arrays, rows):
    tr = rows // 5 if rows % 40 == 0 else rows

    def body(*refs):
        acc = refs[0][...]
        for r in refs[1:-1]:
            acc = acc + r[...]
        refs[-1][...] = acc

    spec = pl.BlockSpec((tr, D_MODEL), lambda i: (i, 0))
    return pl.pallas_call(
        body, name=name, grid=(rows // tr,),
        in_specs=[spec] * len(arrays), out_specs=spec,
        out_shape=jax.ShapeDtypeStruct((rows, D_MODEL), F32),
        compiler_params=_cparams(("parallel",), 40),
    )(*arrays)


def adamw(w, g, m, v):
    tr = PACK_ROWS // 5

    def body(w_ref, g_ref, m_ref, v_ref, d_ref, nm_ref, nv_ref):
        gt = g_ref[...]
        nm = ADAM_B1 * m_ref[...] + (1.0 - ADAM_B1) * gt
        nv = ADAM_B2 * v_ref[...] + (1.0 - ADAM_B2) * (gt * gt)
        m_hat = nm / (1.0 - ADAM_B1 ** ADAM_STEP)
        v_hat = nv / (1.0 - ADAM_B2 ** ADAM_STEP)
        d_ref[...] = -ADAM_LR * (m_hat / (jnp.sqrt(v_hat) + ADAM_EPS) + ADAM_WD * w_ref[...])
        nm_ref[...] = nm
        nv_ref[...] = nv

    spec = pl.BlockSpec((tr, D_MODEL), lambda i: (i, 0))
    shape = jax.ShapeDtypeStruct((PACK_ROWS, D_MODEL), F32)
    return pl.pallas_call(
        body, name="adamw", grid=(PACK_ROWS // tr,),
        in_specs=[spec] * 4, out_specs=[spec] * 3, out_shape=[shape] * 3,
        compiler_params=_cparams(("parallel",), 40),
    )(w, g, m, v)


def kernel(x, even_norm, even_w_in, even_b_f, even_q_gain, even_k_gain, even_w_out, odd_norm, odd_w_in, odd_q_gain, odd_k_gain, odd_w_out, loss_target, m_even_norm, m_even_w_in, m_even_b_f, m_even_q_gain, m_even_k_gain, m_even_w_out, m_odd_norm, m_odd_w_in, m_odd_q_gain, m_odd_k_gain, m_odd_w_out, v_even_norm, v_even_w_in, v_even_b_f, v_even_q_gain, v_even_k_gain, v_even_w_out, v_odd_norm, v_odd_w_in, v_odd_q_gain, v_odd_k_gain, v_odd_w_out):
    names = [n for n, _ in SHARDED + REPLICATED if n != "odd_norm_lo"]
    w = dict(even_norm=even_norm, even_w_in=even_w_in, even_b_f=even_b_f, even_q_gain=even_q_gain, even_k_gain=even_k_gain,
             even_w_out=even_w_out, odd_norm=odd_norm, odd_w_in=odd_w_in, odd_q_gain=odd_q_gain, odd_k_gain=odd_k_gain,
             odd_w_out=odd_w_out)
    m = dict(even_norm=m_even_norm, even_w_in=m_even_w_in, even_b_f=m_even_b_f, even_q_gain=m_even_q_gain,
             even_k_gain=m_even_k_gain, even_w_out=m_even_w_out, odd_norm=m_odd_norm, odd_w_in=m_odd_w_in,
             odd_q_gain=m_odd_q_gain, odd_k_gain=m_odd_k_gain, odd_w_out=m_odd_w_out)
    v = dict(even_norm=v_even_norm, even_w_in=v_even_w_in, even_b_f=v_even_b_f, even_q_gain=v_even_q_gain,
             even_k_gain=v_even_k_gain, even_w_out=v_even_w_out, odd_norm=v_odd_norm, odd_w_in=v_odd_w_in,
             odd_q_gain=v_odd_q_gain, odd_k_gain=v_odd_k_gain, odd_w_out=v_odd_w_out)
    spare = jnp.zeros((D_MODEL // 4,), F32)

    on = odd_norm.reshape(-1)
    on_hi = on.astype(BF16)
    wire = {n: w[n].astype(BF16) for n in names}
    wire["odd_norm"] = on_hi
    wire["odd_norm_lo"] = (on - on_hi.astype(F32)).astype(BF16)
    gathered = all_gather_shards(_pack(wire))
    sh = [_unpack(gathered[j]) for j in range(4)]
    cat = lambda n, axis: jnp.concatenate([t[n] for t in sh], axis=axis)
    gn_o = cat("odd_norm", 0).astype(F32) + cat("odd_norm_lo", 0).astype(F32)

    s = x.shape[1]
    bq = min(512, s)
    loss_local, grad_x, g = local_step(
        x[0], loss_target[0], cat("even_w_in", 1), even_b_f[0], even_q_gain[0], even_k_gain[0], even_norm[0],
        cat("even_w_out", 0), gn_o, cat("odd_w_in", 1), odd_q_gain[0], odd_k_gain[0], cat("odd_w_out", 1),
        (min(1024, s), min(512, s)), (min(1024, s), min(256, s)))
    loss = lax.psum(loss_local, ("x", "y", "c"))

    def grad_parts(j):
        parts = {n: g[n] for n, _ in REPLICATED}
        parts["even_w_in"] = g["even_w_in"][:, j * (EVEN_IN // 4):(j + 1) * (EVEN_IN // 4)]
        parts["even_w_out"] = g["even_w_out"][j * (EVEN_W // 4):(j + 1) * (EVEN_W // 4), :]
        parts["odd_norm"] = g["odd_norm"][j * (D_MODEL // 4):(j + 1) * (D_MODEL // 4)]
        parts["odd_norm_lo"] = spare
        parts["odd_w_in"] = g["odd_w_in"][:, j * (ODD_IN // 4):(j + 1) * (ODD_IN // 4)]
        parts["odd_w_out"] = g["odd_w_out"][:, j * (D_MODEL // 4):(j + 1) * (D_MODEL // 4)]
        return parts

    g_all = jnp.stack([_pack(grad_parts(j)) for j in range(4)])
    from_sibling = sibling_swap_halves(g_all)
    c = lax.axis_index("c")
    g_mine = lax.dynamic_slice_in_dim(g_all, c * HALF, HALF, axis=1)
    pair = _sum_call("pair_sum", [g_mine.reshape(4 * HALF, D_MODEL), from_sibling.reshape(4 * HALF, D_MODEL)], 4 * HALF)
    by_chip = chip_exchange(pair.reshape(4, HALF, D_MODEL))
    half_sum = _sum_call("chip_sum", [by_chip[0], by_chip[1], by_chip[2], by_chip[3]], HALF)
    g_buf = sibling_join_halves(half_sum)

    local = lambda d: _pack({**{n: d[n].astype(F32) for n in names}, "odd_norm_lo": spare})
    delta_buf, m_buf, v_buf = adamw(local(w), g_buf, local(m), local(v))
    outs = [loss.reshape(()), grad_x.reshape(x.shape)]
    order = ["even_norm", "even_w_in", "even_b_f", "even_q_gain", "even_k_gain", "even_w_out", "odd_norm", "odd_w_in",
             "odd_q_gain", "odd_k_gain", "odd_w_out"]
    for buf in (g_buf, delta_buf, m_buf, v_buf):
        parts = _unpack(buf)
        outs += [parts[n].reshape(w[n].shape) for n in order]
    return tuple(outs)
```

```python
import jax
import jax.numpy as jnp
import numpy as np
from jax import lax
from jax.experimental import pallas as pl
from jax.experimental.pallas import tpu as pltpu

F32 = jnp.float32
BF16 = jnp.bfloat16

D_MODEL = 1024
HD = 128
NH = 4
HD2 = 64
NG = 3
NH2 = 8
DILATIONS = (1, 4, 16)
SPAN = 128
EVEN_W = 1024
ODD_W = 512
EVEN_IN = 4100
EVEN_PACK = 4224
FL_OFF = 1536
ODD_IN = 5120
RMS_EPS = 1e-6
SCALE_E = HD ** -0.5
SCALE_O = HD2 ** -0.5
ADAM_LR, ADAM_B1, ADAM_B2, ADAM_EPS, ADAM_WD, ADAM_STEP = 0.001, 0.9, 0.999, 1e-08, 0.01, 10

VMEM_CAP = 64 * 1024 * 1024
LANES = 128
MESH = pl.DeviceIdType.MESH


def _cparams(sem, vmem_mb):
    return pltpu.CompilerParams(dimension_semantics=sem, vmem_limit_bytes=min(vmem_mb << 20, VMEM_CAP - (6 << 20)))


def _silu(g):
    return g / (1.0 + jnp.exp(-g))


def _dsilu(g):
    s = 1.0 / (1.0 + jnp.exp(-g))
    return s * (1.0 + g * (1.0 - s))


def _split2(x):
    hi = x.astype(BF16)
    lo = (x - hi.astype(F32)).astype(BF16)
    return hi, lo


def _split3(x):
    hi = x.astype(BF16)
    r = x - hi.astype(F32)
    mid = r.astype(BF16)
    lo = (r - mid.astype(F32)).astype(BF16)
    return hi, mid, lo


def _dotf(a, b):
    return jnp.dot(a, b, preferred_element_type=F32)


def _dot_nt(a, b):
    return lax.dot_general(a, b, (((1,), (1,)), ((), ())), preferred_element_type=F32)


def _dot_tn(a, b):
    return lax.dot_general(a, b, (((0,), (0,)), ((), ())), preferred_element_type=F32)


def _segsum(x, hd):
    r = lax.broadcasted_iota(jnp.int32, (LANES, LANES), 0) // hd
    c = lax.broadcasted_iota(jnp.int32, (LANES, LANES), 1) // hd
    ones = (r == c).astype(BF16)
    outs = []
    for ch in range(x.shape[1] // LANES):
        hi, lo = _split2(x[:, ch * LANES:(ch + 1) * LANES])
        outs.append(_dotf(hi, ones) + _dotf(lo, ones))
    return outs[0] if len(outs) == 1 else jnp.concatenate(outs, axis=1)


def _suffix_matrix(n):
    r = lax.broadcasted_iota(jnp.int32, (n, n), 0)
    c = lax.broadcasted_iota(jnp.int32, (n, n), 1)
    return (r >= c).astype(BF16)


def _walk_up_staged(i, per, stages):
    assert per % 2 == 0
    n_full = i * per

    def pair(first, masked):
        states = [{"j": first}, {"j": first + 1}]
        for stage in stages:
            for st in states:
                stage(st, masked)

    def full_trip(t, c):
        pair(2 * t, False)
        return c

    def masked_trip(d, c):
        pair(n_full + 2 * d, True)
        return c

    lax.fori_loop(0, n_full // 2, full_trip, 0)
    lax.fori_loop(0, per // 2, masked_trip, 0)


def fox_fwd(qs, kn, v, nc, bq, bk):
    nh, s, _ = qs.shape
    nq = s // bq
    per = bq // bk

    def body(q_ref, k_ref, v_ref, nc_ref, o_ref, lse_ref):
        i = pl.program_id(1)
        q = q_ref[...]

        def logits(j, masked):
            off = pl.multiple_of(j * bk, bk)
            sc = _dot_nt(q, k_ref[pl.ds(off, bk), :]) + nc_ref[:, pl.ds(off, bk)]
            if masked:
                row = i * bq + lax.broadcasted_iota(jnp.int32, (bq, bk), 0)
                col = off + lax.broadcasted_iota(jnp.int32, (bq, bk), 1)
                sc = jnp.where(col <= row, sc, -jnp.inf)
            return off, sc

        def update(carry, off, sc):
            m, l, acc = carry
            m_new = jnp.maximum(m, jnp.max(sc, axis=1, keepdims=True))
            alpha = jnp.exp2(m - m_new)
            p = jnp.exp2(sc - m_new)
            l = alpha * l + jnp.sum(p, axis=1, keepdims=True)
            acc = alpha * acc + _dotf(p.astype(BF16), v_ref[pl.ds(off, bk), :])
            return m_new, l, acc

        def pair(first, carry, masked):
            a, b = logits(first, masked), logits(first + 1, masked)
            return update(update(carry, *a), *b)

        assert per % 2 == 0
        n_full = i * per
        init = (jnp.full((bq, 1), -jnp.inf, F32), jnp.zeros((bq, 1), F32), jnp.zeros((bq, HD), F32))
        carry = lax.fori_loop(0, n_full // 2, lambda t, c: pair(2 * t, c, False), init)
        m, l, acc = lax.fori_loop(0, per // 2, lambda t, c: pair(n_full + 2 * t, c, True), carry)
        o_ref[...] = (acc / l).astype(o_ref.dtype)
        lse_ref[...] = m + jnp.log2(l)

    return pl.pallas_call(
        body, name="fox_fwd",
        grid=(nh, nq),
        in_specs=[pl.BlockSpec((None, bq, HD), lambda h, i: (h, i, 0)),
                  pl.BlockSpec((None, s, HD), lambda h, i: (h, 0, 0)),
                  pl.BlockSpec((None, s, HD), lambda h, i: (h, 0, 0)),
                  pl.BlockSpec((None, 1, s), lambda h, i: (h, 0, 0))],
        out_specs=[pl.BlockSpec((None, bq, HD), lambda h, i: (h, i, 0)),
                   pl.BlockSpec((None, bq, 1), lambda h, i: (h, i, 0))],
        out_shape=[jax.ShapeDtypeStruct((nh, s, HD), BF16), jax.ShapeDtypeStruct((nh, s, 1), F32)],
        compiler_params=_cparams(("arbitrary", "arbitrary"), 40),
    )(qs, kn, v, nc)


LOG2E = 1.4426950408889634
LN2 = 0.6931471805599453


def _neg_abs(z):
    sign = jnp.uint32(0x80000000)
    return lax.bitcast_convert_type(lax.bitcast_convert_type(z, jnp.uint32) | sign, F32)


def _sb_softplus2(z, row0, col0, masked):
    u = jnp.maximum(z, 0.0) + jnp.log2(1.0 + jnp.exp2(_neg_abs(z)))
    strict = None
    if masked:
        row = row0 + lax.broadcasted_iota(jnp.int32, z.shape, 0)
        col = col0 + lax.broadcasted_iota(jnp.int32, z.shape, 1)
        strict = col < row
        u = jnp.where(strict, u, 0.0)
    return u, strict


def _walk_down_staged(i, per, stages):
    assert per % 2 == 0
    n_full = i * per

    def pair(top, masked):
        states = [{"j": top}, {"j": top - 1}]
        for stage in stages:
            for st in states:
                stage(st, masked)

    def masked_trip(d, c):
        pair(n_full + per - 1 - 2 * d, True)
        return c

    def full_trip(t, c):
        pair(n_full - 1 - 2 * t, False)
        return c

    lax.fori_loop(0, per // 2, masked_trip, 0)
    lax.fori_loop(0, n_full // 2, full_trip, 0)


def _suffix2(x, m2):
    hi, lo = _split2(x)
    return _dotf(jnp.concatenate([hi, lo], axis=1), m2)


def _lanes(col, n):
    return jnp.broadcast_to(col, (col.shape[0], n))


def sb_fwd(qs, k, v, bq, bk):
    nh, s, _ = qs.shape
    nq = s // bq
    per = bq // bk

    def body(q_ref, k_ref, v_ref, o_ref, lrun_ref):
        i = pl.program_id(1)
        q = q_ref[...]
        tri = _suffix_matrix(bk)
        o_ref[...] = jnp.zeros_like(o_ref)
        lrun_ref[...] = jnp.zeros_like(lrun_ref)

        def logits(st, masked):
            st["off"] = pl.multiple_of(st["j"] * bk, bk)
            st["z"] = _dot_nt(q, k_ref[pl.ds(st["off"], bk), :])

        def suffix(st, masked):
            u, st["strict"] = _sb_softplus2(st["z"], i * bq, st["off"], masked)
            st["incl"] = _dotf(u.astype(BF16), tri)

        def weigh(st, masked):
            lrun = lrun_ref[...]
            w = jnp.exp2(st["z"] - st["incl"] + jnp.tile(lrun, (1, bk // LANES)))
            if masked:
                w = jnp.where(st["strict"], w, 0.0)
            o_ref[...] += _dotf(w.astype(BF16), v_ref[pl.ds(st["off"], bk), :])
            lrun_ref[...] = lrun - _lanes(st["incl"][:, 0:1], LANES)

        _walk_down_staged(i, per, [logits, suffix, weigh])

    return pl.pallas_call(
        body, name="sb_fwd",
        grid=(nh, nq),
        in_specs=[pl.BlockSpec((None, bq, HD), lambda h, i: (h, i, 0)),
                  pl.BlockSpec((None, s, HD), lambda h, i: (h, 0, 0)),
                  pl.BlockSpec((None, s, HD), lambda h, i: (h, 0, 0))],
        out_specs=pl.BlockSpec((None, bq, HD), lambda h, i: (h, i, 0)),
        out_shape=jax.ShapeDtypeStruct((nh, s, HD), F32),
        scratch_shapes=[pltpu.VMEM((bq, LANES), F32)],
        compiler_params=_cparams(("arbitrary", "arbitrary"), 40),
    )(qs, k, v)


def _attn_bwd_call(name, body, s, bq, ins, in_specs, extra_out_specs, extra_out_shapes, extra_scratch=()):
    nh = NH
    nq = s // bq
    return pl.pallas_call(
        body, name=name,
        grid=(nh, nq),
        in_specs=in_specs,
        out_specs=[pl.BlockSpec((None, bq, HD), lambda h, i: (h, i, 0)),
                   pl.BlockSpec(memory_space=pl.ANY), pl.BlockSpec(memory_space=pl.ANY)] + extra_out_specs,
        out_shape=[jax.ShapeDtypeStruct((nh, s, HD), F32), jax.ShapeDtypeStruct((nh, s, HD), F32),
                   jax.ShapeDtypeStruct((nh, s, HD), F32)] + extra_out_shapes,
        scratch_shapes=[pltpu.VMEM((s, HD), F32), pltpu.VMEM((s, HD), F32), pltpu.SemaphoreType.DMA((2,))] + list(extra_scratch),
        compiler_params=_cparams(("arbitrary", "arbitrary"), 52),
    )(*ins)


def _flush_dkv(i, nq, h, dk_acc, dv_acc, dk_hbm, dv_hbm, sems):
    @pl.when(i == nq - 1)
    def _():
        ck = pltpu.make_async_copy(dk_acc, dk_hbm.at[h], sems.at[0])
        cv = pltpu.make_async_copy(dv_acc, dv_hbm.at[h], sems.at[1])
        ck.start()
        cv.start()
        ck.wait()
        cv.wait()


def fox_bwd(qs, kn, v, nc, lse, do, delta, bq, bk):
    nh, s, _ = qs.shape
    nq = s // bq
    per = bq // bk

    def body(q_ref, k_ref, v_ref, nc_ref, lse_ref, do_ref, dl_ref, dq_ref, dk_hbm, dv_hbm, dnc_ref, drow_ref, dk_acc, dv_acc, sems):
        h, i = pl.program_id(0), pl.program_id(1)

        @pl.when(i == 0)
        def _():
            dk_acc[...] = jnp.zeros_like(dk_acc)
            dv_acc[...] = jnp.zeros_like(dv_acc)
            dnc_ref[...] = jnp.zeros_like(dnc_ref)

        q = q_ref[...]
        do_t = do_ref[...]
        lse_t = lse_ref[...]
        dl_t = dl_ref[...]

        dq_ref[...] = jnp.zeros_like(dq_ref)
        drow_ref[...] = jnp.zeros_like(drow_ref)

        def logits(st, masked):
            st["off"] = pl.multiple_of(st["j"] * bk, bk)
            st["sc"] = _dot_nt(q, k_ref[pl.ds(st["off"], bk), :]) + nc_ref[:, pl.ds(st["off"], bk)]
            st["dp"] = _dot_nt(do_t, v_ref[pl.ds(st["off"], bk), :])

        def grads(st, masked):
            off = st["off"]
            p = jnp.exp2(st["sc"] - lse_t)
            if masked:
                row = i * bq + lax.broadcasted_iota(jnp.int32, (bq, bk), 0)
                col = off + lax.broadcasted_iota(jnp.int32, (bq, bk), 1)
                p = jnp.where(col <= row, p, 0.0)
            ds = p * (st["dp"] - dl_t)
            dsb = ds.astype(BF16)
            dq_ref[...] += _dotf(dsb, k_ref[pl.ds(off, bk), :])
            dk_acc[pl.ds(off, bk), :] += _dot_tn(dsb, q)
            dv_acc[pl.ds(off, bk), :] += _dot_tn(p.astype(BF16), do_t)
            dnc_ref[:, pl.ds(off, bk)] += jnp.sum(ds, axis=0, keepdims=True)
            drow_ref[...] += jnp.sum(ds, axis=1, keepdims=True)

        _walk_up_staged(i, per, [logits, grads])
        _flush_dkv(i, nq, h, dk_acc, dv_acc, dk_hbm, dv_hbm, sems)

    tile_spec = pl.BlockSpec((None, bq, HD), lambda h, i: (h, i, 0))
    col_spec = pl.BlockSpec((None, bq, 1), lambda h, i: (h, i, 0))
    full_spec = pl.BlockSpec((None, s, HD), lambda h, i: (h, 0, 0))
    row_spec = pl.BlockSpec((None, 1, s), lambda h, i: (h, 0, 0))
    return _attn_bwd_call("fox_bwd", body, s, bq, (qs, kn, v, nc, lse, do, delta),
                          [tile_spec, full_spec, full_spec, row_spec, col_spec, tile_spec, col_spec],
                          [row_spec, col_spec],
                          [jax.ShapeDtypeStruct((nh, 1, s), F32), jax.ShapeDtypeStruct((nh, s, 1), F32)])


def sb_bwd(qs, k, v, do, delta, bq, bk):
    nh, s, _ = qs.shape
    nq = s // bq
    per = bq // bk

    def body(q_ref, k_ref, v_ref, do_ref, dl_ref, dq_ref, dk_hbm, dv_hbm, dk_acc, dv_acc, sems, lrun_ref, crun_ref):
        h, i = pl.program_id(0), pl.program_id(1)

        @pl.when(i == 0)
        def _():
            dk_acc[...] = jnp.zeros_like(dk_acc)
            dv_acc[...] = jnp.zeros_like(dv_acc)

        q = q_ref[...]
        do_t = do_ref[...]
        tri = _suffix_matrix(bk)
        tri2 = jnp.concatenate([tri, tri], axis=0)
        dq_ref[...] = jnp.zeros_like(dq_ref)
        lrun_ref[...] = jnp.zeros_like(lrun_ref)
        crun_ref[...] = _lanes(dl_ref[...], LANES)

        def logits(st, masked):
            st["off"] = pl.multiple_of(st["j"] * bk, bk)
            st["z"] = _dot_nt(q, k_ref[pl.ds(st["off"], bk), :])
            st["dw"] = _dot_nt(do_t, v_ref[pl.ds(st["off"], bk), :])

        def suffix(st, masked):
            st["u"], st["strict"] = _sb_softplus2(st["z"], i * bq, st["off"], masked)
            st["incl"] = _dotf(st["u"].astype(BF16), tri)

        def weigh(st, masked):
            lrun = lrun_ref[...]
            w = jnp.exp2(st["z"] - st["incl"] + jnp.tile(lrun, (1, bk // LANES)))
            if masked:
                w = jnp.where(st["strict"], w, 0.0)
            st["wb"] = w.astype(BF16)
            st["e"] = st["dw"] * st["wb"].astype(F32)
            st["einc"] = _suffix2(st["e"], tri2)
            lrun_ref[...] = lrun - _lanes(st["incl"][:, 0:1], LANES)

        def grads(st, masked):
            crun = crun_ref[...]
            prefix = jnp.tile(crun, (1, bk // LANES)) - st["einc"]
            dz = st["e"] - jnp.exp2(st["z"] - st["u"]) * (st["e"] + prefix)
            if masked:
                dz = jnp.where(st["strict"], dz, 0.0)
            dzb = dz.astype(BF16)
            dq_ref[...] += _dotf(dzb, k_ref[pl.ds(st["off"], bk), :])
            dk_acc[pl.ds(st["off"], bk), :] += _dot_tn(dzb, q)
            dv_acc[pl.ds(st["off"], bk), :] += _dot_tn(st["wb"], do_t)
            crun_ref[...] = crun - _lanes(st["einc"][:, 0:1], LANES)

        _walk_down_staged(i, per, [logits, suffix, weigh, grads])
        _flush_dkv(i, nq, h, dk_acc, dv_acc, dk_hbm, dv_hbm, sems)

    tile_spec = pl.BlockSpec((None, bq, HD), lambda h, i: (h, i, 0))
    col_spec = pl.BlockSpec((None, bq, 1), lambda h, i: (h, i, 0))
    full_spec = pl.BlockSpec((None, s, HD), lambda h, i: (h, 0, 0))
    return _attn_bwd_call("sb_bwd", body, s, bq, (qs, k, v, do, delta),
                          [tile_spec, full_spec, full_spec, tile_spec, col_spec], [], [],
                          [pltpu.VMEM((bq, LANES), F32), pltpu.VMEM((bq, LANES), F32)])


BI = SPAN
PAIR = 2 * HD2


def _slopes(g):
    return [float(2.0 ** (-8.0 * (g * NH2 + h + 1) / (NG * NH2))) for h in range(NH2)]


def _head_lanes(hh):
    return (lax.broadcasted_iota(jnp.int32, (1, PAIR), 1) // HD2) == hh


def _dil_masks(n, nblk, d):
    a = lax.broadcasted_iota(jnp.int32, (BI, 2 * BI), 0)
    c = lax.broadcasted_iota(jnp.int32, (BI, 2 * BI), 1)
    dist = a - c + BI
    valid = (dist >= 0) & (dist <= SPAN) & ((c >= BI) | (n > 0))
    a2 = lax.broadcasted_iota(jnp.int32, (2 * BI, BI), 0)
    c2 = lax.broadcasted_iota(jnp.int32, (2 * BI, BI), 1)
    dist2 = a2 - c2
    valid2 = (dist2 >= 0) & (dist2 <= SPAN) & ((a2 < BI) | (n < nblk - 1))
    return valid, (dist * d).astype(F32), valid2, (dist2 * d).astype(F32)


def _dil_pair_fwd(qp, kcat, vcat, valid, distf, slopes2):
    o_pair = jnp.zeros((BI, PAIR), F32)
    lse_pair = jnp.zeros((BI, PAIR), F32)
    for hh in range(2):
        lm = _head_lanes(hh)
        qm = jnp.where(lm, qp, jnp.zeros_like(qp))
        logits = jnp.where(valid, _dot_nt(qm, kcat) - slopes2[hh] * distf, -jnp.inf)
        m = jnp.max(logits, axis=1, keepdims=True)
        p = jnp.exp(logits - m)
        den = jnp.sum(p, axis=1, keepdims=True)
        o_pair = jnp.where(lm, _dotf(p.astype(BF16), vcat) / den, o_pair)
        lse_pair = jnp.where(lm, m + jnp.log(den), lse_pair)
    return o_pair, lse_pair


def _dil_pair_bwd(qc, qn, kc, kp, vc, vp, doc, don, lse_c, lse_n, dl_c, dl_n, masks, slopes2):
    valid, distf, valid2, dist2f = masks
    qcat = jnp.concatenate([qc, qn], axis=0)
    docat = jnp.concatenate([doc, don], axis=0)
    kcat = jnp.concatenate([kp, kc], axis=0)
    vcat = jnp.concatenate([vp, vc], axis=0)
    dq_pair = jnp.zeros((BI, PAIR), F32)
    dk_pair = jnp.zeros((BI, PAIR), F32)
    dv_pair = jnp.zeros((BI, PAIR), F32)
    for hh in range(2):
        col = slice(hh * HD2, hh * HD2 + 1)
        lm = _head_lanes(hh)
        zq = jnp.zeros_like(qc)
        logits = _dot_nt(jnp.where(lm, qc, zq), kcat) - slopes2[hh] * distf
        p = jnp.exp(jnp.where(valid, logits, -jnp.inf) - lse_c[:, col])
        dp = _dot_nt(jnp.where(lm, doc, zq), vcat)
        ds = (p * (dp - dl_c[:, col])).astype(BF16)
        dq_pair = jnp.where(lm, _dotf(ds, kcat), dq_pair)

        zcat = jnp.zeros_like(qcat)
        qm = jnp.where(lm, qcat, zcat)
        dom = jnp.where(lm, docat, zcat)
        lse2 = jnp.concatenate([lse_c[:, col], lse_n[:, col]], axis=0)
        dl2 = jnp.concatenate([dl_c[:, col], dl_n[:, col]], axis=0)
        logits2 = _dot_nt(qm, kc) - slopes2[hh] * dist2f
        p2 = jnp.exp(jnp.where(valid2, logits2, -jnp.inf) - lse2)
        ds2 = (p2 * (_dot_nt(dom, vc) - dl2)).astype(BF16)
        dk_pair = dk_pair + _dot_tn(ds2, qm)
        dv_pair = dv_pair + _dot_tn(p2.astype(BF16), dom)
    return dq_pair, dk_pair, dv_pair


def _pair_slopes(slopes, hp):
    out = []
    for hh in range(2):
        acc = jnp.float32(slopes[hh])
        for t in range(1, NH2 // 2):
            acc = jnp.where(hp == t, jnp.float32(slopes[2 * t + hh]), acc)
        out.append(acc)
    return out


def _for_residues(d, residue):
    per_trip = min(d, 4)

    def trip(t, carry):
        for u in range(per_trip):
            residue(t * per_trip + u)
        return carry

    if d == per_trip:
        trip(0, 0)
    else:
        lax.fori_loop(0, d // per_trip, trip, 0)


def _dil_tiling(d):
    return NH2 // 2 if d == 1 else 1


def dil_fwd(q, k, v, g):
    d = DILATIONS[g]
    s = q.shape[1]
    rows_per = BI * d
    nblk = s // rows_per
    tiles = _dil_tiling(d)
    slopes = _slopes(g)

    def body(q_ref, kc_ref, kp_ref, vc_ref, vp_ref, o_ref, lse_ref):
        n = pl.program_id(0)
        valid, distf, _, _ = _dil_masks(n, nblk, d)
        for t in range(tiles):
            sl = slice(t * PAIR, (t + 1) * PAIR)
            slopes2 = _pair_slopes(slopes, pl.program_id(1) * tiles + t)

            def residue(r):
                rows = pl.ds(r, BI, stride=d)
                kcat = jnp.concatenate([kp_ref[rows, sl], kc_ref[rows, sl]], axis=0).astype(BF16)
                vcat = jnp.concatenate([vp_ref[rows, sl], vc_ref[rows, sl]], axis=0).astype(BF16)
                o_ref[rows, sl], lse_ref[rows, sl] = _dil_pair_fwd(q_ref[rows, sl].astype(BF16), kcat, vcat, valid, distf, slopes2)

            _for_residues(d, residue)

    width = tiles * PAIR
    cur = pl.BlockSpec((None, rows_per, width), lambda n, hp: (g, n, hp))
    prev = pl.BlockSpec((None, rows_per, width), lambda n, hp: (g, jnp.maximum(n - 1, 0), hp))
    out = pl.BlockSpec((rows_per, width), lambda n, hp: (n, hp))
    return pl.pallas_call(
        body, name=f"dil_fwd_{g}",
        grid=(nblk, ODD_W // width),
        in_specs=[cur, cur, prev, cur, prev],
        out_specs=[out, out],
        out_shape=[jax.ShapeDtypeStruct((s, ODD_W), F32), jax.ShapeDtypeStruct((s, ODD_W), F32)],
        compiler_params=_cparams(("parallel", "parallel"), 40),
    )(q, k, k, v, v)


def dil_bwd(q, k, v, do, lse, delta, g):
    d = DILATIONS[g]
    s = q.shape[1]
    rows_per = BI * d
    nblk = s // rows_per
    tiles = _dil_tiling(d)
    slopes = _slopes(g)

    def body(qc_ref, qn_ref, kc_ref, kp_ref, vc_ref, vp_ref, doc_ref, don_ref, lc_ref, ln_ref, dc_ref, dn_ref,
             dq_ref, dk_ref, dv_ref):
        n = pl.program_id(0)
        masks = _dil_masks(n, nblk, d)
        for t in range(tiles):
            sl = slice(t * PAIR, (t + 1) * PAIR)
            slopes2 = _pair_slopes(slopes, pl.program_id(1) * tiles + t)

            def residue(r):
                rows = pl.ds(r, BI, stride=d)
                b = lambda ref: ref[rows, sl].astype(BF16)
                f = lambda ref: ref[rows, sl]
                dq_ref[rows, sl], dk_ref[rows, sl], dv_ref[rows, sl] = _dil_pair_bwd(
                    b(qc_ref), b(qn_ref), b(kc_ref), b(kp_ref), b(vc_ref), b(vp_ref), b(doc_ref), b(don_ref),
                    f(lc_ref), f(ln_ref), f(dc_ref), f(dn_ref), masks, slopes2)

            _for_residues(d, residue)

    width = tiles * PAIR
    nxt_idx = lambda n: jnp.minimum(n + 1, nblk - 1)
    prv_idx = lambda n: jnp.maximum(n - 1, 0)
    cur3 = pl.BlockSpec((None, rows_per, width), lambda n, hp: (g, n, hp))
    nxt3 = pl.BlockSpec((None, rows_per, width), lambda n, hp: (g, nxt_idx(n), hp))
    prv3 = pl.BlockSpec((None, rows_per, width), lambda n, hp: (g, prv_idx(n), hp))
    cur2 = pl.BlockSpec((rows_per, width), lambda n, hp: (n, hp))
    nxt2 = pl.BlockSpec((rows_per, width), lambda n, hp: (nxt_idx(n), hp))
    shape = jax.ShapeDtypeStruct((s, ODD_W), F32)
    return pl.pallas_call(
        body, name=f"dil_bwd_{g}",
        grid=(nblk, ODD_W // width),
        in_specs=[cur3, nxt3, cur3, prv3, cur3, prv3, cur2, nxt2, cur2, nxt2, cur2, nxt2],
        out_specs=[cur2, cur2, cur2],
        out_shape=[shape, shape, shape],
        compiler_params=_cparams(("parallel", "parallel"), 48),
    )(q, q, k, k, v, v, do, do, lse, lse, delta, delta)


TM = 256


def _rows(tm, w):
    return pl.BlockSpec((tm, w), lambda i: (i, 0))


def _whole(shape):
    return pl.BlockSpec(shape, lambda i: (0,) * len(shape))


def _heads(tm):
    return pl.BlockSpec((NH, tm, HD), lambda i: (0, i, 0))


def _groups(tm):
    return pl.BlockSpec((NG, tm, ODD_W), lambda i: (0, i, 0))


def _rms(x):
    return lax.rsqrt(jnp.mean(x * x, axis=1, keepdims=True) + RMS_EPS)


def _seg_rms(q, hd):
    return lax.rsqrt(_segsum(q * q, hd) * (1.0 / hd) + RMS_EPS)


def _seg_rms_bwd(q_raw, dqs, gain, scale, hd):
    q = q_raw.astype(F32)
    r = _seg_rms(q, hd)
    qhat = q * r
    u = dqs * (gain * scale)
    dq = r * (u - qhat * (_segsum(u * qhat, hd) * (1.0 / hd)))
    return dq, jnp.sum(dqs * qhat, axis=0, keepdims=True) * scale


def _rms_bwd(x, dh, gain):
    r = _rms(x)
    xhat = x * r
    u = dh * gain
    dx = r * (u - xhat * jnp.mean(u * xhat, axis=1, keepdims=True))
    return dx, jnp.sum(dh * xhat, axis=0, keepdims=True)


def even_in_fwd(x, gnorm, w_pack, bf_pad, gq, gk):
    s = x.shape[0]

    def body(x_ref, g_ref, w_ref, bf_ref, gq_ref, gk_ref,
             h_ref, fqs_ref, fkn_ref, fv_ref, fqr_ref, fkr_ref, flog_ref, sqs_ref, sk_ref, sv_ref, gate_ref):
        xt = x_ref[...]
        h = (xt * _rms(xt) * g_ref[...]).astype(BF16)
        h_ref[...] = h
        proj = _dotf(h, w_ref[...])
        fq = proj[:, 0:512]
        fk = proj[:, 512:1024]
        fqs = fq * _seg_rms(fq, HD) * (gq_ref[...] * (SCALE_E * LOG2E))
        fkn = fk * _seg_rms(fk, HD) * gk_ref[...]
        flog_ref[...] = proj[:, FL_OFF:FL_OFF + LANES] + bf_ref[...]
        o = FL_OFF + LANES
        for hh in range(NH):
            sl = slice(hh * HD, (hh + 1) * HD)
            fqs_ref[hh] = fqs[:, sl].astype(BF16)
            fkn_ref[hh] = fkn[:, sl].astype(BF16)
            fqr_ref[hh] = fq[:, sl].astype(BF16)
            fkr_ref[hh] = fk[:, sl].astype(BF16)
            fv_ref[hh] = proj[:, 1024 + hh * HD:1024 + (hh + 1) * HD].astype(BF16)
            sqs_ref[hh] = (proj[:, o + hh * HD:o + (hh + 1) * HD] * (SCALE_E * LOG2E)).astype(BF16)
            sk_ref[hh] = proj[:, o + 512 + hh * HD:o + 512 + (hh + 1) * HD].astype(BF16)
            sv_ref[hh] = proj[:, o + 1024 + hh * HD:o + 1024 + (hh + 1) * HD].astype(BF16)
        gate_ref[...] = proj[:, o + 1536:o + 2560].astype(BF16)

    hs = jax.ShapeDtypeStruct((NH, s, HD), BF16)
    return pl.pallas_call(
        body, name="even_in_fwd",
        grid=(s // TM,),
        in_specs=[_rows(TM, D_MODEL), _whole((1, D_MODEL)), _whole((D_MODEL, EVEN_PACK)), _whole((1, LANES)),
                  _whole((1, 512)), _whole((1, 512))],
        out_specs=[_rows(TM, D_MODEL)] + [_heads(TM)] * 5 + [_rows(TM, LANES)] + [_heads(TM)] * 3 + [_rows(TM, EVEN_W)],
        out_shape=[jax.ShapeDtypeStruct((s, D_MODEL), BF16)] + [hs] * 5 + [jax.ShapeDtypeStruct((s, LANES), F32)]
        + [hs] * 3 + [jax.ShapeDtypeStruct((s, EVEN_W), BF16)],
        compiler_params=_cparams(("parallel",), 52),
    )(x, gnorm, w_pack, bf_pad, gq, gk)


def _prefix_matrices(r):
    a = lax.broadcasted_iota(jnp.int32, (LANES, LANES), 0)
    b = lax.broadcasted_iota(jnp.int32, (LANES, LANES), 1)
    ra = lax.broadcasted_iota(jnp.int32, (r, r), 0)
    rb = lax.broadcasted_iota(jnp.int32, (r, r), 1)
    return a, b, ra, rb


def _dot3_right(x, m):
    a, b, c = _split3(x)
    return _dotf(a, m) + _dotf(b, m) + _dotf(c, m)


def _dot3_left(m, x):
    a, b, c = _split3(x)
    return _dotf(m, a) + _dotf(m, b) + _dotf(m, c)


def fox_cum(flog4):
    nh, r, _ = flog4.shape

    def body(f_ref, nc_ref):
        z = f_ref[...]
        lf = jnp.minimum(z, 0.0) - jnp.log(1.0 + jnp.exp(-jnp.abs(z)))
        a, b, ra, rb = _prefix_matrices(r)
        within = _dot3_right(lf, (a <= b).astype(BF16))
        tot = jnp.broadcast_to(within[:, LANES - 1:LANES], (r, LANES))
        nc_ref[...] = (within + _dot3_left((rb < ra).astype(BF16), tot)) * (-LOG2E)

    return pl.pallas_call(
        body, name="fox_cum", grid=(nh,),
        in_specs=[pl.BlockSpec((None, r, LANES), lambda h: (h, 0, 0))],
        out_specs=pl.BlockSpec((None, r, LANES), lambda h: (h, 0, 0)),
        out_shape=jax.ShapeDtypeStruct((nh, r, LANES), F32),
        compiler_params=_cparams(("parallel",), 16),
    )(flog4)


def fox_cum_bwd(dcum4, flog4):
    nh, r, _ = flog4.shape

    def body(d_ref, f_ref, o_ref):
        a, b, ra, rb = _prefix_matrices(r)
        dc = d_ref[...]
        within = _dot3_right(dc, (a >= b).astype(BF16))
        tot = jnp.broadcast_to(within[:, 0:1], (r, LANES))
        dlf = within + _dot3_left((rb > ra).astype(BF16), tot)
        o_ref[...] = dlf / (1.0 + jnp.exp(f_ref[...]))

    spec = pl.BlockSpec((None, r, LANES), lambda h: (h, 0, 0))
    return pl.pallas_call(
        body, name="fox_cum_bwd", grid=(nh,),
        in_specs=[spec, spec], out_specs=spec,
        out_shape=jax.ShapeDtypeStruct((nh, r, LANES), F32),
        compiler_params=_cparams(("parallel",), 16),
    )(dcum4, flog4)


def even_out_fwd(fo, so, gate, x, w_out):
    s = x.shape[0]
    tm = 2 * TM

    def body(fo_ref, so_ref, g_ref, x_ref, w_ref, y_ref):
        sg = _silu(g_ref[...].astype(F32))
        acc = x_ref[...]
        for hh in range(NH):
            mf = (fo_ref[hh].astype(F32) * sg[:, hh * HD:(hh + 1) * HD]).astype(BF16)
            ms = (so_ref[hh] * sg[:, 512 + hh * HD:512 + (hh + 1) * HD]).astype(BF16)
            acc = acc + _dotf(mf, w_ref[hh * HD:(hh + 1) * HD, :]) + _dotf(ms, w_ref[512 + hh * HD:512 + (hh + 1) * HD, :])
        y_ref[...] = acc

    return pl.pallas_call(
        body, name="even_out_fwd", grid=(s // tm,),
        in_specs=[_heads(tm), _heads(tm), _rows(tm, EVEN_W), _rows(tm, D_MODEL), _whole((EVEN_W, D_MODEL))],
        out_specs=_rows(tm, D_MODEL),
        out_shape=jax.ShapeDtypeStruct((s, D_MODEL), F32),
        compiler_params=_cparams(("parallel",), 40),
    )(fo, so, gate, x, w_out)


def odd_in_fwd(y1, gnorm, w2, gq, gk):
    s = y1.shape[0]

    def body(x_ref, g_ref, w_ref, gq_ref, gk_ref, h_ref, qs_ref, kn_ref, v_ref, qr_ref, kr_ref, gate_ref):
        xt = x_ref[...]
        h = (xt * _rms(xt) * g_ref[...]).astype(BF16)
        h_ref[...] = h
        proj = _dotf(h, w_ref[...])
        for g in range(NG):
            q = proj[:, g * ODD_W:(g + 1) * ODD_W]
            k = proj[:, 1536 + g * ODD_W:1536 + (g + 1) * ODD_W]
            qs_ref[g] = q * _seg_rms(q, HD2) * (gq_ref[...] * SCALE_O)
            kn_ref[g] = k * _seg_rms(k, HD2) * gk_ref[...]
            qr_ref[g] = q.astype(BF16)
            kr_ref[g] = k.astype(BF16)
            v_ref[g] = proj[:, 3072 + g * ODD_W:3072 + (g + 1) * ODD_W]
        gate_ref[...] = proj[:, 4608:5120].astype(BF16)

    gs = lambda dt: jax.ShapeDtypeStruct((NG, s, ODD_W), dt)
    return pl.pallas_call(
        body, name="odd_in_fwd", grid=(s // TM,),
        in_specs=[_rows(TM, D_MODEL), _whole((1, D_MODEL)), _whole((D_MODEL, ODD_IN)), _whole((1, ODD_W)), _whole((1, ODD_W))],
        out_specs=[_rows(TM, D_MODEL)] + [_groups(TM)] * 5 + [_rows(TM, ODD_W)],
        out_shape=[jax.ShapeDtypeStruct((s, D_MODEL), BF16), gs(F32), gs(F32), gs(F32), gs(BF16), gs(BF16),
                   jax.ShapeDtypeStruct((s, ODD_W), BF16)],
        compiler_params=_cparams(("parallel",), 52),
    )(y1, gnorm, w2, gq, gk)


def odd_out_fwd(o0, l0, o1, l1, o2, l2, gate2, y1, target, w_out2):
    s = y1.shape[0]
    tm = 2 * TM
    nt = s // tm

    def body(o0_ref, l0_ref, o1_ref, l1_ref, o2_ref, l2_ref, g_ref, y1_ref, t_ref, w_ref,
             att_ref, lse_ref, dy_ref, loss_ref):
        l0t, l1t, l2t = l0_ref[...], l1_ref[...], l2_ref[...]
        m = jnp.maximum(jnp.maximum(l0t, l1t), l2t)
        e0, e1, e2 = jnp.exp(l0t - m), jnp.exp(l1t - m), jnp.exp(l2t - m)
        den = e0 + e1 + e2
        att = (e0 * o0_ref[...] + e1 * o1_ref[...] + e2 * o2_ref[...]) / den
        att_ref[...] = att.astype(BF16)
        lse_ref[...] = m + jnp.log(den)
        mixed = (att * _silu(g_ref[...].astype(F32))).astype(BF16)
        diff = y1_ref[...] + _dotf(mixed, w_ref[...]) - t_ref[...]
        dy_ref[...] = diff * (1.0 / D_MODEL)
        loss_ref[...] = jnp.full((1, 1, LANES), 0.5 / D_MODEL, F32) * jnp.sum(diff * diff)

    big = jax.ShapeDtypeStruct((s, ODD_W), F32)
    return pl.pallas_call(
        body, name="odd_out_fwd", grid=(nt,),
        in_specs=[_rows(tm, ODD_W)] * 7 + [_rows(tm, D_MODEL), _rows(tm, D_MODEL), _whole((ODD_W, D_MODEL))],
        out_specs=[_rows(tm, ODD_W), _rows(tm, ODD_W), _rows(tm, D_MODEL), pl.BlockSpec((1, 1, LANES), lambda i: (i, 0, 0))],
        out_shape=[jax.ShapeDtypeStruct((s, ODD_W), BF16), big, jax.ShapeDtypeStruct((s, D_MODEL), F32),
                   jax.ShapeDtypeStruct((nt, 1, LANES), F32)],
        compiler_params=_cparams(("parallel",), 40),
    )(o0, l0, o1, l1, o2, l2, gate2, y1, target, w_out2)


def odd_out_bwd(dy2, w_out2_t, att, gate2):
    s = dy2.shape[0]
    tm = 2 * TM

    def body(dy_ref, wt_ref, att_ref, g_ref, datt_ref, dgate_ref, delta_ref, dw_ref):
        @pl.when(pl.program_id(0) == 0)
        def _():
            dw_ref[...] = jnp.zeros_like(dw_ref)

        dyb = dy_ref[...].astype(BF16)
        dmixed = _dotf(dyb, wt_ref[...])
        g = g_ref[...].astype(F32)
        att_t = att_ref[...].astype(F32)
        sg = _silu(g)
        datt = (dmixed * sg).astype(BF16).astype(F32)
        datt_ref[...] = datt
        dgate_ref[...] = (dmixed * att_t * _dsilu(g)).astype(BF16)
        delta_ref[...] = _segsum(datt * att_t, HD2)
        dw_ref[...] += _dot_tn((att_t * sg).astype(BF16), dyb)

    return pl.pallas_call(
        body, name="odd_out_bwd", grid=(s // tm,),
        in_specs=[_rows(tm, D_MODEL), _whole((D_MODEL, ODD_W)), _rows(tm, ODD_W), _rows(tm, ODD_W)],
        out_specs=[_rows(tm, ODD_W), _rows(tm, ODD_W), _rows(tm, ODD_W), _whole((ODD_W, D_MODEL))],
        out_shape=[jax.ShapeDtypeStruct((s, ODD_W), F32), jax.ShapeDtypeStruct((s, ODD_W), BF16),
                   jax.ShapeDtypeStruct((s, ODD_W), F32), jax.ShapeDtypeStruct((ODD_W, D_MODEL), F32)],
        compiler_params=_cparams(("arbitrary",), 40),
    )(dy2, w_out2_t, att, gate2)


def odd_in_bwd(dqs, dks, dvs, dgate2, q2r, k2r, gq, gk, w2_t, y1, dy2, gnorm):
    s = y1.shape[0]

    def body(dq0, dq1, dq2, dk0, dk1, dk2, dv0, dv1, dv2, dg_ref, qr_ref, kr_ref, gq_ref, gk_ref, wt_ref, y1_ref, dy_ref, gn_ref,
             dproj_ref, dy1_ref, dgn_ref, dgain_ref):
        @pl.when(pl.program_id(0) == 0)
        def _():
            dgn_ref[...] = jnp.zeros_like(dgn_ref)
            dgain_ref[...] = jnp.zeros_like(dgain_ref)

        for g, (dq_ref, dk_ref, dv_ref) in enumerate(((dq0, dk0, dv0), (dq1, dk1, dv1), (dq2, dk2, dv2))):
            dq, gq_row = _seg_rms_bwd(qr_ref[g], dq_ref[...], gq_ref[...], SCALE_O, HD2)
            dk, gk_row = _seg_rms_bwd(kr_ref[g], dk_ref[...], gk_ref[...], 1.0, HD2)
            dproj_ref[:, g * ODD_W:(g + 1) * ODD_W] = dq.astype(BF16)
            dproj_ref[:, 1536 + g * ODD_W:1536 + (g + 1) * ODD_W] = dk.astype(BF16)
            dproj_ref[:, 3072 + g * ODD_W:3072 + (g + 1) * ODD_W] = dv_ref[...].astype(BF16)
            dgain_ref[g:g + 1, :] += gq_row
            dgain_ref[NG + g:NG + g + 1, :] += gk_row
        dproj_ref[:, 4608:5120] = dg_ref[...]
        dh = _dotf(dproj_ref[...], wt_ref[...])
        dx, gn_row = _rms_bwd(y1_ref[...], dh, gn_ref[...])
        dy1_ref[...] = dy_ref[...] + dx
        dgn_ref[...] += gn_row

    f32r, bf16r = _rows(TM, ODD_W), _rows(TM, ODD_W)
    return pl.pallas_call(
        body, name="odd_in_bwd", grid=(s // TM,),
        in_specs=[f32r] * 6 + [bf16r] * 4 + [_groups(TM), _groups(TM), _whole((1, ODD_W)), _whole((1, ODD_W)),
                                             _whole((ODD_IN, D_MODEL)), _rows(TM, D_MODEL), _rows(TM, D_MODEL), _whole((1, D_MODEL))],
        out_specs=[_rows(TM, ODD_IN), _rows(TM, D_MODEL), _whole((1, D_MODEL)), _whole((8, ODD_W))],
        out_shape=[jax.ShapeDtypeStruct((s, ODD_IN), BF16), jax.ShapeDtypeStruct((s, D_MODEL), F32),
                   jax.ShapeDtypeStruct((1, D_MODEL), F32), jax.ShapeDtypeStruct((8, ODD_W), F32)],
        compiler_params=_cparams(("arbitrary",), 52),
    )(*dqs, *dks, *dvs, dgate2, q2r, k2r, gq, gk, w2_t, y1, dy2, gnorm)


def even_out_bwd(dy1, w_out_t, fo, so, gate):
    s = dy1.shape[0]
    tm = 2 * TM

    def body(dy_ref, wt_ref, fo_ref, so_ref, g_ref, dfo_ref, dso_ref, dgate_ref, delf_ref, dels_ref, dw_ref):
        @pl.when(pl.program_id(0) == 0)
        def _():
            dw_ref[...] = jnp.zeros_like(dw_ref)

        dyb = dy_ref[...].astype(BF16)
        dmixed = _dotf(dyb, wt_ref[...])
        g = g_ref[...].astype(F32)
        sg, dsg = _silu(g), _dsilu(g)
        for hh in range(NH):
            for base, o_ref, do_ref, del_ref in ((0, fo_ref, dfo_ref, delf_ref), (512, so_ref, dso_ref, dels_ref)):
                sl = slice(base + hh * HD, base + (hh + 1) * HD)
                o = o_ref[hh].astype(F32)
                do = (dmixed[:, sl] * sg[:, sl]).astype(BF16)
                do_ref[hh] = do
                del_ref[hh] = jnp.sum(do.astype(F32) * o, axis=1, keepdims=True)
                dgate_ref[:, sl] = (dmixed[:, sl] * o * dsg[:, sl]).astype(BF16)
                dw_ref[sl, :] += _dot_tn((o * sg[:, sl]).astype(BF16), dyb)

    cols = pl.BlockSpec((NH, tm, 1), lambda i: (0, i, 0))
    hs = jax.ShapeDtypeStruct((NH, s, HD), BF16)
    cs = jax.ShapeDtypeStruct((NH, s, 1), F32)
    return pl.pallas_call(
        body, name="even_out_bwd", grid=(s // tm,),
        in_specs=[_rows(tm, D_MODEL), _whole((D_MODEL, EVEN_W)), _heads(tm), _heads(tm), _rows(tm, EVEN_W)],
        out_specs=[_heads(tm), _heads(tm), _rows(tm, EVEN_W), cols, cols, _whole((EVEN_W, D_MODEL))],
        out_shape=[hs, hs, jax.ShapeDtypeStruct((s, EVEN_W), BF16), cs, cs, jax.ShapeDtypeStruct((EVEN_W, D_MODEL), F32)],
        compiler_params=_cparams(("arbitrary",), 48),
    )(dy1, w_out_t, fo, so, gate)


def even_in_bwd(dfqs, dfkn, dfv, dsqs, dsk, dsv, dgate, dflog, fqr, fkr, gq, gk, w_pack_t, x, dy1, gnorm):
    s = x.shape[0]

    def body(dfq_ref, dfk_ref, dfv_ref, dsq_ref, dsk_ref, dsv_ref, dg_ref, dfl_ref, qr_ref, kr_ref, gq_ref, gk_ref,
             wt_ref, x_ref, dy_ref, gn_ref, dproj_ref, dx_ref, dgn_ref, dgain_ref, dbf_ref):
        @pl.when(pl.program_id(0) == 0)
        def _():
            dgn_ref[...] = jnp.zeros_like(dgn_ref)
            dgain_ref[...] = jnp.zeros_like(dgain_ref)
            dbf_ref[...] = jnp.zeros_like(dbf_ref)

        o = FL_OFF + LANES
        for hh in range(NH):
            sl = slice(hh * HD, (hh + 1) * HD)
            dq, gq_row = _seg_rms_bwd(qr_ref[hh], dfq_ref[hh], gq_ref[:, sl], SCALE_E, HD)
            dk, gk_row = _seg_rms_bwd(kr_ref[hh], dfk_ref[hh] * LN2, gk_ref[:, sl], 1.0, HD)
            dproj_ref[:, sl] = dq.astype(BF16)
            dproj_ref[:, 512 + hh * HD:512 + (hh + 1) * HD] = dk.astype(BF16)
            dproj_ref[:, 1024 + hh * HD:1024 + (hh + 1) * HD] = dfv_ref[hh].astype(BF16)
            dproj_ref[:, o + hh * HD:o + (hh + 1) * HD] = (dsq_ref[hh] * SCALE_E).astype(BF16)
            dproj_ref[:, o + 512 + hh * HD:o + 512 + (hh + 1) * HD] = (dsk_ref[hh] * LN2).astype(BF16)
            dproj_ref[:, o + 1024 + hh * HD:o + 1024 + (hh + 1) * HD] = dsv_ref[hh].astype(BF16)
            dgain_ref[0:1, sl] += gq_row
            dgain_ref[1:2, sl] += gk_row
        dfl = dfl_ref[...]
        dproj_ref[:, FL_OFF:FL_OFF + LANES] = dfl.astype(BF16)
        dbf_ref[...] += jnp.sum(dfl, axis=0, keepdims=True)
        dproj_ref[:, o + 1536:o + 2560] = dg_ref[...]
        dh = _dotf(dproj_ref[...], wt_ref[...])
        dx, gn_row = _rms_bwd(x_ref[...], dh, gn_ref[...])
        dx_ref[...] = dy_ref[...] + dx
        dgn_ref[...] += gn_row

    return pl.pallas_call(
        body, name="even_in_bwd", grid=(s // TM,),
        in_specs=[_heads(TM)] * 6 + [_rows(TM, EVEN_W), _rows(TM, LANES), _heads(TM), _heads(TM), _whole((1, 512)), _whole((1, 512)),
                                     _whole((EVEN_PACK, D_MODEL)), _rows(TM, D_MODEL), _rows(TM, D_MODEL), _whole((1, D_MODEL))],
        out_specs=[_rows(TM, EVEN_PACK), _rows(TM, D_MODEL), _whole((1, D_MODEL)), _whole((8, 512)), _whole((1, LANES))],
        out_shape=[jax.ShapeDtypeStruct((s, EVEN_PACK), BF16), jax.ShapeDtypeStruct((s, D_MODEL), F32),
                   jax.ShapeDtypeStruct((1, D_MODEL), F32), jax.ShapeDtypeStruct((8, 512), F32), jax.ShapeDtypeStruct((1, LANES), F32)],
        compiler_params=_cparams(("arbitrary",), 52),
    )(dfqs, dfkn, dfv, dsqs, dsk, dsv, dgate, dflog, fqr, fkr, gq, gk, w_pack_t, x, dy1, gnorm)


def matmul_tn(a, b, tn, name):
    s, m = a.shape
    n = b.shape[1]
    tk = 2 * TM
    nk = s // tk

    def body(a_ref, b_ref, o_ref):
        @pl.when(pl.program_id(1) == 0)
        def _():
            o_ref[...] = jnp.zeros_like(o_ref)

        o_ref[...] += _dot_tn(a_ref[...], b_ref[...])

    return pl.pallas_call(
        body, name=name, grid=(n // tn, nk),
        in_specs=[pl.BlockSpec((tk, m), lambda j, k: (k, 0)), pl.BlockSpec((tk, tn), lambda j, k: (k, j))],
        out_specs=pl.BlockSpec((m, tn), lambda j, k: (0, j)),
        out_shape=jax.ShapeDtypeStruct((m, n), F32),
        compiler_params=_cparams(("parallel", "arbitrary"), 32),
    )(a, b)


def _tile_gain(g, reps):
    return jnp.tile(g.reshape(1, -1), (1, reps))


def local_step(x, target, w_in_e, b_f, gq_e, gk_e, gn_e, w_out_e, gn_o, w_in_o, gq_o, gk_o, w_out_o, fox_blocks, sb_blocks):
    s = x.shape[0]
    r = s // LANES
    w_pack = jnp.concatenate([w_in_e[:, :FL_OFF + NH], jnp.zeros((D_MODEL, LANES - NH), BF16), w_in_e[:, FL_OFF + NH:]], axis=1)
    bf_pad = jnp.pad(b_f.reshape(1, NH), ((0, 0), (0, LANES - NH)))
    gq512, gk512 = _tile_gain(gq_e, NH), _tile_gain(gk_e, NH)
    gq2, gk2 = _tile_gain(gq_o, NH2), _tile_gain(gk_o, NH2)
    gn_e, gn_o = gn_e.reshape(1, D_MODEL), gn_o.reshape(1, D_MODEL)

    h, fqs, fkn, fv, fqr, fkr, flog, sqs, sk, sv, gate = even_in_fwd(x, gn_e, w_pack, bf_pad, gq512, gk512)
    flog4 = flog[:, :NH].T.reshape(NH, r, LANES)
    nc = fox_cum(flog4).reshape(NH, 1, s)
    fo, lse = fox_fwd(fqs, fkn, fv, nc, *fox_blocks)
    so = sb_fwd(sqs, sk, sv, *sb_blocks)
    y1 = even_out_fwd(fo, so, gate, x, w_out_e)
    h2, q2s, k2n, v2, q2r, k2r, gate2 = odd_in_fwd(y1, gn_o, w_in_o, gq2, gk2)
    ol = [dil_fwd(q2s, k2n, v2, g) for g in range(NG)]
    att, lse2, dy2, loss_parts = odd_out_fwd(ol[0][0], ol[0][1], ol[1][0], ol[1][1], ol[2][0], ol[2][1], gate2, y1, target, w_out_o)
    loss = jnp.sum(loss_parts[:, 0, 0])
    datt, dgate2, delta2, d_w_out_o = odd_out_bwd(dy2, w_out_o.T, att, gate2)
    dqkv = [dil_bwd(q2s, k2n, v2, datt, lse2, delta2, g) for g in range(NG)]
    dproj2, dy1, d_gn_o, dgain_o = odd_in_bwd([t[0] for t in dqkv], [t[1] for t in dqkv], [t[2] for t in dqkv], dgate2,
                                              q2r, k2r, gq2, gk2, w_in_o.T, y1, dy2, gn_o)
    d_w_in_o = matmul_tn(h2, dproj2, 512, "dw_in_odd")
    dfo, dso, dgate, delta_f, delta_s, d_w_out_e = even_out_bwd(dy1, w_out_e.T, fo, so, gate)
    dfqs, dfkn, dfv, dnc, drow = fox_bwd(fqs, fkn, fv, nc, lse, dfo, delta_f, fox_blocks[0], fox_blocks[1] // 2)
    dsqs, dsk, dsv = sb_bwd(sqs, sk, sv, dso, delta_s, *sb_blocks)
    dcum4 = (drow.reshape(NH, s) - dnc.reshape(NH, s)).reshape(NH, r, LANES)
    dflog4 = fox_cum_bwd(dcum4, flog4)
    dflog = jnp.pad(dflog4.reshape(NH, s).T, ((0, 0), (0, LANES - NH)))
    dproj, grad_x, d_gn_e, dgain_e, d_bf = even_in_bwd(dfqs, dfkn, dfv, dsqs, dsk, dsv, dgate, dflog, fqr, fkr, gq512, gk512,
                                                       w_pack.T, x, dy1, gn_e)
    d_w_pack = matmul_tn(h, dproj, 384, "dw_in_even")
    d_w_in_e = jnp.concatenate([d_w_pack[:, :FL_OFF + NH], d_w_pack[:, FL_OFF + LANES:]], axis=1)
    grads = dict(
        even_norm=d_gn_e.reshape(-1), even_w_in=d_w_in_e, even_b_f=d_bf[0, :NH],
        even_q_gain=dgain_e[0].reshape(NH, HD).sum(0), even_k_gain=dgain_e[1].reshape(NH, HD).sum(0),
        even_w_out=d_w_out_e, odd_norm=d_gn_o.reshape(-1), odd_w_in=d_w_in_o,
        odd_q_gain=dgain_o[:NG].reshape(NG * NH2, HD2).sum(0), odd_k_gain=dgain_o[NG:2 * NG].reshape(NG * NH2, HD2).sum(0),
        odd_w_out=d_w_out_o)
    return loss, grad_x, grads


SHARDED = (("even_w_in", (D_MODEL, EVEN_IN // 4)), ("even_w_out", (EVEN_W // 4, D_MODEL)), ("odd_norm", (D_MODEL // 4,)),
           ("odd_norm_lo", (D_MODEL // 4,)), ("odd_w_in", (D_MODEL, ODD_IN // 4)), ("odd_w_out", (ODD_W, D_MODEL // 4)))
REPLICATED = (("even_norm", (D_MODEL,)), ("even_b_f", (NH,)), ("even_q_gain", (HD,)), ("even_k_gain", (HD,)),
              ("odd_q_gain", (HD2,)), ("odd_k_gain", (HD2,)))
PACK_ELEMS = sum(int(np.prod(shape)) for _, shape in SHARDED + REPLICATED)
PACK_ROWS = -(-PACK_ELEMS // (D_MODEL * 32)) * 32
HALF = PACK_ROWS // 2
HBM = pl.BlockSpec(memory_space=pl.ANY)


def _pack(parts):
    flat = [parts[n].reshape(-1) for n, _ in SHARDED + REPLICATED]
    flat.append(jnp.zeros((PACK_ROWS * D_MODEL - PACK_ELEMS,), flat[0].dtype))
    return jnp.concatenate(flat).reshape(PACK_ROWS, D_MODEL)


def _unpack(buf):
    flat = buf.reshape(-1)
    out, off = {}, 0
    for n, shape in SHARDED + REPLICATED:
        size = int(np.prod(shape))
        out[n] = flat[off:off + size].reshape(shape)
        off += size
    return out


def _place():
    x, y, c = lax.axis_index("x"), lax.axis_index("y"), lax.axis_index("c")
    return x, y, c, [(1 - x, y), (x, 1 - y), (1 - x, 1 - y)]


def all_gather_shards(mine):
    def body(src_ref, out_ref, send_sems, recv_sems, local_sem):
        x, y, c, chips = _place()
        me = 2 * x + y
        half = lambda cc: pl.ds(cc * HALF, HALF)

        def copy(k, j, cc, to, src=None):
            dst = out_ref.at[j, half(cc)]
            return pltpu.make_async_remote_copy(src_ref=dst if src is None else src, dst_ref=dst,
                                                send_sem=send_sems.at[k], recv_sem=recv_sems.at[k],
                                                device_id=to, device_id_type=MESH)

        local = pltpu.make_async_copy(src_ref, out_ref.at[me], local_sem)
        local.start()
        first = [copy(k, me, c, (cx, cy, c), src=src_ref.at[half(c)]) for k, (cx, cy) in enumerate(chips)]
        for cp in first:
            cp.start()
        passed = [copy(3 + k, 2 * cx + cy, c, (x, y, 1 - c)) for k, (cx, cy) in enumerate(chips)]
        for k, (cx, cy) in enumerate(chips):
            copy(k, 2 * cx + cy, c, (x, y, c)).wait_recv()
            passed[k].start()
        for k, (cx, cy) in enumerate(chips):
            copy(3 + k, 2 * cx + cy, 1 - c, (x, y, c)).wait_recv()
        for cp in first + passed:
            cp.wait_send()
        local.wait()

    return pl.pallas_call(
        body, name="all_gather_shards",
        in_specs=[HBM], out_specs=HBM,
        out_shape=jax.ShapeDtypeStruct((4, PACK_ROWS, D_MODEL), mine.dtype),
        scratch_shapes=[pltpu.SemaphoreType.DMA((6,)), pltpu.SemaphoreType.DMA((6,)), pltpu.SemaphoreType.DMA],
    )(mine)


def sibling_swap_halves(g):
    def body(g_ref, a_ref, send_sem, recv_sem):
        x, y, c, _ = _place()
        cp = pltpu.make_async_remote_copy(src_ref=g_ref.at[:, pl.ds((1 - c) * HALF, HALF)], dst_ref=a_ref,
                                          send_sem=send_sem, recv_sem=recv_sem, device_id=(x, y, 1 - c), device_id_type=MESH)
        cp.start()
        cp.wait()

    return pl.pallas_call(
        body, name="sibling_swap_halves",
        in_specs=[HBM], out_specs=HBM,
        out_shape=jax.ShapeDtypeStruct((4, HALF, D_MODEL), g.dtype),
        scratch_shapes=[pltpu.SemaphoreType.DMA, pltpu.SemaphoreType.DMA],
    )(g)


def chip_exchange(p):
    def body(p_ref, b_ref, send_sems, recv_sems, local_sem):
        x, y, c, chips = _place()
        me = 2 * x + y
        local = pltpu.make_async_copy(p_ref.at[me], b_ref.at[me], local_sem)
        local.start()
        sends = [pltpu.make_async_remote_copy(src_ref=p_ref.at[2 * cx + cy], dst_ref=b_ref.at[me],
                                              send_sem=send_sems.at[k], recv_sem=recv_sems.at[k],
                                              device_id=(cx, cy, c), device_id_type=MESH)
                 for k, (cx, cy) in enumerate(chips)]
        for cp in sends:
            cp.start()
        for k, (cx, cy) in enumerate(chips):
            pltpu.make_async_remote_copy(src_ref=p_ref.at[me], dst_ref=b_ref.at[2 * cx + cy],
                                         send_sem=send_sems.at[k], recv_sem=recv_sems.at[k],
                                         device_id=(cx, cy, c), device_id_type=MESH).wait_recv()
        for cp in sends:
            cp.wait_send()
        local.wait()

    return pl.pallas_call(
        body, name="chip_exchange",
        in_specs=[HBM], out_specs=HBM,
        out_shape=jax.ShapeDtypeStruct((4, HALF, D_MODEL), p.dtype),
        scratch_shapes=[pltpu.SemaphoreType.DMA((3,)), pltpu.SemaphoreType.DMA((3,)), pltpu.SemaphoreType.DMA],
    )(p)


def sibling_join_halves(mine):
    def body(h_ref, out_ref, send_sem, recv_sem, local_sem):
        x, y, c, _ = _place()
        local = pltpu.make_async_copy(h_ref, out_ref.at[pl.ds(c * HALF, HALF)], local_sem)
        local.start()
        cp = pltpu.make_async_remote_copy(src_ref=h_ref, dst_ref=out_ref.at[pl.ds(c * HALF, HALF)],
                                          send_sem=send_sem, recv_sem=recv_sem, device_id=(x, y, 1 - c), device_id_type=MESH)
        cp.start()
        cp.wait_send()
        pltpu.make_async_remote_copy(src_ref=h_ref, dst_ref=out_ref.at[pl.ds((1 - c) * HALF, HALF)],
                                     send_sem=send_sem, recv_sem=recv_sem, device_id=(x, y, 1 - c), device_id_type=MESH).wait_recv()
        local.wait()

    return pl.pallas_call(
        body, name="sibling_join_halves",
        in_specs=[HBM], out_specs=HBM,
        out_shape=jax.ShapeDtypeStruct((PACK_ROWS, D_MODEL), mine.dtype),
        scratch_shapes=[pltpu.SemaphoreType.DMA, pltpu.SemaphoreType.DMA, pltpu.SemaphoreType.DMA],
    )(mine)


def _sum_call(name, arrays, rows):
    tr = rows // 5 if rows % 40 == 0 else rows

    def body(*refs):
        acc = refs[0][...]
        for r in refs[1:-1]:
            acc = acc + r[...]
        refs[-1][...] = acc

    spec = pl.BlockSpec((tr, D_MODEL), lambda i: (i, 0))
    return pl.pallas_call(
        body, name=name, grid=(rows // tr,),
        in_specs=[spec] * len(arrays), out_specs=spec,
        out_shape=jax.ShapeDtypeStruct((rows, D_MODEL), F32),
        compiler_params=_cparams(("parallel",), 40),
    )(*arrays)


def adamw(w, g, m, v):
    tr = PACK_ROWS // 5

    def body(w_ref, g_ref, m_ref, v_ref, d_ref, nm_ref, nv_ref):
        gt = g_ref[...]
        nm = ADAM_B1 * m_ref[...] + (1.0 - ADAM_B1) * gt
        nv = ADAM_B2 * v_ref[...] + (1.0 - ADAM_B2) * (gt * gt)
        m_hat = nm / (1.0 - ADAM_B1 ** ADAM_STEP)
        v_hat = nv / (1.0 - ADAM_B2 ** ADAM_STEP)
        d_ref[...] = -ADAM_LR * (m_hat / (jnp.sqrt(v_hat) + ADAM_EPS) + ADAM_WD * w_ref[...])
        nm_ref[...] = nm
        nv_ref[...] = nv

    spec = pl.BlockSpec((tr, D_MODEL), lambda i: (i, 0))
    shape = jax.ShapeDtypeStruct((PACK_ROWS, D_MODEL), F32)
    return pl.pallas_call(
        body, name="adamw", grid=(PACK_ROWS // tr,),
        in_specs=[spec] * 4, out_specs=[spec] * 3, out_shape=[shape] * 3,
        compiler_params=_cparams(("parallel",), 40),
    )(w, g, m, v)


def kernel(x, even_norm, even_w_in, even_b_f, even_q_gain, even_k_gain, even_w_out, odd_norm, odd_w_in, odd_q_gain, odd_k_gain, odd_w_out, loss_target, m_even_norm, m_even_w_in, m_even_b_f, m_even_q_gain, m_even_k_gain, m_even_w_out, m_odd_norm, m_odd_w_in, m_odd_q_gain, m_odd_k_gain, m_odd_w_out, v_even_norm, v_even_w_in, v_even_b_f, v_even_q_gain, v_even_k_gain, v_even_w_out, v_odd_norm, v_odd_w_in, v_odd_q_gain, v_odd_k_gain, v_odd_w_out):
    names = [n for n, _ in SHARDED + REPLICATED if n != "odd_norm_lo"]
    w = dict(even_norm=even_norm, even_w_in=even_w_in, even_b_f=even_b_f, even_q_gain=even_q_gain, even_k_gain=even_k_gain,
             even_w_out=even_w_out, odd_norm=odd_norm, odd_w_in=odd_w_in, odd_q_gain=odd_q_gain, odd_k_gain=odd_k_gain,
             odd_w_out=odd_w_out)
    m = dict(even_norm=m_even_norm, even_w_in=m_even_w_in, even_b_f=m_even_b_f, even_q_gain=m_even_q_gain,
             even_k_gain=m_even_k_gain, even_w_out=m_even_w_out, odd_norm=m_odd_norm, odd_w_in=m_odd_w_in,
             odd_q_gain=m_odd_q_gain, odd_k_gain=m_odd_k_gain, odd_w_out=m_odd_w_out)
    v = dict(even_norm=v_even_norm, even_w_in=v_even_w_in, even_b_f=v_even_b_f, even_q_gain=v_even_q_gain,
             even_k_gain=v_even_k_gain, even_w_out=v_even_w_out, odd_norm=v_odd_norm, odd_w_in=v_odd_w_in,
             odd_q_gain=v_odd_q_gain, odd_k_gain=v_odd_k_gain, odd_w_out=v_odd_w_out)
    spare = jnp.zeros((D_MODEL // 4,), F32)

    on = odd_norm.reshape(-1)
    on_hi = on.astype(BF16)
    wire = {n: w[n].astype(BF16) for n in names}
    wire["odd_norm"] = on_hi
    wire["odd_norm_lo"] = (on - on_hi.astype(F32)).astype(BF16)
    gathered = all_gather_shards(_pack(wire))
    sh = [_unpack(gathered[j]) for j in range(4)]
    cat = lambda n, axis: jnp.concatenate([t[n] for t in sh], axis=axis)
    gn_o = cat("odd_norm", 0).astype(F32) + cat("odd_norm_lo", 0).astype(F32)

    s = x.shape[1]
    bq = min(512, s)
    loss_local, grad_x, g = local_step(
        x[0], loss_target[0], cat("even_w_in", 1), even_b_f[0], even_q_gain[0], even_k_gain[0], even_norm[0],
        cat("even_w_out", 0), gn_o, cat("odd_w_in", 1), odd_q_gain[0], odd_k_gain[0], cat("odd_w_out", 1),
        (min(1024, s), min(512, s)), (min(1024, s), min(256, s)))
    loss = lax.psum(loss_local, ("x", "y", "c"))

    def grad_parts(j):
        parts = {n: g[n] for n, _ in REPLICATED}
        parts["even_w_in"] = g["even_w_in"][:, j * (EVEN_IN // 4):(j + 1) * (EVEN_IN // 4)]
        parts["even_w_out"] = g["even_w_out"][j * (EVEN_W // 4):(j + 1) * (EVEN_W // 4), :]
        parts["odd_norm"] = g["odd_norm"][j * (D_MODEL // 4):(j + 1) * (D_MODEL // 4)]
        parts["odd_norm_lo"] = spare
        parts["odd_w_in"] = g["odd_w_in"][:, j * (ODD_IN // 4):(j + 1) * (ODD_IN // 4)]
        parts["odd_w_out"] = g["odd_w_out"][:, j * (D_MODEL // 4):(j + 1) * (D_MODEL // 4)]
        return parts

    g_all = jnp.stack([_pack(grad_parts(j)) for j in range(4)])
    from_sibling = sibling_swap_halves(g_all)
    c = lax.axis_index("c")
    g_mine = lax.dynamic_slice_in_dim(g_all, c * HALF, HALF, axis=1)
    pair = _sum_call("pair_sum", [g_mine.reshape(4 * HALF, D_MODEL), from_sibling.reshape(4 * HALF, D_MODEL)], 4 * HALF)
    by_chip = chip_exchange(pair.reshape(4, HALF, D_MODEL))
    half_sum = _sum_call("chip_sum", [by_chip[0], by_chip[1], by_chip[2], by_chip[3]], HALF)
    g_buf = sibling_join_halves(half_sum)

    local = lambda d: _pack({**{n: d[n].astype(F32) for n in names}, "odd_norm_lo": spare})
    delta_buf, m_buf, v_buf = adamw(local(w), g_buf, local(m), local(v))
    outs = [loss.reshape(()), grad_x.reshape(x.shape)]
    order = ["even_norm", "even_w_in", "even_b_f", "even_q_gain", "even_k_gain", "even_w_out", "odd_norm", "odd_w_in",
             "odd_q_gain", "odd_k_gain", "odd_w_out"]
    for buf in (g_buf, delta_buf, m_buf, v_buf):
        parts = _unpack(buf)
        outs += [parts[n].reshape(w[n].shape) for n in order]
    return tuple(outs)
```

```python
import jax
import jax.numpy as jnp
from jax import lax
from jax.experimental import pallas as pl
from jax.experimental.pallas import tpu as pltpu

F32 = jnp.float32
BF16 = jnp.bfloat16

D_MODEL = 1024
HD = 128
NH = 4
HD2 = 64
NG = 3
NH2 = 8
DILATIONS = (1, 4, 16)
SPAN = 128
EVEN_W = 1024
ODD_W = 512
EVEN_IN = 4100
EVEN_PACK = 4224
FL_OFF = 1536
ODD_IN = 5120
RMS_EPS = 1e-6
SCALE_E = HD ** -0.5
SCALE_O = HD2 ** -0.5
ADAM_LR, ADAM_B1, ADAM_B2, ADAM_EPS, ADAM_WD, ADAM_STEP = 0.001, 0.9, 0.999, 1e-08, 0.01, 10

VMEM_CAP = 64 * 1024 * 1024
LANES = 128
MESH = pl.DeviceIdType.MESH


def _cparams(sem, vmem_mb):
    return pltpu.CompilerParams(dimension_semantics=sem, vmem_limit_bytes=min(vmem_mb << 20, VMEM_CAP - (6 << 20)))


def _silu(g):
    return g / (1.0 + jnp.exp(-g))


def _dsilu(g):
    s = 1.0 / (1.0 + jnp.exp(-g))
    return s * (1.0 + g * (1.0 - s))


def _split2(x):
    hi = x.astype(BF16)
    lo = (x - hi.astype(F32)).astype(BF16)
    return hi, lo


def _split3(x):
    hi = x.astype(BF16)
    r = x - hi.astype(F32)
    mid = r.astype(BF16)
    lo = (r - mid.astype(F32)).astype(BF16)
    return hi, mid, lo


def _dotf(a, b):
    return jnp.dot(a, b, preferred_element_type=F32)


def _dot_nt(a, b):
    return lax.dot_general(a, b, (((1,), (1,)), ((), ())), preferred_element_type=F32)


def _dot_tn(a, b):
    return lax.dot_general(a, b, (((0,), (0,)), ((), ())), preferred_element_type=F32)


def _segsum(x, hd):
    r = lax.broadcasted_iota(jnp.int32, (LANES, LANES), 0) // hd
    c = lax.broadcasted_iota(jnp.int32, (LANES, LANES), 1) // hd
    ones = (r == c).astype(BF16)
    outs = []
    for ch in range(x.shape[1] // LANES):
        hi, lo = _split2(x[:, ch * LANES:(ch + 1) * LANES])
        outs.append(_dotf(hi, ones) + _dotf(lo, ones))
    return outs[0] if len(outs) == 1 else jnp.concatenate(outs, axis=1)


def _suffix_matrix(n):
    r = lax.broadcasted_iota(jnp.int32, (n, n), 0)
    c = lax.broadcasted_iota(jnp.int32, (n, n), 1)
    return (r >= c).astype(BF16)


def _walk_up_staged(i, per, stages):
    assert per % 2 == 0
    n_full = i * per

    def pair(first, masked):
        states = [{"j": first}, {"j": first + 1}]
        for stage in stages:
            for st in states:
                stage(st, masked)

    def full_trip(t, c):
        pair(2 * t, False)
        return c

    def masked_trip(d, c):
        pair(n_full + 2 * d, True)
        return c

    lax.fori_loop(0, n_full // 2, full_trip, 0)
    lax.fori_loop(0, per // 2, masked_trip, 0)


def fox_fwd(qs, kn, v, nc, bq, bk):
    nh, s, _ = qs.shape
    nq = s // bq
    per = bq // bk

    def body(q_ref, k_ref, v_ref, nc_ref, o_ref, lse_ref):
        i = pl.program_id(1)
        q = q_ref[...]

        def logits(j, masked):
            off = pl.multiple_of(j * bk, bk)
            sc = _dot_nt(q, k_ref[pl.ds(off, bk), :]) + nc_ref[:, pl.ds(off, bk)]
            if masked:
                row = i * bq + lax.broadcasted_iota(jnp.int32, (bq, bk), 0)
                col = off + lax.broadcasted_iota(jnp.int32, (bq, bk), 1)
                sc = jnp.where(col <= row, sc, -jnp.inf)
            return off, sc

        def update(carry, off, sc):
            m, l, acc = carry
            m_new = jnp.maximum(m, jnp.max(sc, axis=1, keepdims=True))
            alpha = jnp.exp2(m - m_new)
            p = jnp.exp2(sc - m_new)
            l = alpha * l + jnp.sum(p, axis=1, keepdims=True)
            acc = alpha * acc + _dotf(p.astype(BF16), v_ref[pl.ds(off, bk), :])
            return m_new, l, acc

        def pair(first, carry, masked):
            a, b = logits(first, masked), logits(first + 1, masked)
            return update(update(carry, *a), *b)

        assert per % 2 == 0
        n_full = i * per
        init = (jnp.full((bq, 1), -jnp.inf, F32), jnp.zeros((bq, 1), F32), jnp.zeros((bq, HD), F32))
        carry = lax.fori_loop(0, n_full // 2, lambda t, c: pair(2 * t, c, False), init)
        m, l, acc = lax.fori_loop(0, per // 2, lambda t, c: pair(n_full + 2 * t, c, True), carry)
        o_ref[...] = (acc / l).astype(o_ref.dtype)
        lse_ref[...] = m + jnp.log2(l)

    return pl.pallas_call(
        body, name="fox_fwd",
        grid=(nh, nq),
        in_specs=[pl.BlockSpec((None, bq, HD), lambda h, i: (h, i, 0)),
                  pl.BlockSpec((None, s, HD), lambda h, i: (h, 0, 0)),
                  pl.BlockSpec((None, s, HD), lambda h, i: (h, 0, 0)),
                  pl.BlockSpec((None, 1, s), lambda h, i: (h, 0, 0))],
        out_specs=[pl.BlockSpec((None, bq, HD), lambda h, i: (h, i, 0)),
                   pl.BlockSpec((None, bq, 1), lambda h, i: (h, i, 0))],
        out_shape=[jax.ShapeDtypeStruct((nh, s, HD), BF16), jax.ShapeDtypeStruct((nh, s, 1), F32)],
        compiler_params=_cparams(("arbitrary", "arbitrary"), 40),
    )(qs, kn, v, nc)


LOG2E = 1.4426950408889634
LN2 = 0.6931471805599453


def _neg_abs(z):
    sign = jnp.uint32(0x80000000)
    return lax.bitcast_convert_type(lax.bitcast_convert_type(z, jnp.uint32) | sign, F32)


def _sb_softplus2(z, row0, col0, masked):
    u = jnp.maximum(z, 0.0) + jnp.log2(1.0 + jnp.exp2(_neg_abs(z)))
    strict = None
    if masked:
        row = row0 + lax.broadcasted_iota(jnp.int32, z.shape, 0)
        col = col0 + lax.broadcasted_iota(jnp.int32, z.shape, 1)
        strict = col < row
        u = jnp.where(strict, u, 0.0)
    return u, strict


def _walk_down_staged(i, per, stages):
    assert per % 2 == 0
    n_full = i * per

    def pair(top, masked):
        states = [{"j": top}, {"j": top - 1}]
        for stage in stages:
            for st in states:
                stage(st, masked)

    def masked_trip(d, c):
        pair(n_full + per - 1 - 2 * d, True)
        return c

    def full_trip(t, c):
        pair(n_full - 1 - 2 * t, False)
        return c

    lax.fori_loop(0, per // 2, masked_trip, 0)
    lax.fori_loop(0, n_full // 2, full_trip, 0)


def _suffix2(x, m2):
    hi, lo = _split2(x)
    return _dotf(jnp.concatenate([hi, lo], axis=1), m2)


def _lanes(col, n):
    return jnp.broadcast_to(col, (col.shape[0], n))


def sb_fwd(qs, k, v, bq, bk):
    nh, s, _ = qs.shape
    nq = s // bq
    per = bq // bk

    def body(q_ref, k_ref, v_ref, o_ref, lrun_ref):
        i = pl.program_id(1)
        q = q_ref[...]
        tri = _suffix_matrix(bk)
        o_ref[...] = jnp.zeros_like(o_ref)
        lrun_ref[...] = jnp.zeros_like(lrun_ref)

        def logits(st, masked):
            st["off"] = pl.multiple_of(st["j"] * bk, bk)
            st["z"] = _dot_nt(q, k_ref[pl.ds(st["off"], bk), :])

        def suffix(st, masked):
            u, st["strict"] = _sb_softplus2(st["z"], i * bq, st["off"], masked)
            st["incl"] = _dotf(u.astype(BF16), tri)

        def weigh(st, masked):
            lrun = lrun_ref[...]
            w = jnp.exp2(st["z"] - st["incl"] + jnp.tile(lrun, (1, bk // LANES)))
            if masked:
                w = jnp.where(st["strict"], w, 0.0)
            o_ref[...] += _dotf(w.astype(BF16), v_ref[pl.ds(st["off"], bk), :])
            lrun_ref[...] = lrun - _lanes(st["incl"][:, 0:1], LANES)

        _walk_down_staged(i, per, [logits, suffix, weigh])

    return pl.pallas_call(
        body, name="sb_fwd",
        grid=(nh, nq),
        in_specs=[pl.BlockSpec((None, bq, HD), lambda h, i: (h, i, 0)),
                  pl.BlockSpec((None, s, HD), lambda h, i: (h, 0, 0)),
                  pl.BlockSpec((None, s, HD), lambda h, i: (h, 0, 0))],
        out_specs=pl.BlockSpec((None, bq, HD), lambda h, i: (h, i, 0)),
        out_shape=jax.ShapeDtypeStruct((nh, s, HD), F32),
        scratch_shapes=[pltpu.VMEM((bq, LANES), F32)],
        compiler_params=_cparams(("arbitrary", "arbitrary"), 40),
    )(qs, k, v)


def _attn_bwd_call(name, body, s, bq, ins, in_specs, extra_out_specs, extra_out_shapes, extra_scratch=()):
    nh = NH
    nq = s // bq
    return pl.pallas_call(
        body, name=name,
        grid=(nh, nq),
        in_specs=in_specs,
        out_specs=[pl.BlockSpec((None, bq, HD), lambda h, i: (h, i, 0)),
                   pl.BlockSpec(memory_space=pl.ANY), pl.BlockSpec(memory_space=pl.ANY)] + extra_out_specs,
        out_shape=[jax.ShapeDtypeStruct((nh, s, HD), F32), jax.ShapeDtypeStruct((nh, s, HD), F32),
                   jax.ShapeDtypeStruct((nh, s, HD), F32)] + extra_out_shapes,
        scratch_shapes=[pltpu.VMEM((s, HD), F32), pltpu.VMEM((s, HD), F32), pltpu.SemaphoreType.DMA((2,))] + list(extra_scratch),
        compiler_params=_cparams(("arbitrary", "arbitrary"), 52),
    )(*ins)


def _flush_dkv(i, nq, h, dk_acc, dv_acc, dk_hbm, dv_hbm, sems):
    @pl.when(i == nq - 1)
    def _():
        ck = pltpu.make_async_copy(dk_acc, dk_hbm.at[h], sems.at[0])
        cv = pltpu.make_async_copy(dv_acc, dv_hbm.at[h], sems.at[1])
        ck.start()
        cv.start()
        ck.wait()
        cv.wait()


def fox_bwd(qs, kn, v, nc, lse, do, delta, bq, bk):
    nh, s, _ = qs.shape
    nq = s // bq
    per = bq // bk

    def body(q_ref, k_ref, v_ref, nc_ref, lse_ref, do_ref, dl_ref, dq_ref, dk_hbm, dv_hbm, dnc_ref, drow_ref, dk_acc, dv_acc, sems):
        h, i = pl.program_id(0), pl.program_id(1)

        @pl.when(i == 0)
        def _():
            dk_acc[...] = jnp.zeros_like(dk_acc)
            dv_acc[...] = jnp.zeros_like(dv_acc)
            dnc_ref[...] = jnp.zeros_like(dnc_ref)

        q = q_ref[...]
        do_t = do_ref[...]
        lse_t = lse_ref[...]
        dl_t = dl_ref[...]

        dq_ref[...] = jnp.zeros_like(dq_ref)
        drow_ref[...] = jnp.zeros_like(drow_ref)

        def logits(st, masked):
            st["off"] = pl.multiple_of(st["j"] * bk, bk)
            st["sc"] = _dot_nt(q, k_ref[pl.ds(st["off"], bk), :]) + nc_ref[:, pl.ds(st["off"], bk)]
            st["dp"] = _dot_nt(do_t, v_ref[pl.ds(st["off"], bk), :])

        def grads(st, masked):
            off = st["off"]
            p = jnp.exp2(st["sc"] - lse_t)
            if masked:
                row = i * bq + lax.broadcasted_iota(jnp.int32, (bq, bk), 0)
                col = off + lax.broadcasted_iota(jnp.int32, (bq, bk), 1)
                p = jnp.where(col <= row, p, 0.0)
            ds = p * (st["dp"] - dl_t)
            dsb = ds.astype(BF16)
            dq_ref[...] += _dotf(dsb, k_ref[pl.ds(off, bk), :])
            dk_acc[pl.ds(off, bk), :] += _dot_tn(dsb, q)
            dv_acc[pl.ds(off, bk), :] += _dot_tn(p.astype(BF16), do_t)
            dnc_ref[:, pl.ds(off, bk)] += jnp.sum(ds, axis=0, keepdims=True)
            drow_ref[...] += jnp.sum(ds, axis=1, keepdims=True)

        _walk_up_staged(i, per, [logits, grads])
        _flush_dkv(i, nq, h, dk_acc, dv_acc, dk_hbm, dv_hbm, sems)

    tile_spec = pl.BlockSpec((None, bq, HD), lambda h, i: (h, i, 0))
    col_spec = pl.BlockSpec((None, bq, 1), lambda h, i: (h, i, 0))
    full_spec = pl.BlockSpec((None, s, HD), lambda h, i: (h, 0, 0))
    row_spec = pl.BlockSpec((None, 1, s), lambda h, i: (h, 0, 0))
    return _attn_bwd_call("fox_bwd", body, s, bq, (qs, kn, v, nc, lse, do, delta),
                          [tile_spec, full_spec, full_spec, row_spec, col_spec, tile_spec, col_spec],
                          [row_spec, col_spec],
                          [jax.ShapeDtypeStruct((nh, 1, s), F32), jax.ShapeDtypeStruct((nh, s, 1), F32)])


def sb_bwd(qs, k, v, do, delta, bq, bk):
    nh, s, _ = qs.shape
    nq = s // bq
    per = bq // bk

    def body(q_ref, k_ref, v_ref, do_ref, dl_ref, dq_ref, dk_hbm, dv_hbm, dk_acc, dv_acc, sems, lrun_ref, crun_ref):
        h, i = pl.program_id(0), pl.program_id(1)

        @pl.when(i == 0)
        def _():
            dk_acc[...] = jnp.zeros_like(dk_acc)
            dv_acc[...] = jnp.zeros_like(dv_acc)

        q = q_ref[...]
        do_t = do_ref[...]
        tri = _suffix_matrix(bk)
        tri2 = jnp.concatenate([tri, tri], axis=0)
        dq_ref[...] = jnp.zeros_like(dq_ref)
        lrun_ref[...] = jnp.zeros_like(lrun_ref)
        crun_ref[...] = _lanes(dl_ref[...], LANES)

        def logits(st, masked):
            st["off"] = pl.multiple_of(st["j"] * bk, bk)
            st["z"] = _dot_nt(q, k_ref[pl.ds(st["off"], bk), :])
            st["dw"] = _dot_nt(do_t, v_ref[pl.ds(st["off"], bk), :])

        def suffix(st, masked):
            st["u"], st["strict"] = _sb_softplus2(st["z"], i * bq, st["off"], masked)
            st["incl"] = _dotf(st["u"].astype(BF16), tri)

        def weigh(st, masked):
            lrun = lrun_ref[...]
            w = jnp.exp2(st["z"] - st["incl"] + jnp.tile(lrun, (1, bk // LANES)))
            if masked:
                w = jnp.where(st["strict"], w, 0.0)
            st["wb"] = w.astype(BF16)
            st["e"] = st["dw"] * st["wb"].astype(F32)
            st["einc"] = _suffix2(st["e"], tri2)
            lrun_ref[...] = lrun - _lanes(st["incl"][:, 0:1], LANES)

        def grads(st, masked):
            crun = crun_ref[...]
            prefix = jnp.tile(crun, (1, bk // LANES)) - st["einc"]
            dz = st["e"] - jnp.exp2(st["z"] - st["u"]) * (st["e"] + prefix)
            if masked:
                dz = jnp.where(st["strict"], dz, 0.0)
            dzb = dz.astype(BF16)
            dq_ref[...] += _dotf(dzb, k_ref[pl.ds(st["off"], bk), :])
            dk_acc[pl.ds(st["off"], bk), :] += _dot_tn(dzb, q)
            dv_acc[pl.ds(st["off"], bk), :] += _dot_tn(st["wb"], do_t)
            crun_ref[...] = crun - _lanes(st["einc"][:, 0:1], LANES)

        _walk_down_staged(i, per, [logits, suffix, weigh, grads])
        _flush_dkv(i, nq, h, dk_acc, dv_acc, dk_hbm, dv_hbm, sems)

    tile_spec = pl.BlockSpec((None, bq, HD), lambda h, i: (h, i, 0))
    col_spec = pl.BlockSpec((None, bq, 1), lambda h, i: (h, i, 0))
    full_spec = pl.BlockSpec((None, s, HD), lambda h, i: (h, 0, 0))
    return _attn_bwd_call("sb_bwd", body, s, bq, (qs, k, v, do, delta),
                          [tile_spec, full_spec, full_spec, tile_spec, col_spec], [], [],
                          [pltpu.VMEM((bq, LANES), F32), pltpu.VMEM((bq, LANES), F32)])


BI = SPAN
PAIR = 2 * HD2


def _slopes(g):
    return [float(2.0 ** (-8.0 * (g * NH2 + h + 1) / (NG * NH2))) for h in range(NH2)]


def _head_lanes(hh):
    return (lax.broadcasted_iota(jnp.int32, (1, PAIR), 1) // HD2) == hh


def _dil_masks(n, nblk, d):
    a = lax.broadcasted_iota(jnp.int32, (BI, 2 * BI), 0)
    c = lax.broadcasted_iota(jnp.int32, (BI, 2 * BI), 1)
    dist = a - c + BI
    valid = (dist >= 0) & (dist <= SPAN) & ((c >= BI) | (n > 0))
    a2 = lax.broadcasted_iota(jnp.int32, (2 * BI, BI), 0)
    c2 = lax.broadcasted_iota(jnp.int32, (2 * BI, BI), 1)
    dist2 = a2 - c2
    valid2 = (dist2 >= 0) & (dist2 <= SPAN) & ((a2 < BI) | (n < nblk - 1))
    return valid, (dist * d).astype(F32), valid2, (dist2 * d).astype(F32)


def _dil_pair_fwd(qp, kcat, vcat, valid, distf, slopes2):
    o_pair = jnp.zeros((BI, PAIR), F32)
    lse_pair = jnp.zeros((BI, PAIR), F32)
    for hh in range(2):
        lm = _head_lanes(hh)
        qm = jnp.where(lm, qp, jnp.zeros_like(qp))
        logits = jnp.where(valid, _dot_nt(qm, kcat) - slopes2[hh] * distf, -jnp.inf)
        m = jnp.max(logits, axis=1, keepdims=True)
        p = jnp.exp(logits - m)
        den = jnp.sum(p, axis=1, keepdims=True)
        o_pair = jnp.where(lm, _dotf(p.astype(BF16), vcat) / den, o_pair)
        lse_pair = jnp.where(lm, m + jnp.log(den), lse_pair)
    return o_pair, lse_pair


def _dil_pair_bwd(qc, qn, kc, kp, vc, vp, doc, don, lse_c, lse_n, dl_c, dl_n, masks, slopes2):
    valid, distf, valid2, dist2f = masks
    qcat = jnp.concatenate([qc, qn], axis=0)
    docat = jnp.concatenate([doc, don], axis=0)
    kcat = jnp.concatenate([kp, kc], axis=0)
    vcat = jnp.concatenate([vp, vc], axis=0)
    dq_pair = jnp.zeros((BI, PAIR), F32)
    dk_pair = jnp.zeros((BI, PAIR), F32)
    dv_pair = jnp.zeros((BI, PAIR), F32)
    for hh in range(2):
        col = slice(hh * HD2, hh * HD2 + 1)
        lm = _head_lanes(hh)
        zq = jnp.zeros_like(qc)
        logits = _dot_nt(jnp.where(lm, qc, zq), kcat) - slopes2[hh] * distf
        p = jnp.exp(jnp.where(valid, logits, -jnp.inf) - lse_c[:, col])
        dp = _dot_nt(jnp.where(lm, doc, zq), vcat)
        ds = (p * (dp - dl_c[:, col])).astype(BF16)
        dq_pair = jnp.where(lm, _dotf(ds, kcat), dq_pair)

        zcat = jnp.zeros_like(qcat)
        qm = jnp.where(lm, qcat, zcat)
        dom = jnp.where(lm, docat, zcat)
        lse2 = jnp.concatenate([lse_c[:, col], lse_n[:, col]], axis=0)
        dl2 = jnp.concatenate([dl_c[:, col], dl_n[:, col]], axis=0)
        logits2 = _dot_nt(qm, kc) - slopes2[hh] * dist2f
        p2 = jnp.exp(jnp.where(valid2, logits2, -jnp.inf) - lse2)
        ds2 = (p2 * (_dot_nt(dom, vc) - dl2)).astype(BF16)
        dk_pair = dk_pair + _dot_tn(ds2, qm)
        dv_pair = dv_pair + _dot_tn(p2.astype(BF16), dom)
    return dq_pair, dk_pair, dv_pair


def _pair_slopes(slopes, hp):
    out = []
    for hh in range(2):
        acc = jnp.float32(slopes[hh])
        for t in range(1, NH2 // 2):
            acc = jnp.where(hp == t, jnp.float32(slopes[2 * t + hh]), acc)
        out.append(acc)
    return out


def _for_residues(d, residue):
    per_trip = min(d, 4)

    def trip(t, carry):
        for u in range(per_trip):
            residue(t * per_trip + u)
        return carry

    if d == per_trip:
        trip(0, 0)
    else:
        lax.fori_loop(0, d // per_trip, trip, 0)


def _dil_tiling(d):
    return NH2 // 2 if d == 1 else 1


def dil_fwd(q, k, v, g):
    d = DILATIONS[g]
    s = q.shape[1]
    rows_per = BI * d
    nblk = s // rows_per
    tiles = _dil_tiling(d)
    slopes = _slopes(g)

    def body(q_ref, kc_ref, kp_ref, vc_ref, vp_ref, o_ref, lse_ref):
        n = pl.program_id(0)
        valid, distf, _, _ = _dil_masks(n, nblk, d)
        for t in range(tiles):
            sl = slice(t * PAIR, (t + 1) * PAIR)
            slopes2 = _pair_slopes(slopes, pl.program_id(1) * tiles + t)

            def residue(r):
                rows = pl.ds(r, BI, stride=d)
                kcat = jnp.concatenate([kp_ref[rows, sl], kc_ref[rows, sl]], axis=0).astype(BF16)
                vcat = jnp.concatenate([vp_ref[rows, sl], vc_ref[rows, sl]], axis=0).astype(BF16)
                o_ref[rows, sl], lse_ref[rows, sl] = _dil_pair_fwd(q_ref[rows, sl].astype(BF16), kcat, vcat, valid, distf, slopes2)

            _for_residues(d, residue)

    width = tiles * PAIR
    cur = pl.BlockSpec((None, rows_per, width), lambda n, hp: (g, n, hp))
    prev = pl.BlockSpec((None, rows_per, width), lambda n, hp: (g, jnp.maximum(n - 1, 0), hp))
    out = pl.BlockSpec((rows_per, width), lambda n, hp: (n, hp))
    return pl.pallas_call(
        body, name=f"dil_fwd_{g}",
        grid=(nblk, ODD_W // width),
        in_specs=[cur, cur, prev, cur, prev],
        out_specs=[out, out],
        out_shape=[jax.ShapeDtypeStruct((s, ODD_W), F32), jax.ShapeDtypeStruct((s, ODD_W), F32)],
        compiler_params=_cparams(("parallel", "parallel"), 40),
    )(q, k, k, v, v)


def dil_bwd(q, k, v, do, lse, delta, g):
    d = DILATIONS[g]
    s = q.shape[1]
    rows_per = BI * d
    nblk = s // rows_per
    tiles = _dil_tiling(d)
    slopes = _slopes(g)

    def body(qc_ref, qn_ref, kc_ref, kp_ref, vc_ref, vp_ref, doc_ref, don_ref, lc_ref, ln_ref, dc_ref, dn_ref,
             dq_ref, dk_ref, dv_ref):
        n = pl.program_id(0)
        masks = _dil_masks(n, nblk, d)
        for t in range(tiles):
            sl = slice(t * PAIR, (t + 1) * PAIR)
            slopes2 = _pair_slopes(slopes, pl.program_id(1) * tiles + t)

            def residue(r):
                rows = pl.ds(r, BI, stride=d)
                b = lambda ref: ref[rows, sl].astype(BF16)
                f = lambda ref: ref[rows, sl]
                dq_ref[rows, sl], dk_ref[rows, sl], dv_ref[rows, sl] = _dil_pair_bwd(
                    b(qc_ref), b(qn_ref), b(kc_ref), b(kp_ref), b(vc_ref), b(vp_ref), b(doc_ref), b(don_ref),
                    f(lc_ref), f(ln_ref), f(dc_ref), f(dn_ref), masks, slopes2)

            _for_residues(d, residue)

    width = tiles * PAIR
    nxt_idx = lambda n: jnp.minimum(n + 1, nblk - 1)
    prv_idx = lambda n: jnp.maximum(n - 1, 0)
    cur3 = pl.BlockSpec((None, rows_per, width), lambda n, hp: (g, n, hp))
    nxt3 = pl.BlockSpec((None, rows_per, width), lambda n, hp: (g, nxt_idx(n), hp))
    prv3 = pl.BlockSpec((None, rows_per, width), lambda n, hp: (g, prv_idx(n), hp))
    cur2 = pl.BlockSpec((rows_per, width), lambda n, hp: (n, hp))
    nxt2 = pl.BlockSpec((rows_per, width), lambda n, hp: (nxt_idx(n), hp))
    shape = jax.ShapeDtypeStruct((s, ODD_W), F32)
    return pl.pallas_call(
        body, name=f"dil_bwd_{g}",
        grid=(nblk, ODD_W // width),
        in_specs=[cur3, nxt3, cur3, prv3, cur3, prv3, cur2, nxt2, cur2, nxt2, cur2, nxt2],
        out_specs=[cur2, cur2, cur2],
        out_shape=[shape, shape, shape],
        compiler_params=_cparams(("parallel", "parallel"), 48),
    )(q, q, k, k, v, v, do, do, lse, lse, delta, delta)


TM = 256


def _rows(tm, w):
    return pl.BlockSpec((tm, w), lambda i: (i, 0))


def _whole(shape):
    return pl.BlockSpec(shape, lambda i: (0,) * len(shape))


def _heads(tm):
    return pl.BlockSpec((NH, tm, HD), lambda i: (0, i, 0))


def _groups(tm):
    return pl.BlockSpec((NG, tm, ODD_W), lambda i: (0, i, 0))


def _rms(x):
    return lax.rsqrt(jnp.mean(x * x, axis=1, keepdims=True) + RMS_EPS)


def _seg_rms(q, hd):
    return lax.rsqrt(_segsum(q * q, hd) * (1.0 / hd) + RMS_EPS)


def _seg_rms_bwd(q_raw, dqs, gain, scale, hd):
    q = q_raw.astype(F32)
    r = _seg_rms(q, hd)
    qhat = q * r
    u = dqs * (gain * scale)
    dq = r * (u - qhat * (_segsum(u * qhat, hd) * (1.0 / hd)))
    return dq, jnp.sum(dqs * qhat, axis=0, keepdims=True) * scale


def _rms_bwd(x, dh, gain):
    r = _rms(x)
    xhat = x * r
    u = dh * gain
    dx = r * (u - xhat * jnp.mean(u * xhat, axis=1, keepdims=True))
    return dx, jnp.sum(dh * xhat, axis=0, keepdims=True)


def even_in_fwd(x, gnorm, w_pack, bf_pad, gq, gk):
    s = x.shape[0]

    def body(x_ref, g_ref, w_ref, bf_ref, gq_ref, gk_ref,
             h_ref, fqs_ref, fkn_ref, fv_ref, fqr_ref, fkr_ref, flog_ref, sqs_ref, sk_ref, sv_ref, gate_ref):
        xt = x_ref[...]
        h = (xt * _rms(xt) * g_ref[...]).astype(BF16)
        h_ref[...] = h
        proj = _dotf(h, w_ref[...])
        fq = proj[:, 0:512]
        fk = proj[:, 512:1024]
        fqs = fq * _seg_rms(fq, HD) * (gq_ref[...] * (SCALE_E * LOG2E))
        fkn = fk * _seg_rms(fk, HD) * gk_ref[...]
        flog_ref[...] = proj[:, FL_OFF:FL_OFF + LANES] + bf_ref[...]
        o = FL_OFF + LANES
        for hh in range(NH):
            sl = slice(hh * HD, (hh + 1) * HD)
            fqs_ref[hh] = fqs[:, sl].astype(BF16)
            fkn_ref[hh] = fkn[:, sl].astype(BF16)
            fqr_ref[hh] = fq[:, sl].astype(BF16)
            fkr_ref[hh] = fk[:, sl].astype(BF16)
            fv_ref[hh] = proj[:, 1024 + hh * HD:1024 + (hh + 1) * HD].astype(BF16)
            sqs_ref[hh] = (proj[:, o + hh * HD:o + (hh + 1) * HD] * (SCALE_E * LOG2E)).astype(BF16)
            sk_ref[hh] = proj[:, o + 512 + hh * HD:o + 512 + (hh + 1) * HD].astype(BF16)
            sv_ref[hh] = proj[:, o + 1024 + hh * HD:o + 1024 + (hh + 1) * HD].astype(BF16)
        gate_ref[...] = proj[:, o + 1536:o + 2560].astype(BF16)

    hs = jax.ShapeDtypeStruct((NH, s, HD), BF16)
    return pl.pallas_call(
        body, name="even_in_fwd",
        grid=(s // TM,),
        in_specs=[_rows(TM, D_MODEL), _whole((1, D_MODEL)), _whole((D_MODEL, EVEN_PACK)), _whole((1, LANES)),
                  _whole((1, 512)), _whole((1, 512))],
        out_specs=[_rows(TM, D_MODEL)] + [_heads(TM)] * 5 + [_rows(TM, LANES)] + [_heads(TM)] * 3 + [_rows(TM, EVEN_W)],
        out_shape=[jax.ShapeDtypeStruct((s, D_MODEL), BF16)] + [hs] * 5 + [jax.ShapeDtypeStruct((s, LANES), F32)]
        + [hs] * 3 + [jax.ShapeDtypeStruct((s, EVEN_W), BF16)],
        compiler_params=_cparams(("parallel",), 52),
    )(x, gnorm, w_pack, bf_pad, gq, gk)


def _prefix_matrices(r):
    a = lax.broadcasted_iota(jnp.int32, (LANES, LANES), 0)
    b = lax.broadcasted_iota(jnp.int32, (LANES, LANES), 1)
    ra = lax.broadcasted_iota(jnp.int32, (r, r), 0)
    rb = lax.broadcasted_iota(jnp.int32, (r, r), 1)
    return a, b, ra, rb


def _dot3_right(x, m):
    a, b, c = _split3(x)
    return _dotf(a, m) + _dotf(b, m) + _dotf(c, m)


def _dot3_left(m, x):
    a, b, c = _split3(x)
    return _dotf(m, a) + _dotf(m, b) + _dotf(m, c)


def fox_cum(flog4):
    nh, r, _ = flog4.shape

    def body(f_ref, nc_ref):
        z = f_ref[...]
        lf = jnp.minimum(z, 0.0) - jnp.log(1.0 + jnp.exp(-jnp.abs(z)))
        a, b, ra, rb = _prefix_matrices(r)
        within = _dot3_right(lf, (a <= b).astype(BF16))
        tot = jnp.broadcast_to(within[:, LANES - 1:LANES], (r, LANES))
        nc_ref[...] = (within + _dot3_left((rb < ra).astype(BF16), tot)) * (-LOG2E)

    return pl.pallas_call(
        body, name="fox_cum", grid=(nh,),
        in_specs=[pl.BlockSpec((None, r, LANES), lambda h: (h, 0, 0))],
        out_specs=pl.BlockSpec((None, r, LANES), lambda h: (h, 0, 0)),
        out_shape=jax.ShapeDtypeStruct((nh, r, LANES), F32),
        compiler_params=_cparams(("parallel",), 16),
    )(flog4)


def fox_cum_bwd(dcum4, flog4):
    nh, r, _ = flog4.shape

    def body(d_ref, f_ref, o_ref):
        a, b, ra, rb = _prefix_matrices(r)
        dc = d_ref[...]
        within = _dot3_right(dc, (a >= b).astype(BF16))
        tot = jnp.broadcast_to(within[:, 0:1], (r, LANES))
        dlf = within + _dot3_left((rb > ra).astype(BF16), tot)
        o_ref[...] = dlf / (1.0 + jnp.exp(f_ref[...]))

    spec = pl.BlockSpec((None, r, LANES), lambda h: (h, 0, 0))
    return pl.pallas_call(
        body, name="fox_cum_bwd", grid=(nh,),
        in_specs=[spec, spec], out_specs=spec,
        out_shape=jax.ShapeDtypeStruct((nh, r, LANES), F32),
        compiler_params=_cparams(("parallel",), 16),
    )(dcum4, flog4)


def even_out_fwd(fo, so, gate, x, w_out):
    s = x.shape[0]
    tm = 2 * TM

    def body(fo_ref, so_ref, g_ref, x_ref, w_ref, y_ref):
        sg = _silu(g_ref[...].astype(F32))
        acc = x_ref[...]
        for hh in range(NH):
            mf = (fo_ref[hh].astype(F32) * sg[:, hh * HD:(hh + 1) * HD]).astype(BF16)
            ms = (so_ref[hh] * sg[:, 512 + hh * HD:512 + (hh + 1) * HD]).astype(BF16)
            acc = acc + _dotf(mf, w_ref[hh * HD:(hh + 1) * HD, :]) + _dotf(ms, w_ref[512 + hh * HD:512 + (hh + 1) * HD, :])
        y_ref[...] = acc

    return pl.pallas_call(
        body, name="even_out_fwd", grid=(s // tm,),
        in_specs=[_heads(tm), _heads(tm), _rows(tm, EVEN_W), _rows(tm, D_MODEL), _whole((EVEN_W, D_MODEL))],
        out_specs=_rows(tm, D_MODEL),
        out_shape=jax.ShapeDtypeStruct((s, D_MODEL), F32),
        compiler_params=_cparams(("parallel",), 40),
    )(fo, so, gate, x, w_out)


def odd_in_fwd(y1, gnorm, w2, gq, gk):
    s = y1.shape[0]

    def body(x_ref, g_ref, w_ref, gq_ref, gk_ref, h_ref, qs_ref, kn_ref, v_ref, qr_ref, kr_ref, gate_ref):
        xt = x_ref[...]
        h = (xt * _rms(xt) * g_ref[...]).astype(BF16)
        h_ref[...] = h
        proj = _dotf(h, w_ref[...])
        for g in range(NG):
            q = proj[:, g * ODD_W:(g + 1) * ODD_W]
            k = proj[:, 1536 + g * ODD_W:1536 + (g + 1) * ODD_W]
            qs_ref[g] = q * _seg_rms(q, HD2) * (gq_ref[...] * SCALE_O)
            kn_ref[g] = k * _seg_rms(k, HD2) * gk_ref[...]
            qr_ref[g] = q.astype(BF16)
            kr_ref[g] = k.astype(BF16)
            v_ref[g] = proj[:, 3072 + g * ODD_W:3072 + (g + 1) * ODD_W]
        gate_ref[...] = proj[:, 4608:5120].astype(BF16)

    gs = lambda dt: jax.ShapeDtypeStruct((NG, s, ODD_W), dt)
    return pl.pallas_call(
        body, name="odd_in_fwd", grid=(s // TM,),
        in_specs=[_rows(TM, D_MODEL), _whole((1, D_MODEL)), _whole((D_MODEL, ODD_IN)), _whole((1, ODD_W)), _whole((1, ODD_W))],
        out_specs=[_rows(TM, D_MODEL)] + [_groups(TM)] * 5 + [_rows(TM, ODD_W)],
        out_shape=[jax.ShapeDtypeStruct((s, D_MODEL), BF16), gs(F32), gs(F32), gs(F32), gs(BF16), gs(BF16),
                   jax.ShapeDtypeStruct((s, ODD_W), BF16)],
        compiler_params=_cparams(("parallel",), 52),
    )(y1, gnorm, w2, gq, gk)


def odd_out_fwd(o0, l0, o1, l1, o2, l2, gate2, y1, target, w_out2):
    s = y1.shape[0]
    tm = 2 * TM
    nt = s // tm

    def body(o0_ref, l0_ref, o1_ref, l1_ref, o2_ref, l2_ref, g_ref, y1_ref, t_ref, w_ref,
             att_ref, lse_ref, dy_ref, loss_ref):
        l0t, l1t, l2t = l0_ref[...], l1_ref[...], l2_ref[...]
        m = jnp.maximum(jnp.maximum(l0t, l1t), l2t)
        e0, e1, e2 = jnp.exp(l0t - m), jnp.exp(l1t - m), jnp.exp(l2t - m)
        den = e0 + e1 + e2
        att = (e0 * o0_ref[...] + e1 * o1_ref[...] + e2 * o2_ref[...]) / den
        att_ref[...] = att.astype(BF16)
        lse_ref[...] = m + jnp.log(den)
        mixed = (att * _silu(g_ref[...].astype(F32))).astype(BF16)
        diff = y1_ref[...] + _dotf(mixed, w_ref[...]) - t_ref[...]
        dy_ref[...] = diff * (1.0 / D_MODEL)
        loss_ref[...] = jnp.full((1, 1, LANES), 0.5 / D_MODEL, F32) * jnp.sum(diff * diff)

    big = jax.ShapeDtypeStruct((s, ODD_W), F32)
    return pl.pallas_call(
        body, name="odd_out_fwd", grid=(nt,),
        in_specs=[_rows(tm, ODD_W)] * 7 + [_rows(tm, D_MODEL), _rows(tm, D_MODEL), _whole((ODD_W, D_MODEL))],
        out_specs=[_rows(tm, ODD_W), _rows(tm, ODD_W), _rows(tm, D_MODEL), pl.BlockSpec((1, 1, LANES), lambda i: (i, 0, 0))],
        out_shape=[jax.ShapeDtypeStruct((s, ODD_W), BF16), big, jax.ShapeDtypeStruct((s, D_MODEL), F32),
                   jax.ShapeDtypeStruct((nt, 1, LANES), F32)],
        compiler_params=_cparams(("parallel",), 40),
    )(o0, l0, o1, l1, o2, l2, gate2, y1, target, w_out2)


def odd_out_bwd(dy2, w_out2_t, att, gate2):
    s = dy2.shape[0]
    tm = 2 * TM

    def body(dy_ref, wt_ref, att_ref, g_ref, datt_ref, dgate_ref, delta_ref, dw_ref):
        @pl.when(pl.program_id(0) == 0)
        def _():
            dw_ref[...] = jnp.zeros_like(dw_ref)

        dyb = dy_ref[...].astype(BF16)
        dmixed = _dotf(dyb, wt_ref[...])
        g = g_ref[...].astype(F32)
        att_t = att_ref[...].astype(F32)
        sg = _silu(g)
        datt = (dmixed * sg).astype(BF16).astype(F32)
        datt_ref[...] = datt
        dgate_ref[...] = (dmixed * att_t * _dsilu(g)).astype(BF16)
        delta_ref[...] = _segsum(datt * att_t, HD2)
        dw_ref[...] += _dot_tn((att_t * sg).astype(BF16), dyb)

    return pl.pallas_call(
        body, name="odd_out_bwd", grid=(s // tm,),
        in_specs=[_rows(tm, D_MODEL), _whole((D_MODEL, ODD_W)), _rows(tm, ODD_W), _rows(tm, ODD_W)],
        out_specs=[_rows(tm, ODD_W), _rows(tm, ODD_W), _rows(tm, ODD_W), _whole((ODD_W, D_MODEL))],
        out_shape=[jax.ShapeDtypeStruct((s, ODD_W), F32), jax.ShapeDtypeStruct((s, ODD_W), BF16),
                   jax.ShapeDtypeStruct((s, ODD_W), F32), jax.ShapeDtypeStruct((ODD_W, D_MODEL), F32)],
        compiler_params=_cparams(("arbitrary",), 40),
    )(dy2, w_out2_t, att, gate2)


def odd_in_bwd(dqs, dks, dvs, dgate2, q2r, k2r, gq, gk, w2_t, y1, dy2, gnorm):
    s = y1.shape[0]

    def body(dq0, dq1, dq2, dk0, dk1, dk2, dv0, dv1, dv2, dg_ref, qr_ref, kr_ref, gq_ref, gk_ref, wt_ref, y1_ref, dy_ref, gn_ref,
             dproj_ref, dy1_ref, dgn_ref, dgain_ref):
        @pl.when(pl.program_id(0) == 0)
        def _():
            dgn_ref[...] = jnp.zeros_like(dgn_ref)
            dgain_ref[...] = jnp.zeros_like(dgain_ref)

        for g, (dq_ref, dk_ref, dv_ref) in enumerate(((dq0, dk0, dv0), (dq1, dk1, dv1), (dq2, dk2, dv2))):
            dq, gq_row = _seg_rms_bwd(qr_ref[g], dq_ref[...], gq_ref[...], SCALE_O, HD2)
            dk, gk_row = _seg_rms_bwd(kr_ref[g], dk_ref[...], gk_ref[...], 1.0, HD2)
            dproj_ref[:, g * ODD_W:(g + 1) * ODD_W] = dq.astype(BF16)
            dproj_ref[:, 1536 + g * ODD_W:1536 + (g + 1) * ODD_W] = dk.astype(BF16)
            dproj_ref[:, 3072 + g * ODD_W:3072 + (g + 1) * ODD_W] = dv_ref[...].astype(BF16)
            dgain_ref[g:g + 1, :] += gq_row
            dgain_ref[NG + g:NG + g + 1, :] += gk_row
        dproj_ref[:, 4608:5120] = dg_ref[...]
        dh = _dotf(dproj_ref[...], wt_ref[...])
        dx, gn_row = _rms_bwd(y1_ref[...], dh, gn_ref[...])
        dy1_ref[...] = dy_ref[...] + dx
        dgn_ref[...] += gn_row

    f32r, bf16r = _rows(TM, ODD_W), _rows(TM, ODD_W)
    return pl.pallas_call(
        body, name="odd_in_bwd", grid=(s // TM,),
        in_specs=[f32r] * 6 + [bf16r] * 4 + [_groups(TM), _groups(TM), _whole((1, ODD_W)), _whole((1, ODD_W)),
                                             _whole((ODD_IN, D_MODEL)), _rows(TM, D_MODEL), _rows(TM, D_MODEL), _whole((1, D_MODEL))],
        out_specs=[_rows(TM, ODD_IN), _rows(TM, D_MODEL), _whole((1, D_MODEL)), _whole((8, ODD_W))],
        out_shape=[jax.ShapeDtypeStruct((s, ODD_IN), BF16), jax.ShapeDtypeStruct((s, D_MODEL), F32),
                   jax.ShapeDtypeStruct((1, D_MODEL), F32), jax.ShapeDtypeStruct((8, ODD_W), F32)],
        compiler_params=_cparams(("arbitrary",), 52),
    )(*dqs, *dks, *dvs, dgate2, q2r, k2r, gq, gk, w2_t, y1, dy2, gnorm)


def even_out_bwd(dy1, w_out_t, fo, so, gate):
    s = dy1.shape[0]
    tm = 2 * TM

    def body(dy_ref, wt_ref, fo_ref, so_ref, g_ref, dfo_ref, dso_ref, dgate_ref, delf_ref, dels_ref, dw_ref):
        @pl.when(pl.program_id(0) == 0)
        def _():
            dw_ref[...] = jnp.zeros_like(dw_ref)

        dyb = dy_ref[...].astype(BF16)
        dmixed = _dotf(dyb, wt_ref[...])
        g = g_ref[...].astype(F32)
        sg, dsg = _silu(g), _dsilu(g)
        for hh in range(NH):
            for base, o_ref, do_ref, del_ref in ((0, fo_ref, dfo_ref, delf_ref), (512, so_ref, dso_ref, dels_ref)):
                sl = slice(base + hh * HD, base + (hh + 1) * HD)
                o = o_ref[hh].astype(F32)
                do = (dmixed[:, sl] * sg[:, sl]).astype(BF16)
                do_ref[hh] = do
                del_ref[hh] = jnp.sum(do.astype(F32) * o, axis=1, keepdims=True)
                dgate_ref[:, sl] = (dmixed[:, sl] * o * dsg[:, sl]).astype(BF16)
                dw_ref[sl, :] += _dot_tn((o * sg[:, sl]).astype(BF16), dyb)

    cols = pl.BlockSpec((NH, tm, 1), lambda i: (0, i, 0))
    hs = jax.ShapeDtypeStruct((NH, s, HD), BF16)
    cs = jax.ShapeDtypeStruct((NH, s, 1), F32)
    return pl.pallas_call(
        body, name="even_out_bwd", grid=(s // tm,),
        in_specs=[_rows(tm, D_MODEL), _whole((D_MODEL, EVEN_W)), _heads(tm), _heads(tm), _rows(tm, EVEN_W)],
        out_specs=[_heads(tm), _heads(tm), _rows(tm, EVEN_W), cols, cols, _whole((EVEN_W, D_MODEL))],
        out_shape=[hs, hs, jax.ShapeDtypeStruct((s, EVEN_W), BF16), cs, cs, jax.ShapeDtypeStruct((EVEN_W, D_MODEL), F32)],
        compiler_params=_cparams(("arbitrary",), 48),
    )(dy1, w_out_t, fo, so, gate)


def even_in_bwd(dfqs, dfkn, dfv, dsqs, dsk, dsv, dgate, dflog, fqr, fkr, gq, gk, w_pack_t, x, dy1, gnorm):
    s = x.shape[0]

    def body(dfq_ref, dfk_ref, dfv_ref, dsq_ref, dsk_ref, dsv_ref, dg_ref, dfl_ref, qr_ref, kr_ref, gq_ref, gk_ref,
             wt_ref, x_ref, dy_ref, gn_ref, dproj_ref, dx_ref, dgn_ref, dgain_ref, dbf_ref):
        @pl.when(pl.program_id(0) == 0)
        def _():
            dgn_ref[...] = jnp.zeros_like(dgn_ref)
            dgain_ref[...] = jnp.zeros_like(dgain_ref)
            dbf_ref[...] = jnp.zeros_like(dbf_ref)

        o = FL_OFF + LANES
        for hh in range(NH):
            sl = slice(hh * HD, (hh + 1) * HD)
            dq, gq_row = _seg_rms_bwd(qr_ref[hh], dfq_ref[hh], gq_ref[:, sl], SCALE_E, HD)
            dk, gk_row = _seg_rms_bwd(kr_ref[hh], dfk_ref[hh] * LN2, gk_ref[:, sl], 1.0, HD)
            dproj_ref[:, sl] = dq.astype(BF16)
            dproj_ref[:, 512 + hh * HD:512 + (hh + 1) * HD] = dk.astype(BF16)
            dproj_ref[:, 1024 + hh * HD:1024 + (hh + 1) * HD] = dfv_ref[hh].astype(BF16)
            dproj_ref[:, o + hh * HD:o + (hh + 1) * HD] = (dsq_ref[hh] * SCALE_E).astype(BF16)
            dproj_ref[:, o + 512 + hh * HD:o + 512 + (hh + 1) * HD] = (dsk_ref[hh] * LN2).astype(BF16)
            dproj_ref[:, o + 1024 + hh * HD:o + 1024 + (hh + 1) * HD] = dsv_ref[hh].astype(BF16)
            dgain_ref[0:1, sl] += gq_row
            dgain_ref[1:2, sl] += gk_row
        dfl = dfl_ref[...]
        dproj_ref[:, FL_OFF:FL_OFF + LANES] = dfl.astype(BF16)
        dbf_ref[...] += jnp.sum(dfl, axis=0, keepdims=True)
        dproj_ref[:, o + 1536:o + 2560] = dg_ref[...]
        dh = _dotf(dproj_ref[...], wt_ref[...])
        dx, gn_row = _rms_bwd(x_ref[...], dh, gn_ref[...])
        dx_ref[...] = dy_ref[...] + dx
        dgn_ref[...] += gn_row

    return pl.pallas_call(
        body, name="even_in_bwd", grid=(s // TM,),
        in_specs=[_heads(TM)] * 6 + [_rows(TM, EVEN_W), _rows(TM, LANES), _heads(TM), _heads(TM), _whole((1, 512)), _whole((1, 512)),
                                     _whole((EVEN_PACK, D_MODEL)), _rows(TM, D_MODEL), _rows(TM, D_MODEL), _whole((1, D_MODEL))],
        out_specs=[_rows(TM, EVEN_PACK), _rows(TM, D_MODEL), _whole((1, D_MODEL)), _whole((8, 512)), _whole((1, LANES))],
        out_shape=[jax.ShapeDtypeStruct((s, EVEN_PACK), BF16), jax.ShapeDtypeStruct((s, D_MODEL), F32),
                   jax.ShapeDtypeStruct((1, D_MODEL), F32), jax.ShapeDtypeStruct((8, 512), F32), jax.ShapeDtypeStruct((1, LANES), F32)],
        compiler_params=_cparams(("arbitrary",), 52),
    )(dfqs, dfkn, dfv, dsqs, dsk, dsv, dgate, dflog, fqr, fkr, gq, gk, w_pack_t, x, dy1, gnorm)


def matmul_tn(a, b, tm, name):
    s, m = a.shape
    n = b.shape[1]
    tk = 2 * TM
    nk = s // tk

    def body(a_ref, b_ref, o_ref):
        @pl.when(pl.program_id(1) == 0)
        def _():
            o_ref[...] = jnp.zeros_like(o_ref)

        o_ref[...] += _dot_tn(a_ref[...], b_ref[...])

    return pl.pallas_call(
        body, name=name, grid=(m // tm, nk),
        in_specs=[pl.BlockSpec((tk, tm), lambda j, k: (k, j)), pl.BlockSpec((tk, n), lambda j, k: (k, 0))],
        out_specs=pl.BlockSpec((tm, n), lambda j, k: (j, 0)),
        out_shape=jax.ShapeDtypeStruct((m, n), F32),
        compiler_params=_cparams(("parallel", "arbitrary"), 32),
    )(a, b)


def _tile_gain(g, reps):
    return jnp.tile(g.reshape(1, -1), (1, reps))


def local_step(x, target, w_in_e_t, b_f, gq_e, gk_e, gn_e, w_out_e, gn_o, w_in_o_t, gq_o, gk_o, w_out_o, fox_blocks, sb_blocks):
    s = x.shape[0]
    r = s // LANES
    w_pack_t = jnp.concatenate([w_in_e_t[:FL_OFF + NH], jnp.zeros((LANES - NH, D_MODEL), BF16), w_in_e_t[FL_OFF + NH:]], axis=0)
    w_pack, w_in_o = w_pack_t.T, w_in_o_t.T
    bf_pad = jnp.pad(b_f.reshape(1, NH), ((0, 0), (0, LANES - NH)))
    gq512, gk512 = _tile_gain(gq_e, NH), _tile_gain(gk_e, NH)
    gq2, gk2 = _tile_gain(gq_o, NH2), _tile_gain(gk_o, NH2)
    gn_e, gn_o = gn_e.reshape(1, D_MODEL), gn_o.reshape(1, D_MODEL)

    h, fqs, fkn, fv, fqr, fkr, flog, sqs, sk, sv, gate = even_in_fwd(x, gn_e, w_pack, bf_pad, gq512, gk512)
    flog4 = flog[:, :NH].T.reshape(NH, r, LANES)
    nc = fox_cum(flog4).reshape(NH, 1, s)
    fo, lse = fox_fwd(fqs, fkn, fv, nc, *fox_blocks)
    so = sb_fwd(sqs, sk, sv, *sb_blocks)
    y1 = even_out_fwd(fo, so, gate, x, w_out_e)
    h2, q2s, k2n, v2, q2r, k2r, gate2 = odd_in_fwd(y1, gn_o, w_in_o, gq2, gk2)
    ol = [dil_fwd(q2s, k2n, v2, g) for g in range(NG)]
    att, lse2, dy2, loss_parts = odd_out_fwd(ol[0][0], ol[0][1], ol[1][0], ol[1][1], ol[2][0], ol[2][1], gate2, y1, target, w_out_o)
    loss = jnp.sum(loss_parts[:, 0, 0])
    datt, dgate2, delta2, d_w_out_o = odd_out_bwd(dy2, w_out_o.T, att, gate2)
    dqkv = [dil_bwd(q2s, k2n, v2, datt, lse2, delta2, g) for g in range(NG)]
    dproj2, dy1, d_gn_o, dgain_o = odd_in_bwd([t[0] for t in dqkv], [t[1] for t in dqkv], [t[2] for t in dqkv], dgate2,
                                              q2r, k2r, gq2, gk2, w_in_o_t, y1, dy2, gn_o)
    d_w_in_o_t = matmul_tn(dproj2, h2, ODD_IN // 4, "dw_in_odd")
    dfo, dso, dgate, delta_f, delta_s, d_w_out_e = even_out_bwd(dy1, w_out_e.T, fo, so, gate)
    dfqs, dfkn, dfv, dnc, drow = fox_bwd(fqs, fkn, fv, nc, lse, dfo, delta_f, fox_blocks[0], fox_blocks[1] // 2)
    dsqs, dsk, dsv = sb_bwd(sqs, sk, sv, dso, delta_s, *sb_blocks)
    dcum4 = (drow.reshape(NH, s) - dnc.reshape(NH, s)).reshape(NH, r, LANES)
    dflog4 = fox_cum_bwd(dcum4, flog4)
    dflog = jnp.pad(dflog4.reshape(NH, s).T, ((0, 0), (0, LANES - NH)))
    dproj, grad_x, d_gn_e, dgain_e, d_bf = even_in_bwd(dfqs, dfkn, dfv, dsqs, dsk, dsv, dgate, dflog, fqr, fkr, gq512, gk512,
                                                       w_pack_t, x, dy1, gn_e)
    d_w_pack_t = matmul_tn(dproj, h, EVEN_PACK // 3, "dw_in_even")
    d_w_in_e_t = jnp.concatenate([d_w_pack_t[:FL_OFF + NH], d_w_pack_t[FL_OFF + LANES:]], axis=0)
    grads = dict(
        even_norm=d_gn_e.reshape(-1), even_w_in_t=d_w_in_e_t, even_b_f=d_bf[0, :NH],
        even_q_gain=dgain_e[0].reshape(NH, HD).sum(0), even_k_gain=dgain_e[1].reshape(NH, HD).sum(0),
        even_w_out=d_w_out_e, odd_norm=d_gn_o.reshape(-1), odd_w_in_t=d_w_in_o_t,
        odd_q_gain=dgain_o[:NG].reshape(NG * NH2, HD2).sum(0), odd_k_gain=dgain_o[NG:2 * NG].reshape(NG * NH2, HD2).sum(0),
        odd_w_out=d_w_out_o)
    return loss, grad_x, grads


BIG = (("even_w_in_t", (EVEN_IN // 4, D_MODEL)), ("even_w_out", (EVEN_W // 4, D_MODEL)),
       ("odd_w_in_t", (ODD_IN // 4, D_MODEL)), ("odd_w_out", (ODD_W, D_MODEL // 4)))
VECTORS = (("odd_norm", D_MODEL // 4), ("odd_norm_lo", D_MODEL // 4), ("even_norm", D_MODEL), ("even_b_f", NH),
           ("even_q_gain", HD), ("even_k_gain", HD), ("odd_q_gain", HD2), ("odd_k_gain", HD2))
TILE_ROWS = 16


def _block_rows(shape):
    return -(-(shape[0] * shape[1] // D_MODEL) // TILE_ROWS) * TILE_ROWS


PACK_ROWS = sum(_block_rows(shape) for _, shape in BIG) + TILE_ROWS
HALF = PACK_ROWS // 2
assert HALF % TILE_ROWS == 0
HBM = pl.BlockSpec(memory_space=pl.ANY)


def _vector_block(parts):
    rows = [jnp.pad(parts[n].reshape(-1), (0, D_MODEL - size)) for n, size in VECTORS]
    return jnp.pad(jnp.stack(rows), ((0, TILE_ROWS - len(VECTORS)), (0, 0)))


def _vectors_of(block):
    return {n: block[i, :size] for i, (n, size) in enumerate(VECTORS)}


def _pack(parts):
    blocks = []
    for n, shape in BIG:
        t = parts[n].reshape(-1, D_MODEL)
        blocks.append(jnp.pad(t, ((0, _block_rows(shape) - t.shape[0]), (0, 0))))
    blocks.append(_vector_block(parts).astype(blocks[0].dtype))
    return jnp.concatenate(blocks, axis=0)


def _unpack(buf):
    out, off = {}, 0
    for n, shape in BIG:
        rows = shape[0] * shape[1] // D_MODEL
        out[n] = buf[off:off + rows].reshape(shape)
        off += _block_rows(shape)
    out.update(_vectors_of(buf[off:off + TILE_ROWS]))
    return out


def _place():
    x, y, c = lax.axis_index("x"), lax.axis_index("y"), lax.axis_index("c")
    return x, y, c, [(1 - x, y), (x, 1 - y), (1 - x, 1 - y)]


def all_gather_shards(mine):
    def body(src_ref, out_ref, send_sems, recv_sems, local_sem):
        x, y, c, chips = _place()
        me = 2 * x + y
        half = lambda cc: pl.ds(cc * HALF, HALF)

        def copy(k, j, cc, to, src=None):
            dst = out_ref.at[j, half(cc)]
            return pltpu.make_async_remote_copy(src_ref=dst if src is None else src, dst_ref=dst,
                                                send_sem=send_sems.at[k], recv_sem=recv_sems.at[k],
                                                device_id=to, device_id_type=MESH)

        local = pltpu.make_async_copy(src_ref, out_ref.at[me], local_sem)
        local.start()
        first = [copy(k, me, c, (cx, cy, c), src=src_ref.at[half(c)]) for k, (cx, cy) in enumerate(chips)]
        for cp in first:
            cp.start()
        passed = [copy(3 + k, 2 * cx + cy, c, (x, y, 1 - c)) for k, (cx, cy) in enumerate(chips)]
        for k, (cx, cy) in enumerate(chips):
            copy(k, 2 * cx + cy, c, (x, y, c)).wait_recv()
            passed[k].start()
        for k, (cx, cy) in enumerate(chips):
            copy(3 + k, 2 * cx + cy, 1 - c, (x, y, c)).wait_recv()
        for cp in first + passed:
            cp.wait_send()
        local.wait()

    return pl.pallas_call(
        body, name="all_gather_shards",
        in_specs=[HBM], out_specs=HBM,
        out_shape=jax.ShapeDtypeStruct((4, PACK_ROWS, D_MODEL), mine.dtype),
        scratch_shapes=[pltpu.SemaphoreType.DMA((6,)), pltpu.SemaphoreType.DMA((6,)), pltpu.SemaphoreType.DMA],
    )(mine)


def sibling_swap_halves(g):
    def body(g_ref, a_ref, send_sem, recv_sem):
        x, y, c, _ = _place()
        cp = pltpu.make_async_remote_copy(src_ref=g_ref.at[:, pl.ds((1 - c) * HALF, HALF)], dst_ref=a_ref,
                                          send_sem=send_sem, recv_sem=recv_sem, device_id=(x, y, 1 - c), device_id_type=MESH)
        cp.start()
        cp.wait()

    return pl.pallas_call(
        body, name="sibling_swap_halves",
        in_specs=[HBM], out_specs=HBM,
        out_shape=jax.ShapeDtypeStruct((4, HALF, D_MODEL), g.dtype),
        scratch_shapes=[pltpu.SemaphoreType.DMA, pltpu.SemaphoreType.DMA],
    )(g)


def chip_exchange(p):
    def body(p_ref, b_ref, send_sems, recv_sems, local_sem):
        x, y, c, chips = _place()
        me = 2 * x + y
        local = pltpu.make_async_copy(p_ref.at[me], b_ref.at[me], local_sem)
        local.start()
        sends = [pltpu.make_async_remote_copy(src_ref=p_ref.at[2 * cx + cy], dst_ref=b_ref.at[me],
                                              send_sem=send_sems.at[k], recv_sem=recv_sems.at[k],
                                              device_id=(cx, cy, c), device_id_type=MESH)
                 for k, (cx, cy) in enumerate(chips)]
        for cp in sends:
            cp.start()
        for k, (cx, cy) in enumerate(chips):
            pltpu.make_async_remote_copy(src_ref=p_ref.at[me], dst_ref=b_ref.at[2 * cx + cy],
                                         send_sem=send_sems.at[k], recv_sem=recv_sems.at[k],
                                         device_id=(cx, cy, c), device_id_type=MESH).wait_recv()
        for cp in sends:
            cp.wait_send()
        local.wait()

    return pl.pallas_call(
        body, name="chip_exchange",
        in_specs=[HBM], out_specs=HBM,
        out_shape=jax.ShapeDtypeStruct((4, HALF, D_MODEL), p.dtype),
        scratch_shapes=[pltpu.SemaphoreType.DMA((3,)), pltpu.SemaphoreType.DMA((3,)), pltpu.SemaphoreType.DMA],
    )(p)


def sibling_join_halves(mine):
    def body(h_ref, out_ref, send_sem, recv_sem, local_sem):
        x, y, c, _ = _place()
        local = pltpu.make_async_copy(h_ref, out_ref.at[pl.ds(c * HALF, HALF)], local_sem)
        local.start()
        cp = pltpu.make_async_remote_copy(src_ref=h_ref, dst_ref=out_ref.at[pl.ds(c * HALF, HALF)],
                                          send_sem=send_sem, recv_sem=recv_sem, device_id=(x, y, 1 - c), device_id_type=MESH)
        cp.start()
        cp.wait_send()
        pltpu.make_async_remote_copy(src_ref=h_ref, dst_ref=out_ref.at[pl.ds((1 - c) * HALF, HALF)],
                                     send_sem=send_sem, recv_sem=recv_sem, device_id=(x, y, 1 - c), device_id_type=MESH).wait_recv()
        local.wait()

    return pl.pallas_call(
        body, name="sibling_join_halves",
        in_specs=[HBM], out_specs=HBM,
        out_shape=jax.ShapeDtypeStruct((PACK_ROWS, D_MODEL), mine.dtype),
        scratch_shapes=[pltpu.SemaphoreType.DMA, pltpu.SemaphoreType.DMA, pltpu.SemaphoreType.DMA],
    )(mine)


def _sum_call(name, arrays, rows):
    tr = rows // 5 if rows % 40 == 0 else rows

    def body(*refs):
        acc = refs[0][...]
        for r in refs[1:-1]:
            acc = acc + r[...]
        refs[-1][...] = acc

    spec = pl.BlockSpec((tr, D_MODEL), lambda i: (i, 0))
    return pl.pallas_call(
        body, name=name, grid=(rows // tr,),
        in_specs=[spec] * len(arrays), out_specs=spec,
        out_shape=jax.ShapeDtypeStruct((rows, D_MODEL), F32),
        compiler_params=_cparams(("parallel",), 40),
    )(*arrays)


def adamw(w, g, m, v, name):
    rows, cols = w.shape
    tr = min(rows, TM)

    def body(w_ref, g_ref, m_ref, v_ref, d_ref, nm_ref, nv_ref):
        gt = g_ref[...]
        nm = ADAM_B1 * m_ref[...] + (1.0 - ADAM_B1) * gt
        nv = ADAM_B2 * v_ref[...] + (1.0 - ADAM_B2) * (gt * gt)
        m_hat = nm / (1.0 - ADAM_B1 ** ADAM_STEP)
        v_hat = nv / (1.0 - ADAM_B2 ** ADAM_STEP)
        d_ref[...] = -ADAM_LR * (m_hat / (jnp.sqrt(v_hat) + ADAM_EPS) + ADAM_WD * w_ref[...])
        nm_ref[...] = nm
        nv_ref[...] = nv

    spec = pl.BlockSpec((tr, cols), lambda i: (i, 0))
    shape = jax.ShapeDtypeStruct((rows, cols), F32)
    return pl.pallas_call(
        body, name=name, grid=(rows // tr,),
        in_specs=[spec] * 4, out_specs=[spec] * 3, out_shape=[shape] * 3,
        compiler_params=_cparams(("parallel",), 40),
    )(w, g, m, v)


def kernel(x, even_norm, even_w_in, even_b_f, even_q_gain, even_k_gain, even_w_out, odd_norm, odd_w_in, odd_q_gain, odd_k_gain, odd_w_out, loss_target, m_even_norm, m_even_w_in, m_even_b_f, m_even_q_gain, m_even_k_gain, m_even_w_out, m_odd_norm, m_odd_w_in, m_odd_q_gain, m_odd_k_gain, m_odd_w_out, v_even_norm, v_even_w_in, v_even_b_f, v_even_q_gain, v_even_k_gain, v_even_w_out, v_odd_norm, v_odd_w_in, v_odd_q_gain, v_odd_k_gain, v_odd_w_out):
    w = dict(even_norm=even_norm, even_w_in=even_w_in, even_b_f=even_b_f, even_q_gain=even_q_gain, even_k_gain=even_k_gain,
             even_w_out=even_w_out, odd_norm=odd_norm, odd_w_in=odd_w_in, odd_q_gain=odd_q_gain, odd_k_gain=odd_k_gain,
             odd_w_out=odd_w_out)
    m = dict(even_norm=m_even_norm, even_w_in=m_even_w_in, even_b_f=m_even_b_f, even_q_gain=m_even_q_gain,
             even_k_gain=m_even_k_gain, even_w_out=m_even_w_out, odd_norm=m_odd_norm, odd_w_in=m_odd_w_in,
             odd_q_gain=m_odd_q_gain, odd_k_gain=m_odd_k_gain, odd_w_out=m_odd_w_out)
    v = dict(even_norm=v_even_norm, even_w_in=v_even_w_in, even_b_f=v_even_b_f, even_q_gain=v_even_q_gain,
             even_k_gain=v_even_k_gain, even_w_out=v_even_w_out, odd_norm=v_odd_norm, odd_w_in=v_odd_w_in,
             odd_q_gain=v_odd_q_gain, odd_k_gain=v_odd_k_gain, odd_w_out=v_odd_w_out)
    spare = jnp.zeros((D_MODEL // 4,), F32)
    vector_names = [n for n, _ in VECTORS if n != "odd_norm_lo"]
    flat = lambda d: {**{n: d[n].reshape(-1) for n in vector_names}, "odd_norm_lo": spare}

    on = odd_norm.reshape(-1)
    on_hi = on.astype(BF16)
    wire = {n: t.astype(BF16) for n, t in flat(w).items()}
    wire.update(odd_norm=on_hi, odd_norm_lo=(on - on_hi.astype(F32)).astype(BF16),
                even_w_in_t=even_w_in[0].T.astype(BF16), even_w_out=even_w_out[0].astype(BF16),
                odd_w_in_t=odd_w_in[0].T.astype(BF16), odd_w_out=odd_w_out[0].astype(BF16))
    gathered = all_gather_shards(_pack(wire))
    sh = [_unpack(gathered[j]) for j in range(4)]
    cat = lambda n, axis: jnp.concatenate([t[n] for t in sh], axis=axis)
    gn_o = cat("odd_norm", 0).astype(F32) + cat("odd_norm_lo", 0).astype(F32)

    s = x.shape[1]
    loss_local, grad_x, g = local_step(
        x[0], loss_target[0], cat("even_w_in_t", 0), even_b_f[0], even_q_gain[0], even_k_gain[0], even_norm[0],
        cat("even_w_out", 0), gn_o, cat("odd_w_in_t", 0), odd_q_gain[0], odd_k_gain[0], cat("odd_w_out", 1),
        (min(1024, s), min(512, s)), (min(1024, s), min(256, s)))
    loss = lax.psum(loss_local, ("x", "y", "c"))

    def grad_parts(j):
        parts = {n: g[n] for n in vector_names}
        parts["odd_norm"] = g["odd_norm"][j * (D_MODEL // 4):(j + 1) * (D_MODEL // 4)]
        parts["odd_norm_lo"] = spare
        parts["even_w_in_t"] = g["even_w_in_t"][j * (EVEN_IN // 4):(j + 1) * (EVEN_IN // 4)]
        parts["even_w_out"] = g["even_w_out"][j * (EVEN_W // 4):(j + 1) * (EVEN_W // 4)]
        parts["odd_w_in_t"] = g["odd_w_in_t"][j * (ODD_IN // 4):(j + 1) * (ODD_IN // 4)]
        parts["odd_w_out"] = g["odd_w_out"][:, j * (D_MODEL // 4):(j + 1) * (D_MODEL // 4)]
        return parts

    g_all = jnp.stack([_pack(grad_parts(j)) for j in range(4)])
    from_sibling = sibling_swap_halves(g_all)
    c = lax.axis_index("c")
    g_mine = lax.dynamic_slice_in_dim(g_all, c * HALF, HALF, axis=1)
    pair = _sum_call("pair_sum", [g_mine.reshape(4 * HALF, D_MODEL), from_sibling.reshape(4 * HALF, D_MODEL)], 4 * HALF)
    by_chip = chip_exchange(pair.reshape(4, HALF, D_MODEL))
    half_sum = _sum_call("chip_sum", [by_chip[0], by_chip[1], by_chip[2], by_chip[3]], HALF)
    g_buf = sibling_join_halves(half_sum)

    gp = _unpack(g_buf)
    grad = {n: gp[n] for n in vector_names}
    grad.update(even_w_in=gp["even_w_in_t"].T, even_w_out=gp["even_w_out"], odd_w_in=gp["odd_w_in_t"].T, odd_w_out=gp["odd_w_out"])
    results = [grad, {}, {}, {}]
    for n in ("even_w_in", "even_w_out", "odd_w_in", "odd_w_out"):
        stepped = adamw(w[n][0], grad[n], m[n][0], v[n][0], "adamw_" + n)
        for res, t in zip(results[1:], stepped):
            res[n] = t
    block = lambda d: _vector_block(flat(d))
    stepped = adamw(block(w), _vector_block({**grad, "odd_norm_lo": spare}), block(m), block(v), "adamw_vectors")
    for res, t in zip(results[1:], stepped):
        res.update(_vectors_of(t))
    outs = [loss.reshape(()), grad_x.reshape(x.shape)]
    order = ["even_norm", "even_w_in", "even_b_f", "even_q_gain", "even_k_gain", "even_w_out", "odd_norm", "odd_w_in",
             "odd_q_gain", "odd_k_gain", "odd_w_out"]
    for res in results:
        outs += [res[n].reshape(w[n].shape) for n in order]
    return tuple(outs)
```

```python
import jax
import jax.numpy as jnp
from jax import lax
from jax.experimental import pallas as pl
from jax.experimental.pallas import tpu as pltpu

F32 = jnp.float32
BF16 = jnp.bfloat16

D_MODEL = 1024
HD = 128
NH = 4
HD2 = 64
NG = 3
NH2 = 8
DILATIONS = (1, 4, 16)
SPAN = 128
EVEN_W = 1024
ODD_W = 512
EVEN_IN = 4100
EVEN_PACK = 4224
FL_OFF = 1536
ODD_IN = 5120
RMS_EPS = 1e-6
SCALE_E = HD ** -0.5
SCALE_O = HD2 ** -0.5
ADAM_LR, ADAM_B1, ADAM_B2, ADAM_EPS, ADAM_WD, ADAM_STEP = 0.001, 0.9, 0.999, 1e-08, 0.01, 10

VMEM_CAP = 64 * 1024 * 1024
LANES = 128
MESH = pl.DeviceIdType.MESH


def _cparams(sem, vmem_mb):
    return pltpu.CompilerParams(dimension_semantics=sem, vmem_limit_bytes=min(vmem_mb << 20, VMEM_CAP - (6 << 20)))


def _silu(g):
    return g / (1.0 + jnp.exp(-g))


def _dsilu(g):
    s = 1.0 / (1.0 + jnp.exp(-g))
    return s * (1.0 + g * (1.0 - s))


def _split2(x):
    hi = x.astype(BF16)
    lo = (x - hi.astype(F32)).astype(BF16)
    return hi, lo


def _split3(x):
    hi = x.astype(BF16)
    r = x - hi.astype(F32)
    mid = r.astype(BF16)
    lo = (r - mid.astype(F32)).astype(BF16)
    return hi, mid, lo


def _dotf(a, b):
    return jnp.dot(a, b, preferred_element_type=F32)


def _dot_nt(a, b):
    return lax.dot_general(a, b, (((1,), (1,)), ((), ())), preferred_element_type=F32)


def _dot_tn(a, b):
    return lax.dot_general(a, b, (((0,), (0,)), ((), ())), preferred_element_type=F32)


def _segsum(x, hd):
    r = lax.broadcasted_iota(jnp.int32, (LANES, LANES), 0) // hd
    c = lax.broadcasted_iota(jnp.int32, (LANES, LANES), 1) // hd
    ones = (r == c).astype(BF16)
    ones2 = jnp.concatenate([ones, ones], axis=0)
    outs = []
    for ch in range(x.shape[1] // LANES):
        hi, lo = _split2(x[:, ch * LANES:(ch + 1) * LANES])
        outs.append(_dotf(jnp.concatenate([hi, lo], axis=1), ones2))
    return outs[0] if len(outs) == 1 else jnp.concatenate(outs, axis=1)


def _suffix_matrix(n):
    r = lax.broadcasted_iota(jnp.int32, (n, n), 0)
    c = lax.broadcasted_iota(jnp.int32, (n, n), 1)
    return (r >= c).astype(BF16)


def _walk_up_staged(i, per, stages):
    assert per % 2 == 0
    n_full = i * per

    def pair(first, masked):
        states = [{"j": first}, {"j": first + 1}]
        for stage in stages:
            for st in states:
                stage(st, masked)

    def full_trip(t, c):
        pair(2 * t, False)
        return c

    def masked_trip(d, c):
        pair(n_full + 2 * d, True)
        return c

    lax.fori_loop(0, n_full // 2, full_trip, 0)
    lax.fori_loop(0, per // 2, masked_trip, 0)


def fox_fwd(qs, kn, v, nc, bq, bk):
    nh, s, _ = qs.shape
    nq = s // bq
    per = bq // bk

    def body(q_ref, k_ref, v_ref, nc_ref, o_ref, lse_ref):
        i = pl.program_id(1)
        q = q_ref[...]

        def logits(j, masked):
            off = pl.multiple_of(j * bk, bk)
            sc = _dot_nt(q, k_ref[pl.ds(off, bk), :]) + nc_ref[:, pl.ds(off, bk)]
            if masked:
                row = i * bq + lax.broadcasted_iota(jnp.int32, (bq, bk), 0)
                col = off + lax.broadcasted_iota(jnp.int32, (bq, bk), 1)
                sc = jnp.where(col <= row, sc, -jnp.inf)
            return off, sc

        def update(carry, off, sc):
            m, l, acc = carry
            m_new = jnp.maximum(m, jnp.max(sc, axis=1, keepdims=True))
            alpha = jnp.exp2(m - m_new)
            p = jnp.exp2(sc - m_new)
            l = alpha * l + jnp.sum(p, axis=1, keepdims=True)
            acc = alpha * acc + _dotf(p.astype(BF16), v_ref[pl.ds(off, bk), :])
            return m_new, l, acc

        def pair(first, carry, masked):
            a, b = logits(first, masked), logits(first + 1, masked)
            return update(update(carry, *a), *b)

        assert per % 2 == 0
        n_full = i * per
        init = (jnp.full((bq, 1), -jnp.inf, F32), jnp.zeros((bq, 1), F32), jnp.zeros((bq, HD), F32))
        carry = lax.fori_loop(0, n_full // 2, lambda t, c: pair(2 * t, c, False), init)
        m, l, acc = lax.fori_loop(0, per // 2, lambda t, c: pair(n_full + 2 * t, c, True), carry)
        o_ref[...] = (acc / l).astype(o_ref.dtype)
        lse_ref[...] = m + jnp.log2(l)

    return pl.pallas_call(
        body, name="fox_fwd",
        grid=(nh, nq),
        in_specs=[pl.BlockSpec((None, bq, HD), lambda h, i: (h, i, 0)),
                  pl.BlockSpec((None, s, HD), lambda h, i: (h, 0, 0)),
                  pl.BlockSpec((None, s, HD), lambda h, i: (h, 0, 0)),
                  pl.BlockSpec((None, 1, s), lambda h, i: (h, 0, 0))],
        out_specs=[pl.BlockSpec((None, bq, HD), lambda h, i: (h, i, 0)),
                   pl.BlockSpec((None, bq, 1), lambda h, i: (h, i, 0))],
        out_shape=[jax.ShapeDtypeStruct((nh, s, HD), BF16), jax.ShapeDtypeStruct((nh, s, 1), F32)],
        compiler_params=_cparams(("arbitrary", "arbitrary"), 40),
    )(qs, kn, v, nc)


LOG2E = 1.4426950408889634
LN2 = 0.6931471805599453


def _neg_abs(z):
    sign = jnp.uint32(0x80000000)
    return lax.bitcast_convert_type(lax.bitcast_convert_type(z, jnp.uint32) | sign, F32)


def _sb_softplus2(z, row0, col0, masked):
    u = jnp.maximum(z, 0.0) + jnp.log2(1.0 + jnp.exp2(_neg_abs(z)))
    strict = None
    if masked:
        row = row0 + lax.broadcasted_iota(jnp.int32, z.shape, 0)
        col = col0 + lax.broadcasted_iota(jnp.int32, z.shape, 1)
        strict = col < row
        u = jnp.where(strict, u, 0.0)
    return u, strict


def _walk_down_staged(i, per, stages, group=2):
    assert per % group == 0
    n_full = i * per

    def trip(top, masked):
        states = [{"j": top - t} for t in range(group)]
        for stage in stages:
            for st in states:
                stage(st, masked)

    def masked_trip(d, c):
        trip(n_full + per - 1 - group * d, True)
        return c

    def full_trip(t, c):
        trip(n_full - 1 - group * t, False)
        return c

    lax.fori_loop(0, per // group, masked_trip, 0)
    lax.fori_loop(0, n_full // group, full_trip, 0)


def _suffix2(x, m2):
    hi, lo = _split2(x)
    return _dotf(jnp.concatenate([hi, lo], axis=1), m2)


def _lanes(col, n):
    return jnp.broadcast_to(col, (col.shape[0], n))


def sb_fwd(qs, k, v, bq, bk):
    nh, s, _ = qs.shape
    nq = s // bq
    per = bq // bk

    def body(q_ref, k_ref, v_ref, o_ref, lrun_ref):
        i = pl.program_id(1)
        q = q_ref[...]
        tri = _suffix_matrix(bk)
        o_ref[...] = jnp.zeros_like(o_ref)
        lrun_ref[...] = jnp.zeros_like(lrun_ref)

        def logits(st, masked):
            st["off"] = pl.multiple_of(st["j"] * bk, bk)
            st["z"] = _dot_nt(q, k_ref[pl.ds(st["off"], bk), :])

        def suffix(st, masked):
            u, st["strict"] = _sb_softplus2(st["z"], i * bq, st["off"], masked)
            st["incl"] = _dotf(u.astype(BF16), tri)

        def weigh(st, masked):
            lrun = lrun_ref[...]
            w = jnp.exp2(st["z"] - st["incl"] + jnp.tile(lrun, (1, bk // LANES)))
            if masked:
                w = jnp.where(st["strict"], w, 0.0)
            o_ref[...] += _dotf(w.astype(BF16), v_ref[pl.ds(st["off"], bk), :])
            lrun_ref[...] = lrun - _lanes(st["incl"][:, 0:1], LANES)

        _walk_down_staged(i, per, [logits, suffix, weigh], group=4 if per % 4 == 0 else 2)

    return pl.pallas_call(
        body, name="sb_fwd",
        grid=(nh, nq),
        in_specs=[pl.BlockSpec((None, bq, HD), lambda h, i: (h, i, 0)),
                  pl.BlockSpec((None, s, HD), lambda h, i: (h, 0, 0)),
                  pl.BlockSpec((None, s, HD), lambda h, i: (h, 0, 0))],
        out_specs=pl.BlockSpec((None, bq, HD), lambda h, i: (h, i, 0)),
        out_shape=jax.ShapeDtypeStruct((nh, s, HD), F32),
        scratch_shapes=[pltpu.VMEM((bq, LANES), F32)],
        compiler_params=_cparams(("arbitrary", "arbitrary"), 40),
    )(qs, k, v)


def _attn_bwd_call(name, body, s, bq, ins, in_specs, extra_out_specs, extra_out_shapes, extra_scratch=()):
    nh = NH
    nq = s // bq
    return pl.pallas_call(
        body, name=name,
        grid=(nh, nq),
        in_specs=in_specs,
        out_specs=[pl.BlockSpec((None, bq, HD), lambda h, i: (h, i, 0)),
                   pl.BlockSpec(memory_space=pl.ANY), pl.BlockSpec(memory_space=pl.ANY)] + extra_out_specs,
        out_shape=[jax.ShapeDtypeStruct((nh, s, HD), F32), jax.ShapeDtypeStruct((nh, s, HD), F32),
                   jax.ShapeDtypeStruct((nh, s, HD), F32)] + extra_out_shapes,
        scratch_shapes=[pltpu.VMEM((s, HD), F32), pltpu.VMEM((s, HD), F32), pltpu.SemaphoreType.DMA((2,))] + list(extra_scratch),
        compiler_params=_cparams(("arbitrary", "arbitrary"), 52),
    )(*ins)


def _flush_dkv(i, nq, h, dk_acc, dv_acc, dk_hbm, dv_hbm, sems):
    @pl.when(i == nq - 1)
    def _():
        ck = pltpu.make_async_copy(dk_acc, dk_hbm.at[h], sems.at[0])
        cv = pltpu.make_async_copy(dv_acc, dv_hbm.at[h], sems.at[1])
        ck.start()
        cv.start()
        ck.wait()
        cv.wait()


def fox_bwd(qs, kn, v, nc, lse, do, delta, bq, bk):
    nh, s, _ = qs.shape
    nq = s // bq
    per = bq // bk

    def body(q_ref, k_ref, v_ref, nc_ref, lse_ref, do_ref, dl_ref, dq_ref, dk_hbm, dv_hbm, dnc_ref, drow_ref, dk_acc, dv_acc, sems):
        h, i = pl.program_id(0), pl.program_id(1)

        @pl.when(i == 0)
        def _():
            dk_acc[...] = jnp.zeros_like(dk_acc)
            dv_acc[...] = jnp.zeros_like(dv_acc)
            dnc_ref[...] = jnp.zeros_like(dnc_ref)

        q = q_ref[...]
        do_t = do_ref[...]
        lse_t = lse_ref[...]
        dl_t = dl_ref[...]

        dq_ref[...] = jnp.zeros_like(dq_ref)
        drow_ref[...] = jnp.zeros_like(drow_ref)

        def logits(st, masked):
            st["off"] = pl.multiple_of(st["j"] * bk, bk)
            st["sc"] = _dot_nt(q, k_ref[pl.ds(st["off"], bk), :]) + nc_ref[:, pl.ds(st["off"], bk)]
            st["dp"] = _dot_nt(do_t, v_ref[pl.ds(st["off"], bk), :])

        def grads(st, masked):
            off = st["off"]
            p = jnp.exp2(st["sc"] - lse_t)
            if masked:
                row = i * bq + lax.broadcasted_iota(jnp.int32, (bq, bk), 0)
                col = off + lax.broadcasted_iota(jnp.int32, (bq, bk), 1)
                p = jnp.where(col <= row, p, 0.0)
            ds = p * (st["dp"] - dl_t)
            dsb = ds.astype(BF16)
            dq_ref[...] += _dotf(dsb, k_ref[pl.ds(off, bk), :])
            dk_acc[pl.ds(off, bk), :] += _dot_tn(dsb, q)
            dv_acc[pl.ds(off, bk), :] += _dot_tn(p.astype(BF16), do_t)
            dnc_ref[:, pl.ds(off, bk)] += jnp.sum(ds, axis=0, keepdims=True)
            drow_ref[...] += jnp.sum(ds, axis=1, keepdims=True)

        _walk_up_staged(i, per, [logits, grads])
        _flush_dkv(i, nq, h, dk_acc, dv_acc, dk_hbm, dv_hbm, sems)

    tile_spec = pl.BlockSpec((None, bq, HD), lambda h, i: (h, i, 0))
    col_spec = pl.BlockSpec((None, bq, 1), lambda h, i: (h, i, 0))
    full_spec = pl.BlockSpec((None, s, HD), lambda h, i: (h, 0, 0))
    row_spec = pl.BlockSpec((None, 1, s), lambda h, i: (h, 0, 0))
    return _attn_bwd_call("fox_bwd", body, s, bq, (qs, kn, v, nc, lse, do, delta),
                          [tile_spec, full_spec, full_spec, row_spec, col_spec, tile_spec, col_spec],
                          [row_spec, col_spec],
                          [jax.ShapeDtypeStruct((nh, 1, s), F32), jax.ShapeDtypeStruct((nh, s, 1), F32)])


def sb_bwd(qs, k, v, do, delta, bq, bk):
    nh, s, _ = qs.shape
    nq = s // bq
    per = bq // bk

    def body(q_ref, k_ref, v_ref, do_ref, dl_ref, dq_ref, dk_hbm, dv_hbm, dk_acc, dv_acc, sems, lrun_ref, crun_ref):
        h, i = pl.program_id(0), pl.program_id(1)

        @pl.when(i == 0)
        def _():
            dk_acc[...] = jnp.zeros_like(dk_acc)
            dv_acc[...] = jnp.zeros_like(dv_acc)

        q = q_ref[...]
        do_t = do_ref[...]
        tri = _suffix_matrix(bk)
        tri2 = jnp.concatenate([tri, tri], axis=0)
        dq_ref[...] = jnp.zeros_like(dq_ref)
        lrun_ref[...] = jnp.zeros_like(lrun_ref)
        crun_ref[...] = _lanes(dl_ref[...], LANES)

        def logits(st, masked):
            st["off"] = pl.multiple_of(st["j"] * bk, bk)
            st["z"] = _dot_nt(q, k_ref[pl.ds(st["off"], bk), :])
            st["dw"] = _dot_nt(do_t, v_ref[pl.ds(st["off"], bk), :])

        def suffix(st, masked):
            st["u"], st["strict"] = _sb_softplus2(st["z"], i * bq, st["off"], masked)
            st["incl"] = _dotf(st["u"].astype(BF16), tri)

        def weigh(st, masked):
            lrun = lrun_ref[...]
            w = jnp.exp2(st["z"] - st["incl"] + jnp.tile(lrun, (1, bk // LANES)))
            if masked:
                w = jnp.where(st["strict"], w, 0.0)
            st["wb"] = w.astype(BF16)
            st["e"] = st["dw"] * st["wb"].astype(F32)
            st["einc"] = _suffix2(st["e"], tri2)
            lrun_ref[...] = lrun - _lanes(st["incl"][:, 0:1], LANES)

        def grads(st, masked):
            crun = crun_ref[...]
            prefix = jnp.tile(crun, (1, bk // LANES)) - st["einc"]
            dz = st["e"] - jnp.exp2(st["z"] - st["u"]) * (st["e"] + prefix)
            if masked:
                dz = jnp.where(st["strict"], dz, 0.0)
            dzb = dz.astype(BF16)
            dq_ref[...] += _dotf(dzb, k_ref[pl.ds(st["off"], bk), :])
            dk_acc[pl.ds(st["off"], bk), :] += _dot_tn(dzb, q)
            dv_acc[pl.ds(st["off"], bk), :] += _dot_tn(st["wb"], do_t)
            crun_ref[...] = crun - _lanes(st["einc"][:, 0:1], LANES)

        _walk_down_staged(i, per, [logits, suffix, weigh, grads])
        _flush_dkv(i, nq, h, dk_acc, dv_acc, dk_hbm, dv_hbm, sems)

    tile_spec = pl.BlockSpec((None, bq, HD), lambda h, i: (h, i, 0))
    col_spec = pl.BlockSpec((None, bq, 1), lambda h, i: (h, i, 0))
    full_spec = pl.BlockSpec((None, s, HD), lambda h, i: (h, 0, 0))
    return _attn_bwd_call("sb_bwd", body, s, bq, (qs, k, v, do, delta),
                          [tile_spec, full_spec, full_spec, tile_spec, col_spec], [], [],
                          [pltpu.VMEM((bq, LANES), F32), pltpu.VMEM((bq, LANES), F32)])


BI = SPAN
PAIR = 2 * HD2


def _slopes(g):
    return [float(2.0 ** (-8.0 * (g * NH2 + h + 1) / (NG * NH2))) for h in range(NH2)]


def _head_lanes(hh):
    return (lax.broadcasted_iota(jnp.int32, (1, PAIR), 1) // HD2) == hh


def _dil_masks(n, d):
    a = lax.broadcasted_iota(jnp.int32, (BI, 2 * BI), 0)
    c = lax.broadcasted_iota(jnp.int32, (BI, 2 * BI), 1)
    dist = a - c + BI
    valid = (dist >= 0) & (dist <= SPAN) & ((c >= BI) | (n > 0))
    return valid, (dist * d).astype(F32)


def _dil_pair_fwd(qp, kcat, vcat, valid, distf, slopes2):
    o_pair = jnp.zeros((BI, PAIR), F32)
    lse_pair = jnp.zeros((BI, PAIR), F32)
    for hh in range(2):
        lm = _head_lanes(hh)
        qm = jnp.where(lm, qp, jnp.zeros_like(qp))
        logits = jnp.where(valid, _dot_nt(qm, kcat) - slopes2[hh] * distf, -jnp.inf)
        m = jnp.max(logits, axis=1, keepdims=True)
        p = jnp.exp(logits - m)
        den = jnp.sum(p, axis=1, keepdims=True)
        o_pair = jnp.where(lm, _dotf(p.astype(BF16), vcat) / den, o_pair)
        lse_pair = jnp.where(lm, m + jnp.log(den), lse_pair)
    return o_pair, lse_pair


def _dil_pair_bwd(qc, kcat, vcat, doc, lse_c, dl_c, valid, distf, slopes2):
    dq_pair = jnp.zeros((BI, PAIR), F32)
    dk_cat = jnp.zeros((2 * BI, PAIR), F32)
    dv_cat = jnp.zeros((2 * BI, PAIR), F32)
    for hh in range(2):
        col = slice(hh * HD2, hh * HD2 + 1)
        lm = _head_lanes(hh)
        zq = jnp.zeros_like(qc)
        qm = jnp.where(lm, qc, zq)
        dom = jnp.where(lm, doc, zq)
        logits = _dot_nt(qm, kcat) - slopes2[hh] * distf
        p = jnp.exp(jnp.where(valid, logits, -jnp.inf) - lse_c[:, col])
        ds = (p * (_dot_nt(dom, vcat) - dl_c[:, col])).astype(BF16)
        dq_pair = jnp.where(lm, _dotf(ds, kcat), dq_pair)
        dk_cat = dk_cat + _dot_tn(ds, qm)
        dv_cat = dv_cat + _dot_tn(p.astype(BF16), dom)
    return dq_pair, dk_cat, dv_cat


def _pair_slopes(slopes, hp):
    out = []
    for hh in range(2):
        acc = jnp.float32(slopes[hh])
        for t in range(1, NH2 // 2):
            acc = jnp.where(hp == t, jnp.float32(slopes[2 * t + hh]), acc)
        out.append(acc)
    return out


def _for_residues(d, residue):
    per_trip = min(d, 4)

    def trip(t, carry):
        for u in range(per_trip):
            residue(t * per_trip + u)
        return carry

    if d == per_trip:
        trip(0, 0)
    else:
        lax.fori_loop(0, d // per_trip, trip, 0)


def _dil_tiling(d):
    return NH2 // 2 if d == 1 else 1


def dil_fwd(q, k, v, g):
    d = DILATIONS[g]
    s = q.shape[1]
    rows_per = BI * d
    nblk = s // rows_per
    tiles = _dil_tiling(d)
    slopes = _slopes(g)

    def body(q_ref, kc_ref, kp_ref, vc_ref, vp_ref, o_ref, lse_ref):
        n = pl.program_id(0)
        valid, distf = _dil_masks(n, d)
        for t in range(tiles):
            sl = slice(t * PAIR, (t + 1) * PAIR)
            slopes2 = _pair_slopes(slopes, pl.program_id(1) * tiles + t)

            def residue(r):
                rows = pl.ds(r, BI, stride=d)
                kcat = jnp.concatenate([kp_ref[rows, sl], kc_ref[rows, sl]], axis=0).astype(BF16)
                vcat = jnp.concatenate([vp_ref[rows, sl], vc_ref[rows, sl]], axis=0).astype(BF16)
                o_ref[rows, sl], lse_ref[rows, sl] = _dil_pair_fwd(q_ref[rows, sl].astype(BF16), kcat, vcat, valid, distf, slopes2)

            _for_residues(d, residue)

    width = tiles * PAIR
    cur = pl.BlockSpec((None, rows_per, width), lambda n, hp: (g, n, hp))
    prev = pl.BlockSpec((None, rows_per, width), lambda n, hp: (g, jnp.maximum(n - 1, 0), hp))
    out = pl.BlockSpec((rows_per, width), lambda n, hp: (n, hp))
    return pl.pallas_call(
        body, name=f"dil_fwd_{g}",
        grid=(nblk, ODD_W // width),
        in_specs=[cur, cur, prev, cur, prev],
        out_specs=[out, out],
        out_shape=[jax.ShapeDtypeStruct((s, ODD_W), F32), jax.ShapeDtypeStruct((s, ODD_W), F32)],
        compiler_params=_cparams(("parallel", "parallel"), 40),
    )(q, k, k, v, v)


def dil_bwd(q, k, v, do, lse, delta, g):
    d = DILATIONS[g]
    s = q.shape[1]
    rows_per = BI * d
    nblk = s // rows_per
    tiles = _dil_tiling(d)
    slopes = _slopes(g)

    def body(q_ref, kc_ref, kp_ref, vc_ref, vp_ref, do_ref, l_ref, d_ref, dq_ref, dk_ref, dv_ref, dk_carry, dv_carry):
        first_tile, n = pl.program_id(0) * tiles, pl.program_id(1)

        @pl.when(n == 0)
        def _():
            dk_carry[...] = jnp.zeros_like(dk_carry)
            dv_carry[...] = jnp.zeros_like(dv_carry)

        @pl.when(n < nblk)
        def _():
            valid, distf = _dil_masks(n, d)
            for t in range(tiles):
                sl = slice(t * PAIR, (t + 1) * PAIR)
                slopes2 = _pair_slopes(slopes, first_tile + t)

                def residue(r):
                    rows = pl.ds(r, BI, stride=d)
                    kcat = jnp.concatenate([kp_ref[rows, sl], kc_ref[rows, sl]], axis=0).astype(BF16)
                    vcat = jnp.concatenate([vp_ref[rows, sl], vc_ref[rows, sl]], axis=0).astype(BF16)
                    dq, dk_cat, dv_cat = _dil_pair_bwd(q_ref[rows, sl].astype(BF16), kcat, vcat, do_ref[rows, sl].astype(BF16),
                                                       l_ref[rows, sl], d_ref[rows, sl], valid, distf, slopes2)
                    dq_ref[rows, sl] = dq
                    dk_ref[rows, sl] = dk_carry[rows, sl] + dk_cat[:BI]
                    dv_ref[rows, sl] = dv_carry[rows, sl] + dv_cat[:BI]
                    dk_carry[rows, sl] = dk_cat[BI:]
                    dv_carry[rows, sl] = dv_cat[BI:]

                _for_residues(d, residue)

        @pl.when(n == nblk)
        def _():
            dk_ref[...] = dk_carry[...]
            dv_ref[...] = dv_carry[...]

    width = tiles * PAIR
    cur_idx = lambda n: jnp.minimum(n, nblk - 1)
    prv_idx = lambda n: jnp.maximum(n - 1, 0)
    cur3 = pl.BlockSpec((None, rows_per, width), lambda hp, n: (g, cur_idx(n), hp))
    prv3 = pl.BlockSpec((None, rows_per, width), lambda hp, n: (g, prv_idx(n), hp))
    cur2 = pl.BlockSpec((rows_per, width), lambda hp, n: (cur_idx(n), hp))
    prv2 = pl.BlockSpec((rows_per, width), lambda hp, n: (prv_idx(n), hp))
    shape = jax.ShapeDtypeStruct((s, ODD_W), F32)
    return pl.pallas_call(
        body, name=f"dil_bwd_{g}",
        grid=(ODD_W // width, nblk + 1),
        in_specs=[cur3, cur3, prv3, cur3, prv3, cur2, cur2, cur2],
        out_specs=[cur2, prv2, prv2],
        out_shape=[shape, shape, shape],
        scratch_shapes=[pltpu.VMEM((rows_per, width), F32), pltpu.VMEM((rows_per, width), F32)],
        compiler_params=_cparams(("parallel", "arbitrary"), 48),
    )(q, k, k, v, v, do, lse, delta)


TM = 256


def _rows(tm, w):
    return pl.BlockSpec((tm, w), lambda i: (i, 0))


def _whole(shape):
    return pl.BlockSpec(shape, lambda i: (0,) * len(shape))


def _heads(tm):
    return pl.BlockSpec((NH, tm, HD), lambda i: (0, i, 0))


def _groups(tm):
    return pl.BlockSpec((NG, tm, ODD_W), lambda i: (0, i, 0))


def _rms(x):
    return lax.rsqrt(jnp.mean(x * x, axis=1, keepdims=True) + RMS_EPS)


def _seg_rms(q, hd):
    return lax.rsqrt(_segsum(q * q, hd) * (1.0 / hd) + RMS_EPS)


def _seg_rms_bwd(q_raw, dqs, gain, scale, hd):
    q = q_raw.astype(F32)
    r = _seg_rms(q, hd)
    qhat = q * r
    u = dqs * (gain * scale)
    dq = r * (u - qhat * (_segsum(u * qhat, hd) * (1.0 / hd)))
    return dq, jnp.sum(dqs * qhat, axis=0, keepdims=True) * scale


def _rms_bwd(x, dh, gain):
    r = _rms(x)
    xhat = x * r
    u = dh * gain
    dx = r * (u - xhat * jnp.mean(u * xhat, axis=1, keepdims=True))
    return dx, jnp.sum(dh * xhat, axis=0, keepdims=True)


def even_in_fwd(x, gnorm, w_pack, bf_pad, gq, gk):
    s = x.shape[0]

    def body(x_ref, g_ref, w_ref, bf_ref, gq_ref, gk_ref,
             h_ref, fqs_ref, fkn_ref, fv_ref, fqr_ref, fkr_ref, flog_ref, sqs_ref, sk_ref, sv_ref, gate_ref):
        xt = x_ref[...]
        h = (xt * _rms(xt) * g_ref[...]).astype(BF16)
        h_ref[...] = h
        proj = _dotf(h, w_ref[...])
        fq = proj[:, 0:512]
        fk = proj[:, 512:1024]
        fqs = fq * _seg_rms(fq, HD) * (gq_ref[...] * (SCALE_E * LOG2E))
        fkn = fk * _seg_rms(fk, HD) * gk_ref[...]
        flog_ref[...] = proj[:, FL_OFF:FL_OFF + LANES] + bf_ref[...]
        o = FL_OFF + LANES
        for hh in range(NH):
            sl = slice(hh * HD, (hh + 1) * HD)
            fqs_ref[hh] = fqs[:, sl].astype(BF16)
            fkn_ref[hh] = fkn[:, sl].astype(BF16)
            fqr_ref[hh] = fq[:, sl].astype(BF16)
            fkr_ref[hh] = fk[:, sl].astype(BF16)
            fv_ref[hh] = proj[:, 1024 + hh * HD:1024 + (hh + 1) * HD].astype(BF16)
            sqs_ref[hh] = (proj[:, o + hh * HD:o + (hh + 1) * HD] * (SCALE_E * LOG2E)).astype(BF16)
            sk_ref[hh] = proj[:, o + 512 + hh * HD:o + 512 + (hh + 1) * HD].astype(BF16)
            sv_ref[hh] = proj[:, o + 1024 + hh * HD:o + 1024 + (hh + 1) * HD].astype(BF16)
        gate_ref[...] = proj[:, o + 1536:o + 2560].astype(BF16)

    hs = jax.ShapeDtypeStruct((NH, s, HD), BF16)
    return pl.pallas_call(
        body, name="even_in_fwd",
        grid=(s // TM,),
        in_specs=[_rows(TM, D_MODEL), _whole((1, D_MODEL)), _whole((D_MODEL, EVEN_PACK)), _whole((1, LANES)),
                  _whole((1, 512)), _whole((1, 512))],
        out_specs=[_rows(TM, D_MODEL)] + [_heads(TM)] * 5 + [_rows(TM, LANES)] + [_heads(TM)] * 3 + [_rows(TM, EVEN_W)],
        out_shape=[jax.ShapeDtypeStruct((s, D_MODEL), BF16)] + [hs] * 5 + [jax.ShapeDtypeStruct((s, LANES), F32)]
        + [hs] * 3 + [jax.ShapeDtypeStruct((s, EVEN_W), BF16)],
        compiler_params=_cparams(("parallel",), 52),
    )(x, gnorm, w_pack, bf_pad, gq, gk)


def _prefix_matrices(r):
    a = lax.broadcasted_iota(jnp.int32, (LANES, LANES), 0)
    b = lax.broadcasted_iota(jnp.int32, (LANES, LANES), 1)
    ra = lax.broadcasted_iota(jnp.int32, (r, r), 0)
    rb = lax.broadcasted_iota(jnp.int32, (r, r), 1)
    return a, b, ra, rb


def _dot3_right(x, m):
    a, b, c = _split3(x)
    return _dotf(a, m) + _dotf(b, m) + _dotf(c, m)


def _dot3_left(m, x):
    a, b, c = _split3(x)
    return _dotf(m, a) + _dotf(m, b) + _dotf(m, c)


def fox_cum(flog4):
    nh, r, _ = flog4.shape

    def body(f_ref, nc_ref):
        z = f_ref[...]
        lf = jnp.minimum(z, 0.0) - jnp.log(1.0 + jnp.exp(-jnp.abs(z)))
        a, b, ra, rb = _prefix_matrices(r)
        within = _dot3_right(lf, (a <= b).astype(BF16))
        tot = jnp.broadcast_to(within[:, LANES - 1:LANES], (r, LANES))
        nc_ref[...] = (within + _dot3_left((rb < ra).astype(BF16), tot)) * (-LOG2E)

    return pl.pallas_call(
        body, name="fox_cum", grid=(nh,),
        in_specs=[pl.BlockSpec((None, r, LANES), lambda h: (h, 0, 0))],
        out_specs=pl.BlockSpec((None, r, LANES), lambda h: (h, 0, 0)),
        out_shape=jax.ShapeDtypeStruct((nh, r, LANES), F32),
        compiler_params=_cparams(("parallel",), 16),
    )(flog4)


def fox_cum_bwd(dcum4, flog4):
    nh, r, _ = flog4.shape

    def body(d_ref, f_ref, o_ref):
        a, b, ra, rb = _prefix_matrices(r)
        dc = d_ref[...]
        within = _dot3_right(dc, (a >= b).astype(BF16))
        tot = jnp.broadcast_to(within[:, 0:1], (r, LANES))
        dlf = within + _dot3_left((rb > ra).astype(BF16), tot)
        o_ref[...] = dlf / (1.0 + jnp.exp(f_ref[...]))

    spec = pl.BlockSpec((None, r, LANES), lambda h: (h, 0, 0))
    return pl.pallas_call(
        body, name="fox_cum_bwd", grid=(nh,),
        in_specs=[spec, spec], out_specs=spec,
        out_shape=jax.ShapeDtypeStruct((nh, r, LANES), F32),
        compiler_params=_cparams(("parallel",), 16),
    )(dcum4, flog4)


def even_out_fwd(fo, so, gate, x, w_out):
    s = x.shape[0]
    tm = 2 * TM

    def body(fo_ref, so_ref, g_ref, x_ref, w_ref, y_ref):
        sg = _silu(g_ref[...].astype(F32))
        acc = x_ref[...]
        for hh in range(NH):
            mf = (fo_ref[hh].astype(F32) * sg[:, hh * HD:(hh + 1) * HD]).astype(BF16)
            ms = (so_ref[hh] * sg[:, 512 + hh * HD:512 + (hh + 1) * HD]).astype(BF16)
            acc = acc + _dotf(mf, w_ref[hh * HD:(hh + 1) * HD, :]) + _dotf(ms, w_ref[512 + hh * HD:512 + (hh + 1) * HD, :])
        y_ref[...] = acc

    return pl.pallas_call(
        body, name="even_out_fwd", grid=(s // tm,),
        in_specs=[_heads(tm), _heads(tm), _rows(tm, EVEN_W), _rows(tm, D_MODEL), _whole((EVEN_W, D_MODEL))],
        out_specs=_rows(tm, D_MODEL),
        out_shape=jax.ShapeDtypeStruct((s, D_MODEL), F32),
        compiler_params=_cparams(("parallel",), 40),
    )(fo, so, gate, x, w_out)


def odd_in_fwd(y1, gnorm, w2, gq, gk):
    s = y1.shape[0]

    def body(x_ref, g_ref, w_ref, gq_ref, gk_ref, h_ref, qs_ref, kn_ref, v_ref, qr_ref, kr_ref, gate_ref):
        xt = x_ref[...]
        h = (xt * _rms(xt) * g_ref[...]).astype(BF16)
        h_ref[...] = h
        proj = _dotf(h, w_ref[...])
        for g in range(NG):
            q = proj[:, g * ODD_W:(g + 1) * ODD_W]
            k = proj[:, 1536 + g * ODD_W:1536 + (g + 1) * ODD_W]
            qs_ref[g] = q * _seg_rms(q, HD2) * (gq_ref[...] * SCALE_O)
            kn_ref[g] = k * _seg_rms(k, HD2) * gk_ref[...]
            qr_ref[g] = q.astype(BF16)
            kr_ref[g] = k.astype(BF16)
            v_ref[g] = proj[:, 3072 + g * ODD_W:3072 + (g + 1) * ODD_W]
        gate_ref[...] = proj[:, 4608:5120].astype(BF16)

    gs = lambda dt: jax.ShapeDtypeStruct((NG, s, ODD_W), dt)
    return pl.pallas_call(
        body, name="odd_in_fwd", grid=(s // TM,),
        in_specs=[_rows(TM, D_MODEL), _whole((1, D_MODEL)), _whole((D_MODEL, ODD_IN)), _whole((1, ODD_W)), _whole((1, ODD_W))],
        out_specs=[_rows(TM, D_MODEL)] + [_groups(TM)] * 5 + [_rows(TM, ODD_W)],
        out_shape=[jax.ShapeDtypeStruct((s, D_MODEL), BF16), gs(F32), gs(F32), gs(F32), gs(BF16), gs(BF16),
                   jax.ShapeDtypeStruct((s, ODD_W), BF16)],
        compiler_params=_cparams(("parallel",), 52),
    )(y1, gnorm, w2, gq, gk)


def odd_out_fwd(o0, l0, o1, l1, o2, l2, gate2, y1, target, w_out2):
    s = y1.shape[0]
    tm = 2 * TM
    nt = s // tm

    def body(o0_ref, l0_ref, o1_ref, l1_ref, o2_ref, l2_ref, g_ref, y1_ref, t_ref, w_ref,
             att_ref, lse_ref, dy_ref, loss_ref):
        l0t, l1t, l2t = l0_ref[...], l1_ref[...], l2_ref[...]
        m = jnp.maximum(jnp.maximum(l0t, l1t), l2t)
        e0, e1, e2 = jnp.exp(l0t - m), jnp.exp(l1t - m), jnp.exp(l2t - m)
        den = e0 + e1 + e2
        att = (e0 * o0_ref[...] + e1 * o1_ref[...] + e2 * o2_ref[...]) / den
        att_ref[...] = att.astype(BF16)
        lse_ref[...] = m + jnp.log(den)
        mixed = (att * _silu(g_ref[...].astype(F32))).astype(BF16)
        diff = y1_ref[...] + _dotf(mixed, w_ref[...]) - t_ref[...]
        dy_ref[...] = diff * (1.0 / D_MODEL)
        loss_ref[...] = jnp.full((1, 1, LANES), 0.5 / D_MODEL, F32) * jnp.sum(diff * diff)

    big = jax.ShapeDtypeStruct((s, ODD_W), F32)
    return pl.pallas_call(
        body, name="odd_out_fwd", grid=(nt,),
        in_specs=[_rows(tm, ODD_W)] * 7 + [_rows(tm, D_MODEL), _rows(tm, D_MODEL), _whole((ODD_W, D_MODEL))],
        out_specs=[_rows(tm, ODD_W), _rows(tm, ODD_W), _rows(tm, D_MODEL), pl.BlockSpec((1, 1, LANES), lambda i: (i, 0, 0))],
        out_shape=[jax.ShapeDtypeStruct((s, ODD_W), BF16), big, jax.ShapeDtypeStruct((s, D_MODEL), F32),
                   jax.ShapeDtypeStruct((nt, 1, LANES), F32)],
        compiler_params=_cparams(("parallel",), 40),
    )(o0, l0, o1, l1, o2, l2, gate2, y1, target, w_out2)


def odd_out_bwd(dy2, w_out2_t, att, gate2):
    s = dy2.shape[0]
    tm = 2 * TM

    def body(dy_ref, wt_ref, att_ref, g_ref, datt_ref, dgate_ref, delta_ref, dw_ref):
        @pl.when(pl.program_id(0) == 0)
        def _():
            dw_ref[...] = jnp.zeros_like(dw_ref)

        dyb = dy_ref[...].astype(BF16)
        dmixed = _dotf(dyb, wt_ref[...])
        g = g_ref[...].astype(F32)
        att_t = att_ref[...].astype(F32)
        sg = _silu(g)
        datt = (dmixed * sg).astype(BF16).astype(F32)
        datt_ref[...] = datt
        dgate_ref[...] = (dmixed * att_t * _dsilu(g)).astype(BF16)
        delta_ref[...] = _segsum(datt * att_t, HD2)
        dw_ref[...] += _dot_tn((att_t * sg).astype(BF16), dyb)

    return pl.pallas_call(
        body, name="odd_out_bwd", grid=(s // tm,),
        in_specs=[_rows(tm, D_MODEL), _whole((D_MODEL, ODD_W)), _rows(tm, ODD_W), _rows(tm, ODD_W)],
        out_specs=[_rows(tm, ODD_W), _rows(tm, ODD_W), _rows(tm, ODD_W), _whole((ODD_W, D_MODEL))],
        out_shape=[jax.ShapeDtypeStruct((s, ODD_W), F32), jax.ShapeDtypeStruct((s, ODD_W), BF16),
                   jax.ShapeDtypeStruct((s, ODD_W), F32), jax.ShapeDtypeStruct((ODD_W, D_MODEL), F32)],
        compiler_params=_cparams(("arbitrary",), 40),
    )(dy2, w_out2_t, att, gate2)


def odd_in_bwd(dqs, dks, dvs, dgate2, q2r, k2r, gq, gk, w2_t, y1, dy2, gnorm):
    s = y1.shape[0]

    def body(dq0, dq1, dq2, dk0, dk1, dk2, dv0, dv1, dv2, dg_ref, qr_ref, kr_ref, gq_ref, gk_ref, wt_ref, y1_ref, dy_ref, gn_ref,
             dproj_ref, dy1_ref, dgn_ref, dgain_ref):
        @pl.when(pl.program_id(0) == 0)
        def _():
            dgn_ref[...] = jnp.zeros_like(dgn_ref)
            dgain_ref[...] = jnp.zeros_like(dgain_ref)

        for g, (dq_ref, dk_ref, dv_ref) in enumerate(((dq0, dk0, dv0), (dq1, dk1, dv1), (dq2, dk2, dv2))):
            dq, gq_row = _seg_rms_bwd(qr_ref[g], dq_ref[...], gq_ref[...], SCALE_O, HD2)
            dk, gk_row = _seg_rms_bwd(kr_ref[g], dk_ref[...], gk_ref[...], 1.0, HD2)
            dproj_ref[:, g * ODD_W:(g + 1) * ODD_W] = dq.astype(BF16)
            dproj_ref[:, 1536 + g * ODD_W:1536 + (g + 1) * ODD_W] = dk.astype(BF16)
            dproj_ref[:, 3072 + g * ODD_W:3072 + (g + 1) * ODD_W] = dv_ref[...].astype(BF16)
            dgain_ref[g:g + 1, :] += gq_row
            dgain_ref[NG + g:NG + g + 1, :] += gk_row
        dproj_ref[:, 4608:5120] = dg_ref[...]
        dh = _dotf(dproj_ref[...], wt_ref[...])
        dx, gn_row = _rms_bwd(y1_ref[...], dh, gn_ref[...])
        dy1_ref[...] = dy_ref[...] + dx
        dgn_ref[...] += gn_row

    f32r, bf16r = _rows(TM, ODD_W), _rows(TM, ODD_W)
    return pl.pallas_call(
        body, name="odd_in_bwd", grid=(s // TM,),
        in_specs=[f32r] * 6 + [bf16r] * 4 + [_groups(TM), _groups(TM), _whole((1, ODD_W)), _whole((1, ODD_W)),
                                             _whole((ODD_IN, D_MODEL)), _rows(TM, D_MODEL), _rows(TM, D_MODEL), _whole((1, D_MODEL))],
        out_specs=[_rows(TM, ODD_IN), _rows(TM, D_MODEL), _whole((1, D_MODEL)), _whole((8, ODD_W))],
        out_shape=[jax.ShapeDtypeStruct((s, ODD_IN), BF16), jax.ShapeDtypeStruct((s, D_MODEL), F32),
                   jax.ShapeDtypeStruct((1, D_MODEL), F32), jax.ShapeDtypeStruct((8, ODD_W), F32)],
        compiler_params=_cparams(("arbitrary",), 52),
    )(*dqs, *dks, *dvs, dgate2, q2r, k2r, gq, gk, w2_t, y1, dy2, gnorm)


def even_out_bwd(dy1, w_out_t, fo, so, gate):
    s = dy1.shape[0]
    tm = 2 * TM

    def body(dy_ref, wt_ref, fo_ref, so_ref, g_ref, dfo_ref, dso_ref, dgate_ref, delf_ref, dels_ref, dw_ref):
        @pl.when(pl.program_id(0) == 0)
        def _():
            dw_ref[...] = jnp.zeros_like(dw_ref)

        dyb = dy_ref[...].astype(BF16)
        dmixed = _dotf(dyb, wt_ref[...])
        g = g_ref[...].astype(F32)
        sg, dsg = _silu(g), _dsilu(g)
        for hh in range(NH):
            for base, o_ref, do_ref, del_ref in ((0, fo_ref, dfo_ref, delf_ref), (512, so_ref, dso_ref, dels_ref)):
                sl = slice(base + hh * HD, base + (hh + 1) * HD)
                o = o_ref[hh].astype(F32)
                do = (dmixed[:, sl] * sg[:, sl]).astype(BF16)
                do_ref[hh] = do
                del_ref[hh] = jnp.sum(do.astype(F32) * o, axis=1, keepdims=True)
                dgate_ref[:, sl] = (dmixed[:, sl] * o * dsg[:, sl]).astype(BF16)
                dw_ref[sl, :] += _dot_tn((o * sg[:, sl]).astype(BF16), dyb)

    cols = pl.BlockSpec((NH, tm, 1), lambda i: (0, i, 0))
    hs = jax.ShapeDtypeStruct((NH, s, HD), BF16)
    cs = jax.ShapeDtypeStruct((NH, s, 1), F32)
    return pl.pallas_call(
        body, name="even_out_bwd", grid=(s // tm,),
        in_specs=[_rows(tm, D_MODEL), _whole((D_MODEL, EVEN_W)), _heads(tm), _heads(tm), _rows(tm, EVEN_W)],
        out_specs=[_heads(tm), _heads(tm), _rows(tm, EVEN_W), cols, cols, _whole((EVEN_W, D_MODEL))],
        out_shape=[hs, hs, jax.ShapeDtypeStruct((s, EVEN_W), BF16), cs, cs, jax.ShapeDtypeStruct((EVEN_W, D_MODEL), F32)],
        compiler_params=_cparams(("arbitrary",), 48),
    )(dy1, w_out_t, fo, so, gate)


def even_in_bwd(dfqs, dfkn, dfv, dsqs, dsk, dsv, dgate, dflog, fqr, fkr, gq, gk, w_pack_t, x, dy1, gnorm):
    s = x.shape[0]

    def body(dfq_ref, dfk_ref, dfv_ref, dsq_ref, dsk_ref, dsv_ref, dg_ref, dfl_ref, qr_ref, kr_ref, gq_ref, gk_ref,
             wt_ref, x_ref, dy_ref, gn_ref, dproj_ref, dx_ref, dgn_ref, dgain_ref, dbf_ref):
        @pl.when(pl.program_id(0) == 0)
        def _():
            dgn_ref[...] = jnp.zeros_like(dgn_ref)
            dgain_ref[...] = jnp.zeros_like(dgain_ref)
            dbf_ref[...] = jnp.zeros_like(dbf_ref)

        o = FL_OFF + LANES
        for hh in range(NH):
            sl = slice(hh * HD, (hh + 1) * HD)
            dq, gq_row = _seg_rms_bwd(qr_ref[hh], dfq_ref[hh], gq_ref[:, sl], SCALE_E, HD)
            dk, gk_row = _seg_rms_bwd(kr_ref[hh], dfk_ref[hh] * LN2, gk_ref[:, sl], 1.0, HD)
            dproj_ref[:, sl] = dq.astype(BF16)
            dproj_ref[:, 512 + hh * HD:512 + (hh + 1) * HD] = dk.astype(BF16)
            dproj_ref[:, 1024 + hh * HD:1024 + (hh + 1) * HD] = dfv_ref[hh].astype(BF16)
            dproj_ref[:, o + hh * HD:o + (hh + 1) * HD] = (dsq_ref[hh] * SCALE_E).astype(BF16)
            dproj_ref[:, o + 512 + hh * HD:o + 512 + (hh + 1) * HD] = (dsk_ref[hh] * LN2).astype(BF16)
            dproj_ref[:, o + 1024 + hh * HD:o + 1024 + (hh + 1) * HD] = dsv_ref[hh].astype(BF16)
            dgain_ref[0:1, sl] += gq_row
            dgain_ref[1:2, sl] += gk_row
        dfl = dfl_ref[...]
        dproj_ref[:, FL_OFF:FL_OFF + LANES] = dfl.astype(BF16)
        dbf_ref[...] += jnp.sum(dfl, axis=0, keepdims=True)
        dproj_ref[:, o + 1536:o + 2560] = dg_ref[...]
        dh = _dotf(dproj_ref[...], wt_ref[...])
        dx, gn_row = _rms_bwd(x_ref[...], dh, gn_ref[...])
        dx_ref[...] = dy_ref[...] + dx
        dgn_ref[...] += gn_row

    return pl.pallas_call(
        body, name="even_in_bwd", grid=(s // TM,),
        in_specs=[_heads(TM)] * 6 + [_rows(TM, EVEN_W), _rows(TM, LANES), _heads(TM), _heads(TM), _whole((1, 512)), _whole((1, 512)),
                                     _whole((EVEN_PACK, D_MODEL)), _rows(TM, D_MODEL), _rows(TM, D_MODEL), _whole((1, D_MODEL))],
        out_specs=[_rows(TM, EVEN_PACK), _rows(TM, D_MODEL), _whole((1, D_MODEL)), _whole((8, 512)), _whole((1, LANES))],
        out_shape=[jax.ShapeDtypeStruct((s, EVEN_PACK), BF16), jax.ShapeDtypeStruct((s, D_MODEL), F32),
                   jax.ShapeDtypeStruct((1, D_MODEL), F32), jax.ShapeDtypeStruct((8, 512), F32), jax.ShapeDtypeStruct((1, LANES), F32)],
        compiler_params=_cparams(("arbitrary",), 52),
    )(dfqs, dfkn, dfv, dsqs, dsk, dsv, dgate, dflog, fqr, fkr, gq, gk, w_pack_t, x, dy1, gnorm)


def matmul_tn(a, b, tm, name):
    s, m = a.shape
    n = b.shape[1]
    tk = 2 * TM
    nk = s // tk

    def body(a_ref, b_ref, o_ref):
        @pl.when(pl.program_id(1) == 0)
        def _():
            o_ref[...] = jnp.zeros_like(o_ref)

        o_ref[...] += _dot_tn(a_ref[...], b_ref[...])

    return pl.pallas_call(
        body, name=name, grid=(m // tm, nk),
        in_specs=[pl.BlockSpec((tk, tm), lambda j, k: (k, j)), pl.BlockSpec((tk, n), lambda j, k: (k, 0))],
        out_specs=pl.BlockSpec((tm, n), lambda j, k: (j, 0)),
        out_shape=jax.ShapeDtypeStruct((m, n), F32),
        compiler_params=_cparams(("parallel", "arbitrary"), 32),
    )(a, b)


def _tile_gain(g, reps):
    return jnp.tile(g.reshape(1, -1), (1, reps))


def local_step(x, target, w_in_e_t, b_f, gq_e, gk_e, gn_e, w_out_e, gn_o, w_in_o_t, gq_o, gk_o, w_out_o, fox_blocks, sb_blocks):
    s = x.shape[0]
    r = s // LANES
    w_pack_t = jnp.concatenate([w_in_e_t[:FL_OFF + NH], jnp.zeros((LANES - NH, D_MODEL), BF16), w_in_e_t[FL_OFF + NH:]], axis=0)
    w_pack, w_in_o = w_pack_t.T, w_in_o_t.T
    bf_pad = jnp.pad(b_f.reshape(1, NH), ((0, 0), (0, LANES - NH)))
    gq512, gk512 = _tile_gain(gq_e, NH), _tile_gain(gk_e, NH)
    gq2, gk2 = _tile_gain(gq_o, NH2), _tile_gain(gk_o, NH2)
    gn_e, gn_o = gn_e.reshape(1, D_MODEL), gn_o.reshape(1, D_MODEL)

    h, fqs, fkn, fv, fqr, fkr, flog, sqs, sk, sv, gate = even_in_fwd(x, gn_e, w_pack, bf_pad, gq512, gk512)
    flog4 = flog[:, :NH].T.reshape(NH, r, LANES)
    nc = fox_cum(flog4).reshape(NH, 1, s)
    fo, lse = fox_fwd(fqs, fkn, fv, nc, *fox_blocks)
    so = sb_fwd(sqs, sk, sv, *sb_blocks)
    y1 = even_out_fwd(fo, so, gate, x, w_out_e)
    h2, q2s, k2n, v2, q2r, k2r, gate2 = odd_in_fwd(y1, gn_o, w_in_o, gq2, gk2)
    ol = [dil_fwd(q2s, k2n, v2, g) for g in range(NG)]
    att, lse2, dy2, loss_parts = odd_out_fwd(ol[0][0], ol[0][1], ol[1][0], ol[1][1], ol[2][0], ol[2][1], gate2, y1, target, w_out_o)
    loss = jnp.sum(loss_parts[:, 0, 0])
    datt, dgate2, delta2, d_w_out_o = odd_out_bwd(dy2, w_out_o.T, att, gate2)
    dqkv = [dil_bwd(q2s, k2n, v2, datt, lse2, delta2, g) for g in range(NG)]
    dproj2, dy1, d_gn_o, dgain_o = odd_in_bwd([t[0] for t in dqkv], [t[1] for t in dqkv], [t[2] for t in dqkv], dgate2,
                                              q2r, k2r, gq2, gk2, w_in_o_t, y1, dy2, gn_o)
    d_w_in_o_t = matmul_tn(dproj2, h2, ODD_IN // 4, "dw_in_odd")
    dfo, dso, dgate, delta_f, delta_s, d_w_out_e = even_out_bwd(dy1, w_out_e.T, fo, so, gate)
    dfqs, dfkn, dfv, dnc, drow = fox_bwd(fqs, fkn, fv, nc, lse, dfo, delta_f, fox_blocks[0], fox_blocks[1] // 2)
    dsqs, dsk, dsv = sb_bwd(sqs, sk, sv, dso, delta_s, *sb_blocks)
    dcum4 = (drow.reshape(NH, s) - dnc.reshape(NH, s)).reshape(NH, r, LANES)
    dflog4 = fox_cum_bwd(dcum4, flog4)
    dflog = jnp.pad(dflog4.reshape(NH, s).T, ((0, 0), (0, LANES - NH)))
    dproj, grad_x, d_gn_e, dgain_e, d_bf = even_in_bwd(dfqs, dfkn, dfv, dsqs, dsk, dsv, dgate, dflog, fqr, fkr, gq512, gk512,
                                                       w_pack_t, x, dy1, gn_e)
    d_w_pack_t = matmul_tn(dproj, h, EVEN_PACK // 3, "dw_in_even")
    d_w_in_e_t = jnp.concatenate([d_w_pack_t[:FL_OFF + NH], d_w_pack_t[FL_OFF + LANES:]], axis=0)
    grads = dict(
        even_norm=d_gn_e.reshape(-1), even_w_in_t=d_w_in_e_t, even_b_f=d_bf[0, :NH],
        even_q_gain=dgain_e[0].reshape(NH, HD).sum(0), even_k_gain=dgain_e[1].reshape(NH, HD).sum(0),
        even_w_out=d_w_out_e, odd_norm=d_gn_o.reshape(-1), odd_w_in_t=d_w_in_o_t,
        odd_q_gain=dgain_o[:NG].reshape(NG * NH2, HD2).sum(0), odd_k_gain=dgain_o[NG:2 * NG].reshape(NG * NH2, HD2).sum(0),
        odd_w_out=d_w_out_o)
    return loss, grad_x, grads


BIG = (("even_w_in_t", (EVEN_IN // 4, D_MODEL)), ("even_w_out", (EVEN_W // 4, D_MODEL)),
       ("odd_w_in_t", (ODD_IN // 4, D_MODEL)), ("odd_w_out", (ODD_W, D_MODEL // 4)))
VECTORS = (("odd_norm", D_MODEL // 4), ("odd_norm_lo", D_MODEL // 4), ("even_norm", D_MODEL), ("even_b_f", NH),
           ("even_q_gain", HD), ("even_k_gain", HD), ("odd_q_gain", HD2), ("odd_k_gain", HD2))
TILE_ROWS = 16


def _block_rows(shape):
    return -(-(shape[0] * shape[1] // D_MODEL) // TILE_ROWS) * TILE_ROWS


PACK_ROWS = sum(_block_rows(shape) for _, shape in BIG) + TILE_ROWS
HALF = PACK_ROWS // 2
assert HALF % TILE_ROWS == 0
HBM = pl.BlockSpec(memory_space=pl.ANY)


GRAD_VECTORS = (("odd_norm", D_MODEL),) + VECTORS[2:]


def _vector_block(parts, layout=VECTORS):
    rows = [jnp.pad(parts[n].reshape(-1), (0, D_MODEL - size)) for n, size in layout]
    return jnp.pad(jnp.stack(rows), ((0, TILE_ROWS - len(layout)), (0, 0)))


def _vectors_of(block, layout=VECTORS):
    return {n: block[i, :size] for i, (n, size) in enumerate(layout)}


def _pack(parts):
    blocks = []
    for n, shape in BIG:
        t = parts[n].reshape(-1, D_MODEL)
        blocks.append(jnp.pad(t, ((0, _block_rows(shape) - t.shape[0]), (0, 0))))
    blocks.append(_vector_block(parts).astype(blocks[0].dtype))
    return jnp.concatenate(blocks, axis=0)


def _unpack(buf):
    out, off = {}, 0
    for n, shape in BIG:
        rows = shape[0] * shape[1] // D_MODEL
        out[n] = buf[off:off + rows].reshape(shape)
        off += _block_rows(shape)
    out.update(_vectors_of(buf[off:off + TILE_ROWS]))
    return out


def _place():
    x, y, c = lax.axis_index("x"), lax.axis_index("y"), lax.axis_index("c")
    return x, y, c, [(1 - x, y), (x, 1 - y), (1 - x, 1 - y)]


def all_gather_shards(mine):
    def body(src_ref, out_ref, send_sems, recv_sems, local_sem):
        x, y, c, chips = _place()
        me = 2 * x + y
        half = lambda cc: pl.ds(cc * HALF, HALF)

        def copy(k, j, cc, to, src=None):
            dst = out_ref.at[j, half(cc)]
            return pltpu.make_async_remote_copy(src_ref=dst if src is None else src, dst_ref=dst,
                                                send_sem=send_sems.at[k], recv_sem=recv_sems.at[k],
                                                device_id=to, device_id_type=MESH)

        local = pltpu.make_async_copy(src_ref, out_ref.at[me], local_sem)
        local.start()
        first = [copy(k, me, c, (cx, cy, c), src=src_ref.at[half(c)]) for k, (cx, cy) in enumerate(chips)]
        for cp in first:
            cp.start()
        passed = [copy(3 + k, 2 * cx + cy, c, (x, y, 1 - c)) for k, (cx, cy) in enumerate(chips)]
        for k, (cx, cy) in enumerate(chips):
            copy(k, 2 * cx + cy, c, (x, y, c)).wait_recv()
            passed[k].start()
        for k, (cx, cy) in enumerate(chips):
            copy(3 + k, 2 * cx + cy, 1 - c, (x, y, c)).wait_recv()
        for cp in first + passed:
            cp.wait_send()
        local.wait()

    return pl.pallas_call(
        body, name="all_gather_shards",
        in_specs=[HBM], out_specs=HBM,
        out_shape=jax.ShapeDtypeStruct((4, PACK_ROWS, D_MODEL), mine.dtype),
        scratch_shapes=[pltpu.SemaphoreType.DMA((6,)), pltpu.SemaphoreType.DMA((6,)), pltpu.SemaphoreType.DMA],
    )(mine)


def sibling_swap_halves(g):
    def body(g_ref, a_ref, send_sem, recv_sem):
        x, y, c, _ = _place()
        cp = pltpu.make_async_remote_copy(src_ref=g_ref.at[:, pl.ds((1 - c) * HALF, HALF)], dst_ref=a_ref,
                                          send_sem=send_sem, recv_sem=recv_sem, device_id=(x, y, 1 - c), device_id_type=MESH)
        cp.start()
        cp.wait()

    return pl.pallas_call(
        body, name="sibling_swap_halves",
        in_specs=[HBM], out_specs=HBM,
        out_shape=jax.ShapeDtypeStruct((4, HALF, D_MODEL), g.dtype),
        scratch_shapes=[pltpu.SemaphoreType.DMA, pltpu.SemaphoreType.DMA],
    )(g)


def chip_exchange(p):
    def body(p_ref, b_ref, send_sems, recv_sems, local_sem):
        x, y, c, chips = _place()
        me = 2 * x + y
        local = pltpu.make_async_copy(p_ref.at[me], b_ref.at[me], local_sem)
        local.start()
        sends = [pltpu.make_async_remote_copy(src_ref=p_ref.at[2 * cx + cy], dst_ref=b_ref.at[me],
                                              send_sem=send_sems.at[k], recv_sem=recv_sems.at[k],
                                              device_id=(cx, cy, c), device_id_type=MESH)
                 for k, (cx, cy) in enumerate(chips)]
        for cp in sends:
            cp.start()
        for k, (cx, cy) in enumerate(chips):
            pltpu.make_async_remote_copy(src_ref=p_ref.at[me], dst_ref=b_ref.at[2 * cx + cy],
                                         send_sem=send_sems.at[k], recv_sem=recv_sems.at[k],
                                         device_id=(cx, cy, c), device_id_type=MESH).wait_recv()
        for cp in sends:
            cp.wait_send()
        local.wait()

    return pl.pallas_call(
        body, name="chip_exchange",
        in_specs=[HBM], out_specs=HBM,
        out_shape=jax.ShapeDtypeStruct((4, HALF, D_MODEL), p.dtype),
        scratch_shapes=[pltpu.SemaphoreType.DMA((3,)), pltpu.SemaphoreType.DMA((3,)), pltpu.SemaphoreType.DMA],
    )(p)


def sibling_join_halves(mine):
    def body(h_ref, out_ref, send_sem, recv_sem, local_sem):
        x, y, c, _ = _place()
        local = pltpu.make_async_copy(h_ref, out_ref.at[pl.ds(c * HALF, HALF)], local_sem)
        local.start()
        cp = pltpu.make_async_remote_copy(src_ref=h_ref, dst_ref=out_ref.at[pl.ds(c * HALF, HALF)],
                                          send_sem=send_sem, recv_sem=recv_sem, device_id=(x, y, 1 - c), device_id_type=MESH)
        cp.start()
        cp.wait_send()
        pltpu.make_async_remote_copy(src_ref=h_ref, dst_ref=out_ref.at[pl.ds((1 - c) * HALF, HALF)],
                                     send_sem=send_sem, recv_sem=recv_sem, device_id=(x, y, 1 - c), device_id_type=MESH).wait_recv()
        local.wait()

    return pl.pallas_call(
        body, name="sibling_join_halves",
        in_specs=[HBM], out_specs=HBM,
        out_shape=jax.ShapeDtypeStruct((PACK_ROWS, D_MODEL), mine.dtype),
        scratch_shapes=[pltpu.SemaphoreType.DMA, pltpu.SemaphoreType.DMA, pltpu.SemaphoreType.DMA],
    )(mine)


def _sum_call(name, arrays, rows, out_dtype):
    tr = rows // 5 if rows % (5 * TILE_ROWS) == 0 else rows

    def body(*refs):
        acc = refs[0][...].astype(F32)
        for r in refs[1:-1]:
            acc = acc + r[...].astype(F32)
        refs[-1][...] = acc.astype(out_dtype)

    spec = pl.BlockSpec((tr, D_MODEL), lambda i: (i, 0))
    return pl.pallas_call(
        body, name=name, grid=(rows // tr,),
        in_specs=[spec] * len(arrays), out_specs=spec,
        out_shape=jax.ShapeDtypeStruct((rows, D_MODEL), out_dtype),
        compiler_params=_cparams(("parallel",), 40),
    )(*arrays)


def allreduce_rows(block):
    n_dev = 8

    def body(src_ref, out_ref, slots, send_sems, recv_sems):
        x, y, c = lax.axis_index("x"), lax.axis_index("y"), lax.axis_index("c")
        flip = lambda v, bit: 1 - v if bit else v
        index = lambda px, py, pc: 4 * px + 2 * py + pc
        slots[index(x, y, c)] = src_ref[...]
        peers = [(flip(x, k >> 2 & 1), flip(y, k >> 1 & 1), flip(c, k & 1)) for k in range(1, n_dev)]

        def copy(k, slot, peer):
            return pltpu.make_async_remote_copy(src_ref=src_ref, dst_ref=slots.at[slot], send_sem=send_sems.at[k],
                                                recv_sem=recv_sems.at[k], device_id=peer, device_id_type=MESH)

        sends = [copy(k, index(x, y, c), peer) for k, peer in enumerate(peers)]
        for cp in sends:
            cp.start()
        for k, peer in enumerate(peers):
            copy(k, index(*peer), peer).wait_recv()
        for cp in sends:
            cp.wait_send()
        acc = slots[0]
        for dev in range(1, n_dev):
            acc = acc + slots[dev]
        out_ref[...] = acc

    vmem = pl.BlockSpec(memory_space=pltpu.VMEM)
    return pl.pallas_call(
        body, name="allreduce_rows",
        in_specs=[vmem], out_specs=vmem,
        out_shape=jax.ShapeDtypeStruct(block.shape, F32),
        scratch_shapes=[pltpu.VMEM((n_dev,) + block.shape, F32), pltpu.SemaphoreType.DMA((n_dev - 1,)),
                        pltpu.SemaphoreType.DMA((n_dev - 1,))],
    )(block)


def adamw(w, g, m, v, name):
    rows, cols = w.shape
    tr = min(rows, TM)

    def body(w_ref, g_ref, m_ref, v_ref, d_ref, nm_ref, nv_ref):
        gt = g_ref[...]
        nm = ADAM_B1 * m_ref[...] + (1.0 - ADAM_B1) * gt
        nv = ADAM_B2 * v_ref[...] + (1.0 - ADAM_B2) * (gt * gt)
        m_hat = nm / (1.0 - ADAM_B1 ** ADAM_STEP)
        v_hat = nv / (1.0 - ADAM_B2 ** ADAM_STEP)
        d_ref[...] = -ADAM_LR * (m_hat / (jnp.sqrt(v_hat) + ADAM_EPS) + ADAM_WD * w_ref[...])
        nm_ref[...] = nm
        nv_ref[...] = nv

    spec = pl.BlockSpec((tr, cols), lambda i: (i, 0))
    shape = jax.ShapeDtypeStruct((rows, cols), F32)
    return pl.pallas_call(
        body, name=name, grid=(rows // tr,),
        in_specs=[spec] * 4, out_specs=[spec] * 3, out_shape=[shape] * 3,
        compiler_params=_cparams(("parallel",), 40),
    )(w, g, m, v)


def kernel(x, even_norm, even_w_in, even_b_f, even_q_gain, even_k_gain, even_w_out, odd_norm, odd_w_in, odd_q_gain, odd_k_gain, odd_w_out, loss_target, m_even_norm, m_even_w_in, m_even_b_f, m_even_q_gain, m_even_k_gain, m_even_w_out, m_odd_norm, m_odd_w_in, m_odd_q_gain, m_odd_k_gain, m_odd_w_out, v_even_norm, v_even_w_in, v_even_b_f, v_even_q_gain, v_even_k_gain, v_even_w_out, v_odd_norm, v_odd_w_in, v_odd_q_gain, v_odd_k_gain, v_odd_w_out):
    w = dict(even_norm=even_norm, even_w_in=even_w_in, even_b_f=even_b_f, even_q_gain=even_q_gain, even_k_gain=even_k_gain,
             even_w_out=even_w_out, odd_norm=odd_norm, odd_w_in=odd_w_in, odd_q_gain=odd_q_gain, odd_k_gain=odd_k_gain,
             odd_w_out=odd_w_out)
    m = dict(even_norm=m_even_norm, even_w_in=m_even_w_in, even_b_f=m_even_b_f, even_q_gain=m_even_q_gain,
             even_k_gain=m_even_k_gain, even_w_out=m_even_w_out, odd_norm=m_odd_norm, odd_w_in=m_odd_w_in,
             odd_q_gain=m_odd_q_gain, odd_k_gain=m_odd_k_gain, odd_w_out=m_odd_w_out)
    v = dict(even_norm=v_even_norm, even_w_in=v_even_w_in, even_b_f=v_even_b_f, even_q_gain=v_even_q_gain,
             even_k_gain=v_even_k_gain, even_w_out=v_even_w_out, odd_norm=v_odd_norm, odd_w_in=v_odd_w_in,
             odd_q_gain=v_odd_q_gain, odd_k_gain=v_odd_k_gain, odd_w_out=v_odd_w_out)
    spare = jnp.zeros((D_MODEL // 4,), F32)
    vector_names = [n for n, _ in VECTORS if n != "odd_norm_lo"]
    flat = lambda d: {**{n: d[n].reshape(-1) for n in vector_names}, "odd_norm_lo": spare}

    on = odd_norm.reshape(-1)
    on_hi = on.astype(BF16)
    wire = {n: t.astype(BF16) for n, t in flat(w).items()}
    wire.update(odd_norm=on_hi, odd_norm_lo=(on - on_hi.astype(F32)).astype(BF16),
                even_w_in_t=even_w_in[0].T.astype(BF16), even_w_out=even_w_out[0].astype(BF16),
                odd_w_in_t=odd_w_in[0].T.astype(BF16), odd_w_out=odd_w_out[0].astype(BF16))
    gathered = all_gather_shards(_pack(wire))
    sh = [_unpack(gathered[j]) for j in range(4)]
    cat = lambda n, axis: jnp.concatenate([t[n] for t in sh], axis=axis)
    gn_o = cat("odd_norm", 0).astype(F32) + cat("odd_norm_lo", 0).astype(F32)

    s = x.shape[1]
    loss_local, grad_x, g = local_step(
        x[0], loss_target[0], cat("even_w_in_t", 0), even_b_f[0], even_q_gain[0], even_k_gain[0], even_norm[0],
        cat("even_w_out", 0), gn_o, cat("odd_w_in_t", 0), odd_q_gain[0], odd_k_gain[0], cat("odd_w_out", 1),
        (min(1024, s), min(512, s)), (min(1024, s), min(256, s)))
    loss = lax.psum(loss_local, ("x", "y", "c"))

    g_vec = _vectors_of(allreduce_rows(_vector_block(g, GRAD_VECTORS)), GRAD_VECTORS)
    chip = 2 * lax.axis_index("x") + lax.axis_index("y")
    g_vec["odd_norm"] = lax.dynamic_slice_in_dim(g_vec["odd_norm"], chip * (D_MODEL // 4), D_MODEL // 4)

    zero_vectors = {n: jnp.zeros((size,), F32) for n, size in VECTORS}

    def grad_parts(j):
        parts = dict(zero_vectors)
        parts["even_w_in_t"] = g["even_w_in_t"][j * (EVEN_IN // 4):(j + 1) * (EVEN_IN // 4)]
        parts["even_w_out"] = g["even_w_out"][j * (EVEN_W // 4):(j + 1) * (EVEN_W // 4)]
        parts["odd_w_in_t"] = g["odd_w_in_t"][j * (ODD_IN // 4):(j + 1) * (ODD_IN // 4)]
        parts["odd_w_out"] = g["odd_w_out"][:, j * (D_MODEL // 4):(j + 1) * (D_MODEL // 4)]
        return parts

    g_all = jnp.stack([_pack(grad_parts(j)) for j in range(4)])
    from_sibling = sibling_swap_halves(g_all)
    c = lax.axis_index("c")
    g_mine = lax.dynamic_slice_in_dim(g_all, c * HALF, HALF, axis=1)
    pair = _sum_call("pair_sum", [g_mine.reshape(4 * HALF, D_MODEL), from_sibling.reshape(4 * HALF, D_MODEL)], 4 * HALF, BF16)
    by_chip = chip_exchange(pair.reshape(4, HALF, D_MODEL))
    half_sum = _sum_call("chip_sum", [by_chip[0], by_chip[1], by_chip[2], by_chip[3]], HALF, F32)
    g_buf = sibling_join_halves(half_sum)

    gp = _unpack(g_buf)
    grad = dict(g_vec)
    grad.update(even_w_in=gp["even_w_in_t"].T, even_w_out=gp["even_w_out"], odd_w_in=gp["odd_w_in_t"].T, odd_w_out=gp["odd_w_out"])
    results = [grad, {}, {}, {}]
    for n in ("even_w_in", "even_w_out", "odd_w_in", "odd_w_out"):
        stepped = adamw(w[n][0], grad[n], m[n][0], v[n][0], "adamw_" + n)
        for res, t in zip(results[1:], stepped):
            res[n] = t
    block = lambda d: _vector_block(flat(d))
    stepped = adamw(block(w), _vector_block({**grad, "odd_norm_lo": spare}), block(m), block(v), "adamw_vectors")
    for res, t in zip(results[1:], stepped):
        res.update(_vectors_of(t))
    outs = [loss.reshape(()), grad_x.reshape(x.shape)]
    order = ["even_norm", "even_w_in", "even_b_f", "even_q_gain", "even_k_gain", "even_w_out", "odd_norm", "odd_w_in",
             "odd_q_gain", "odd_k_gain", "odd_w_out"]
    for res in results:
        outs += [res[n].reshape(w[n].shape) for n in order]
    return tuple(outs)
```

```python
import jax
import jax.numpy as jnp
from jax import lax
from jax.experimental import pallas as pl
from jax.experimental.pallas import tpu as pltpu

F32 = jnp.float32
BF16 = jnp.bfloat16

D_MODEL = 1024
HD = 128
NH = 4
HD2 = 64
NG = 3
NH2 = 8
DILATIONS = (1, 4, 16)
SPAN = 128
EVEN_W = 1024
ODD_W = 512
EVEN_IN = 4100
EVEN_PACK = 4224
FL_OFF = 1536
ODD_IN = 5120
RMS_EPS = 1e-6
SCALE_E = HD ** -0.5
SCALE_O = HD2 ** -0.5
ADAM_LR, ADAM_B1, ADAM_B2, ADAM_EPS, ADAM_WD, ADAM_STEP = 0.001, 0.9, 0.999, 1e-08, 0.01, 10

VMEM_CAP = 64 * 1024 * 1024
LANES = 128
MESH = pl.DeviceIdType.MESH


def _cparams(sem, vmem_mb):
    return pltpu.CompilerParams(dimension_semantics=sem, vmem_limit_bytes=min(vmem_mb << 20, VMEM_CAP - (6 << 20)))


def _silu(g):
    return g / (1.0 + jnp.exp(-g))


def _dsilu(g):
    s = 1.0 / (1.0 + jnp.exp(-g))
    return s * (1.0 + g * (1.0 - s))


def _split2(x):
    hi = x.astype(BF16)
    lo = (x - hi.astype(F32)).astype(BF16)
    return hi, lo


def _split3(x):
    hi = x.astype(BF16)
    r = x - hi.astype(F32)
    mid = r.astype(BF16)
    lo = (r - mid.astype(F32)).astype(BF16)
    return hi, mid, lo


def _dotf(a, b):
    return jnp.dot(a, b, preferred_element_type=F32)


def _dot_nt(a, b):
    return lax.dot_general(a, b, (((1,), (1,)), ((), ())), preferred_element_type=F32)


def _dot_tn(a, b):
    return lax.dot_general(a, b, (((0,), (0,)), ((), ())), preferred_element_type=F32)


def _segsum(x, hd):
    r = lax.broadcasted_iota(jnp.int32, (LANES, LANES), 0) // hd
    c = lax.broadcasted_iota(jnp.int32, (LANES, LANES), 1) // hd
    ones = (r == c).astype(BF16)
    ones2 = jnp.concatenate([ones, ones], axis=0)
    outs = []
    for ch in range(x.shape[1] // LANES):
        hi, lo = _split2(x[:, ch * LANES:(ch + 1) * LANES])
        outs.append(_dotf(jnp.concatenate([hi, lo], axis=1), ones2))
    return outs[0] if len(outs) == 1 else jnp.concatenate(outs, axis=1)


def _suffix_matrix(n):
    r = lax.broadcasted_iota(jnp.int32, (n, n), 0)
    c = lax.broadcasted_iota(jnp.int32, (n, n), 1)
    return (r >= c).astype(BF16)


def _walk_up_staged(i, per, stages):
    assert per % 2 == 0
    n_full = i * per

    def trip(states, masked):
        for stage in stages:
            for st in states:
                stage(st, masked)

    def full_trip(t, c):
        trip([{"j": 2 * t + u, "band": 0} for u in range(2)], False)
        return c

    lax.fori_loop(0, n_full // 2, full_trip, 0)
    for first in range(0, per, 2):
        trip([{"j": n_full + b, "band": b} for b in (first, first + 1)], True)


def fox_fwd(qs, kn, v, nc, bq, bk):
    nh, s, _ = qs.shape
    nq = s // bq
    per = bq // bk

    def body(q_ref, k_ref, v_ref, nc_ref, o_ref, lse_ref):
        i = pl.program_id(1)
        q = q_ref[...]

        def logits(j, masked):
            off = pl.multiple_of(j * bk, bk)
            sc = _dot_nt(q, k_ref[pl.ds(off, bk), :]) + nc_ref[:, pl.ds(off, bk)]
            if masked:
                row = i * bq + lax.broadcasted_iota(jnp.int32, (bq, bk), 0)
                col = off + lax.broadcasted_iota(jnp.int32, (bq, bk), 1)
                sc = jnp.where(col <= row, sc, -jnp.inf)
            return off, sc

        def update(carry, off, sc):
            m, l, acc = carry
            m_new = jnp.maximum(m, jnp.max(sc, axis=1, keepdims=True))
            alpha = jnp.exp2(m - m_new)
            p = jnp.exp2(sc - m_new)
            l = alpha * l + jnp.sum(p, axis=1, keepdims=True)
            acc = alpha * acc + _dotf(p.astype(BF16), v_ref[pl.ds(off, bk), :])
            return m_new, l, acc

        def pair(first, carry, masked):
            a, b = logits(first, masked), logits(first + 1, masked)
            return update(update(carry, *a), *b)

        assert per % 2 == 0
        n_full = i * per
        init = (jnp.full((bq, 1), -jnp.inf, F32), jnp.zeros((bq, 1), F32), jnp.zeros((bq, HD), F32))
        carry = lax.fori_loop(0, n_full // 2, lambda t, c: pair(2 * t, c, False), init)
        m, l, acc = lax.fori_loop(0, per // 2, lambda t, c: pair(n_full + 2 * t, c, True), carry)
        o_ref[...] = (acc / l).astype(o_ref.dtype)
        lse_ref[...] = m + jnp.log2(l)

    return pl.pallas_call(
        body, name="fox_fwd",
        grid=(nh, nq),
        in_specs=[pl.BlockSpec((None, bq, HD), lambda h, i: (h, i, 0)),
                  pl.BlockSpec((None, s, HD), lambda h, i: (h, 0, 0)),
                  pl.BlockSpec((None, s, HD), lambda h, i: (h, 0, 0)),
                  pl.BlockSpec((None, 1, s), lambda h, i: (h, 0, 0))],
        out_specs=[pl.BlockSpec((None, bq, HD), lambda h, i: (h, i, 0)),
                   pl.BlockSpec((None, bq, 1), lambda h, i: (h, i, 0))],
        out_shape=[jax.ShapeDtypeStruct((nh, s, HD), BF16), jax.ShapeDtypeStruct((nh, s, 1), F32)],
        compiler_params=_cparams(("arbitrary", "arbitrary"), 40),
    )(qs, kn, v, nc)


LOG2E = 1.4426950408889634
LN2 = 0.6931471805599453


def _neg_abs(z):
    sign = jnp.uint32(0x80000000)
    return lax.bitcast_convert_type(lax.bitcast_convert_type(z, jnp.uint32) | sign, F32)


def _sb_softplus2(z, row0, col0, masked):
    u = jnp.maximum(z, 0.0) + jnp.log2(1.0 + jnp.exp2(_neg_abs(z)))
    strict = None
    if masked:
        row = row0 + lax.broadcasted_iota(jnp.int32, z.shape, 0)
        col = col0 + lax.broadcasted_iota(jnp.int32, z.shape, 1)
        strict = col < row
        u = jnp.where(strict, u, 0.0)
    return u, strict


def _walk_down_staged(i, per, stages, group=2, unroll_band=True):
    assert per % group == 0
    n_full = i * per

    def trip(states, masked):
        for stage in stages:
            for st in states:
                stage(st, masked)

    def tiles_down(top, band):
        return [{"j": top - u, "band": 0 if band is None else band - u} for u in range(group)]

    def band_trip(t, c):
        trip(tiles_down(n_full + per - 1 - group * t, None), True)
        return c

    def full_trip(t, c):
        trip(tiles_down(n_full - 1 - group * t, None), False)
        return c

    if unroll_band:
        for first in range(per - 1, -1, -group):
            trip(tiles_down(n_full + first, first), True)
    else:
        lax.fori_loop(0, per // group, band_trip, 0)
    lax.fori_loop(0, n_full // group, full_trip, 0)


def _suffix2(x, m2):
    hi, lo = _split2(x)
    return _dotf(jnp.concatenate([hi, lo], axis=1), m2)


def _lanes(col, n):
    return jnp.broadcast_to(col, (col.shape[0], n))


def sb_fwd(qs, k, v, bq, bk):
    nh, s, _ = qs.shape
    nq = s // bq
    per = bq // bk

    def body(q_ref, k_ref, v_ref, o_ref, lrun_ref):
        i = pl.program_id(1)
        q = q_ref[...]
        tri = _suffix_matrix(bk)
        o_ref[...] = jnp.zeros_like(o_ref)
        lrun_ref[...] = jnp.zeros_like(lrun_ref)

        def logits(st, masked):
            st["off"] = pl.multiple_of(st["j"] * bk, bk)
            st["r0"] = st["band"] * bk
            st["z"] = _dot_nt(q[st["r0"]:], k_ref[pl.ds(st["off"], bk), :])

        def suffix(st, masked):
            u, st["strict"] = _sb_softplus2(st["z"], i * bq + st["r0"], st["off"], masked)
            st["incl"] = _dotf(u.astype(BF16), tri)

        def weigh(st, masked):
            rows = slice(st["r0"], bq)
            lrun = lrun_ref[rows, :]
            w = jnp.exp2(st["z"] - st["incl"] + jnp.tile(lrun, (1, bk // LANES)))
            if masked:
                w = jnp.where(st["strict"], w, 0.0)
            o_ref[rows, :] += _dotf(w.astype(BF16), v_ref[pl.ds(st["off"], bk), :])
            lrun_ref[rows, :] = lrun - _lanes(st["incl"][:, 0:1], LANES)

        _walk_down_staged(i, per, [logits, suffix, weigh], group=4 if per % 4 == 0 else 2)

    return pl.pallas_call(
        body, name="sb_fwd",
        grid=(nh, nq),
        in_specs=[pl.BlockSpec((None, bq, HD), lambda h, i: (h, i, 0)),
                  pl.BlockSpec((None, s, HD), lambda h, i: (h, 0, 0)),
                  pl.BlockSpec((None, s, HD), lambda h, i: (h, 0, 0))],
        out_specs=pl.BlockSpec((None, bq, HD), lambda h, i: (h, i, 0)),
        out_shape=jax.ShapeDtypeStruct((nh, s, HD), F32),
        scratch_shapes=[pltpu.VMEM((bq, LANES), F32)],
        compiler_params=_cparams(("arbitrary", "arbitrary"), 40),
    )(qs, k, v)


def _attn_bwd_call(name, body, s, bq, ins, in_specs, extra_out_specs, extra_out_shapes, extra_scratch=()):
    nh = NH
    nq = s // bq
    return pl.pallas_call(
        body, name=name,
        grid=(nh, nq),
        in_specs=in_specs,
        out_specs=[pl.BlockSpec((None, bq, HD), lambda h, i: (h, i, 0)),
                   pl.BlockSpec(memory_space=pl.ANY), pl.BlockSpec(memory_space=pl.ANY)] + extra_out_specs,
        out_shape=[jax.ShapeDtypeStruct((nh, s, HD), F32), jax.ShapeDtypeStruct((nh, s, HD), F32),
                   jax.ShapeDtypeStruct((nh, s, HD), F32)] + extra_out_shapes,
        scratch_shapes=[pltpu.VMEM((s, HD), F32), pltpu.VMEM((s, HD), F32), pltpu.SemaphoreType.DMA((2,))] + list(extra_scratch),
        compiler_params=_cparams(("arbitrary", "arbitrary"), 52),
    )(*ins)


def _flush_dkv(i, nq, h, dk_acc, dv_acc, dk_hbm, dv_hbm, sems):
    @pl.when(i == nq - 1)
    def _():
        ck = pltpu.make_async_copy(dk_acc, dk_hbm.at[h], sems.at[0])
        cv = pltpu.make_async_copy(dv_acc, dv_hbm.at[h], sems.at[1])
        ck.start()
        cv.start()
        ck.wait()
        cv.wait()


def fox_bwd(qs, kn, v, nc, lse, do, delta, bq, bk):
    nh, s, _ = qs.shape
    nq = s // bq
    per = bq // bk

    def body(q_ref, k_ref, v_ref, nc_ref, lse_ref, do_ref, dl_ref, dq_ref, dk_hbm, dv_hbm, dnc_ref, drow_ref, dk_acc, dv_acc, sems):
        h, i = pl.program_id(0), pl.program_id(1)

        @pl.when(i == 0)
        def _():
            dk_acc[...] = jnp.zeros_like(dk_acc)
            dv_acc[...] = jnp.zeros_like(dv_acc)
            dnc_ref[...] = jnp.zeros_like(dnc_ref)

        q = q_ref[...]
        do_t = do_ref[...]
        lse_t = lse_ref[...]
        dl_t = dl_ref[...]

        dq_ref[...] = jnp.zeros_like(dq_ref)
        drow_ref[...] = jnp.zeros_like(drow_ref)

        def logits(st, masked):
            st["off"] = pl.multiple_of(st["j"] * bk, bk)
            st["r0"] = st["band"] * bk
            st["sc"] = _dot_nt(q[st["r0"]:], k_ref[pl.ds(st["off"], bk), :]) + nc_ref[:, pl.ds(st["off"], bk)]
            st["dp"] = _dot_nt(do_t[st["r0"]:], v_ref[pl.ds(st["off"], bk), :])

        def grads(st, masked):
            off, r0 = st["off"], st["r0"]
            rows = slice(r0, bq)
            p = jnp.exp2(st["sc"] - lse_t[rows])
            if masked:
                row = i * bq + r0 + lax.broadcasted_iota(jnp.int32, p.shape, 0)
                col = off + lax.broadcasted_iota(jnp.int32, p.shape, 1)
                p = jnp.where(col <= row, p, 0.0)
            ds = p * (st["dp"] - dl_t[rows])
            dsb = ds.astype(BF16)
            dq_ref[rows, :] += _dotf(dsb, k_ref[pl.ds(off, bk), :])
            dk_acc[pl.ds(off, bk), :] += _dot_tn(dsb, q[rows])
            dv_acc[pl.ds(off, bk), :] += _dot_tn(p.astype(BF16), do_t[rows])
            dnc_ref[:, pl.ds(off, bk)] += jnp.sum(ds, axis=0, keepdims=True)
            drow_ref[rows, :] += jnp.sum(ds, axis=1, keepdims=True)

        _walk_up_staged(i, per, [logits, grads])
        _flush_dkv(i, nq, h, dk_acc, dv_acc, dk_hbm, dv_hbm, sems)

    tile_spec = pl.BlockSpec((None, bq, HD), lambda h, i: (h, i, 0))
    col_spec = pl.BlockSpec((None, bq, 1), lambda h, i: (h, i, 0))
    full_spec = pl.BlockSpec((None, s, HD), lambda h, i: (h, 0, 0))
    row_spec = pl.BlockSpec((None, 1, s), lambda h, i: (h, 0, 0))
    return _attn_bwd_call("fox_bwd", body, s, bq, (qs, kn, v, nc, lse, do, delta),
                          [tile_spec, full_spec, full_spec, row_spec, col_spec, tile_spec, col_spec],
                          [row_spec, col_spec],
                          [jax.ShapeDtypeStruct((nh, 1, s), F32), jax.ShapeDtypeStruct((nh, s, 1), F32)])


def sb_bwd(qs, k, v, do, delta, bq, bk):
    nh, s, _ = qs.shape
    nq = s // bq
    per = bq // bk

    def body(q_ref, k_ref, v_ref, do_ref, dl_ref, dq_ref, dk_hbm, dv_hbm, dk_acc, dv_acc, sems, lrun_ref, crun_ref):
        h, i = pl.program_id(0), pl.program_id(1)

        @pl.when(i == 0)
        def _():
            dk_acc[...] = jnp.zeros_like(dk_acc)
            dv_acc[...] = jnp.zeros_like(dv_acc)

        q = q_ref[...]
        do_t = do_ref[...]
        tri = _suffix_matrix(bk)
        tri2 = jnp.concatenate([tri, tri], axis=0)
        dq_ref[...] = jnp.zeros_like(dq_ref)
        lrun_ref[...] = jnp.zeros_like(lrun_ref)
        crun_ref[...] = _lanes(dl_ref[...], LANES)

        def logits(st, masked):
            st["off"] = pl.multiple_of(st["j"] * bk, bk)
            st["rows"] = slice(st["band"] * bk, bq)
            st["z"] = _dot_nt(q[st["rows"]], k_ref[pl.ds(st["off"], bk), :])
            st["dw"] = _dot_nt(do_t[st["rows"]], v_ref[pl.ds(st["off"], bk), :])

        def suffix(st, masked):
            st["u"], st["strict"] = _sb_softplus2(st["z"], i * bq + st["rows"].start, st["off"], masked)
            st["incl"] = _dotf(st["u"].astype(BF16), tri)

        def weigh(st, masked):
            lrun = lrun_ref[st["rows"], :]
            w = jnp.exp2(st["z"] - st["incl"] + jnp.tile(lrun, (1, bk // LANES)))
            if masked:
                w = jnp.where(st["strict"], w, 0.0)
            st["wb"] = w.astype(BF16)
            st["e"] = st["dw"] * st["wb"].astype(F32)
            st["einc"] = _suffix2(st["e"], tri2)
            lrun_ref[st["rows"], :] = lrun - _lanes(st["incl"][:, 0:1], LANES)

        def grads(st, masked):
            rows = st["rows"]
            crun = crun_ref[rows, :]
            prefix = jnp.tile(crun, (1, bk // LANES)) - st["einc"]
            dz = st["e"] - jnp.exp2(st["z"] - st["u"]) * (st["e"] + prefix)
            if masked:
                dz = jnp.where(st["strict"], dz, 0.0)
            dzb = dz.astype(BF16)
            dq_ref[rows, :] += _dotf(dzb, k_ref[pl.ds(st["off"], bk), :])
            dk_acc[pl.ds(st["off"], bk), :] += _dot_tn(dzb, q[rows])
            dv_acc[pl.ds(st["off"], bk), :] += _dot_tn(st["wb"], do_t[rows])
            crun_ref[rows, :] = crun - _lanes(st["einc"][:, 0:1], LANES)

        _walk_down_staged(i, per, [logits, suffix, weigh, grads], unroll_band=False)
        _flush_dkv(i, nq, h, dk_acc, dv_acc, dk_hbm, dv_hbm, sems)

    tile_spec = pl.BlockSpec((None, bq, HD), lambda h, i: (h, i, 0))
    col_spec = pl.BlockSpec((None, bq, 1), lambda h, i: (h, i, 0))
    full_spec = pl.BlockSpec((None, s, HD), lambda h, i: (h, 0, 0))
    return _attn_bwd_call("sb_bwd", body, s, bq, (qs, k, v, do, delta),
                          [tile_spec, full_spec, full_spec, tile_spec, col_spec], [], [],
                          [pltpu.VMEM((bq, LANES), F32), pltpu.VMEM((bq, LANES), F32)])


BI = SPAN
PAIR = 2 * HD2


def _slopes(g):
    return [float(2.0 ** (-8.0 * (g * NH2 + h + 1) / (NG * NH2))) for h in range(NH2)]


def _head_lanes(hh):
    return (lax.broadcasted_iota(jnp.int32, (1, PAIR), 1) // HD2) == hh


def _dil_masks(n, d):
    a = lax.broadcasted_iota(jnp.int32, (BI, 2 * BI), 0)
    c = lax.broadcasted_iota(jnp.int32, (BI, 2 * BI), 1)
    dist = a - c + BI
    valid = (dist >= 0) & (dist <= SPAN) & ((c >= BI) | (n > 0))
    return valid, (dist * d).astype(F32)


def _dil_pair_fwd(qp, kcat, vcat, valid, distf, slopes2):
    o_pair = jnp.zeros((BI, PAIR), F32)
    lse_pair = jnp.zeros((BI, PAIR), F32)
    for hh in range(2):
        lm = _head_lanes(hh)
        qm = jnp.where(lm, qp, jnp.zeros_like(qp))
        logits = jnp.where(valid, _dot_nt(qm, kcat) - slopes2[hh] * distf, -jnp.inf)
        m = jnp.max(logits, axis=1, keepdims=True)
        p = jnp.exp(logits - m)
        den = jnp.sum(p, axis=1, keepdims=True)
        o_pair = jnp.where(lm, _dotf(p.astype(BF16), vcat) / den, o_pair)
        lse_pair = jnp.where(lm, m + jnp.log(den), lse_pair)
    return o_pair, lse_pair


def _dil_pair_bwd(qc, kcat, vcat, doc, lse_c, dl_c, valid, distf, slopes2):
    dq_pair = jnp.zeros((BI, PAIR), F32)
    dk_cat = jnp.zeros((2 * BI, PAIR), F32)
    dv_cat = jnp.zeros((2 * BI, PAIR), F32)
    for hh in range(2):
        col = slice(hh * HD2, hh * HD2 + 1)
        lm = _head_lanes(hh)
        zq = jnp.zeros_like(qc)
        qm = jnp.where(lm, qc, zq)
        dom = jnp.where(lm, doc, zq)
        logits = _dot_nt(qm, kcat) - slopes2[hh] * distf
        p = jnp.exp(jnp.where(valid, logits, -jnp.inf) - lse_c[:, col])
        ds = (p * (_dot_nt(dom, vcat) - dl_c[:, col])).astype(BF16)
        dq_pair = jnp.where(lm, _dotf(ds, kcat), dq_pair)
        dk_cat = dk_cat + _dot_tn(ds, qm)
        dv_cat = dv_cat + _dot_tn(p.astype(BF16), dom)
    return dq_pair, dk_cat, dv_cat


def _pair_slopes(slopes, hp):
    out = []
    for hh in range(2):
        acc = jnp.float32(slopes[hh])
        for t in range(1, NH2 // 2):
            acc = jnp.where(hp == t, jnp.float32(slopes[2 * t + hh]), acc)
        out.append(acc)
    return out


def _for_residues(d, residue):
    per_trip = min(d, 4)

    def trip(t, carry):
        for u in range(per_trip):
            residue(t * per_trip + u)
        return carry

    if d == per_trip:
        trip(0, 0)
    else:
        lax.fori_loop(0, d // per_trip, trip, 0)


def _dil_tiling(d):
    return NH2 // 2 if d == 1 else 1


def dil_fwd(q, k, v, g):
    d = DILATIONS[g]
    s = q.shape[1]
    rows_per = BI * d
    nblk = s // rows_per
    tiles = _dil_tiling(d)
    slopes = _slopes(g)

    def body(q_ref, kc_ref, kp_ref, vc_ref, vp_ref, o_ref, lse_ref):
        n = pl.program_id(0)
        valid, distf = _dil_masks(n, d)
        for t in range(tiles):
            sl = slice(t * PAIR, (t + 1) * PAIR)
            slopes2 = _pair_slopes(slopes, pl.program_id(1) * tiles + t)

            def residue(r):
                rows = pl.ds(r, BI, stride=d)
                kcat = jnp.concatenate([kp_ref[rows, sl], kc_ref[rows, sl]], axis=0).astype(BF16)
                vcat = jnp.concatenate([vp_ref[rows, sl], vc_ref[rows, sl]], axis=0).astype(BF16)
                o_ref[rows, sl], lse_ref[rows, sl] = _dil_pair_fwd(q_ref[rows, sl].astype(BF16), kcat, vcat, valid, distf, slopes2)

            _for_residues(d, residue)

    width = tiles * PAIR
    cur = pl.BlockSpec((None, rows_per, width), lambda n, hp: (g, n, hp))
    prev = pl.BlockSpec((None, rows_per, width), lambda n, hp: (g, jnp.maximum(n - 1, 0), hp))
    out = pl.BlockSpec((rows_per, width), lambda n, hp: (n, hp))
    return pl.pallas_call(
        body, name=f"dil_fwd_{g}",
        grid=(nblk, ODD_W // width),
        in_specs=[cur, cur, prev, cur, prev],
        out_specs=[out, out],
        out_shape=[jax.ShapeDtypeStruct((s, ODD_W), F32), jax.ShapeDtypeStruct((s, ODD_W), F32)],
        compiler_params=_cparams(("parallel", "parallel"), 40),
    )(q, k, k, v, v)


def dil_bwd(q, k, v, do, lse, delta, g):
    d = DILATIONS[g]
    s = q.shape[1]
    rows_per = BI * d
    nblk = s // rows_per
    tiles = _dil_tiling(d)
    slopes = _slopes(g)

    def body(q_ref, kc_ref, kp_ref, vc_ref, vp_ref, do_ref, l_ref, d_ref, dq_ref, dk_ref, dv_ref, dk_carry, dv_carry):
        first_tile, n = pl.program_id(0) * tiles, pl.program_id(1)

        @pl.when(n == 0)
        def _():
            dk_carry[...] = jnp.zeros_like(dk_carry)
            dv_carry[...] = jnp.zeros_like(dv_carry)

        @pl.when(n < nblk)
        def _():
            valid, distf = _dil_masks(n, d)
            for t in range(tiles):
                sl = slice(t * PAIR, (t + 1) * PAIR)
                slopes2 = _pair_slopes(slopes, first_tile + t)

                def residue(r):
                    rows = pl.ds(r, BI, stride=d)
                    kcat = jnp.concatenate([kp_ref[rows, sl], kc_ref[rows, sl]], axis=0).astype(BF16)
                    vcat = jnp.concatenate([vp_ref[rows, sl], vc_ref[rows, sl]], axis=0).astype(BF16)
                    dq, dk_cat, dv_cat = _dil_pair_bwd(q_ref[rows, sl].astype(BF16), kcat, vcat, do_ref[rows, sl].astype(BF16),
                                                       l_ref[rows, sl], d_ref[rows, sl], valid, distf, slopes2)
                    dq_ref[rows, sl] = dq
                    dk_ref[rows, sl] = dk_carry[rows, sl] + dk_cat[:BI]
                    dv_ref[rows, sl] = dv_carry[rows, sl] + dv_cat[:BI]
                    dk_carry[rows, sl] = dk_cat[BI:]
                    dv_carry[rows, sl] = dv_cat[BI:]

                _for_residues(d, residue)

        @pl.when(n == nblk)
        def _():
            dk_ref[...] = dk_carry[...]
            dv_ref[...] = dv_carry[...]

    width = tiles * PAIR
    cur_idx = lambda n: jnp.minimum(n, nblk - 1)
    prv_idx = lambda n: jnp.maximum(n - 1, 0)
    cur3 = pl.BlockSpec((None, rows_per, width), lambda hp, n: (g, cur_idx(n), hp))
    prv3 = pl.BlockSpec((None, rows_per, width), lambda hp, n: (g, prv_idx(n), hp))
    cur2 = pl.BlockSpec((rows_per, width), lambda hp, n: (cur_idx(n), hp))
    prv2 = pl.BlockSpec((rows_per, width), lambda hp, n: (prv_idx(n), hp))
    shape = jax.ShapeDtypeStruct((s, ODD_W), F32)
    return pl.pallas_call(
        body, name=f"dil_bwd_{g}",
        grid=(ODD_W // width, nblk + 1),
        in_specs=[cur3, cur3, prv3, cur3, prv3, cur2, cur2, cur2],
        out_specs=[cur2, prv2, prv2],
        out_shape=[shape, shape, shape],
        scratch_shapes=[pltpu.VMEM((rows_per, width), F32), pltpu.VMEM((rows_per, width), F32)],
        compiler_params=_cparams(("parallel", "arbitrary"), 48),
    )(q, k, k, v, v, do, lse, delta)


TM = 256


def _rows(tm, w):
    return pl.BlockSpec((tm, w), lambda i: (i, 0))


def _whole(shape):
    return pl.BlockSpec(shape, lambda i: (0,) * len(shape))


def _heads(tm):
    return pl.BlockSpec((NH, tm, HD), lambda i: (0, i, 0))


def _groups(tm):
    return pl.BlockSpec((NG, tm, ODD_W), lambda i: (0, i, 0))


def _rms(x):
    return lax.rsqrt(jnp.mean(x * x, axis=1, keepdims=True) + RMS_EPS)


def _seg_rms(q, hd):
    return lax.rsqrt(_segsum(q * q, hd) * (1.0 / hd) + RMS_EPS)


def _seg_rms_bwd(q_raw, dqs, gain, scale, hd):
    q = q_raw.astype(F32)
    r = _seg_rms(q, hd)
    qhat = q * r
    u = dqs * (gain * scale)
    dq = r * (u - qhat * (_segsum(u * qhat, hd) * (1.0 / hd)))
    return dq, jnp.sum(dqs * qhat, axis=0, keepdims=True) * scale


def _rms_bwd(x, dh, gain):
    r = _rms(x)
    xhat = x * r
    u = dh * gain
    dx = r * (u - xhat * jnp.mean(u * xhat, axis=1, keepdims=True))
    return dx, jnp.sum(dh * xhat, axis=0, keepdims=True)


def even_in_fwd(x, gnorm, w_pack, bf_pad, gq, gk):
    s = x.shape[0]

    def body(x_ref, g_ref, w_ref, bf_ref, gq_ref, gk_ref,
             h_ref, fqs_ref, fkn_ref, fv_ref, fqr_ref, fkr_ref, flog_ref, sqs_ref, sk_ref, sv_ref, gate_ref):
        xt = x_ref[...]
        h = (xt * _rms(xt) * g_ref[...]).astype(BF16)
        h_ref[...] = h
        proj = _dotf(h, w_ref[...])
        fq = proj[:, 0:512]
        fk = proj[:, 512:1024]
        fqs = fq * _seg_rms(fq, HD) * (gq_ref[...] * (SCALE_E * LOG2E))
        fkn = fk * _seg_rms(fk, HD) * gk_ref[...]
        flog_ref[...] = proj[:, FL_OFF:FL_OFF + LANES] + bf_ref[...]
        o = FL_OFF + LANES
        for hh in range(NH):
            sl = slice(hh * HD, (hh + 1) * HD)
            fqs_ref[hh] = fqs[:, sl].astype(BF16)
            fkn_ref[hh] = fkn[:, sl].astype(BF16)
            fqr_ref[hh] = fq[:, sl].astype(BF16)
            fkr_ref[hh] = fk[:, sl].astype(BF16)
            fv_ref[hh] = proj[:, 1024 + hh * HD:1024 + (hh + 1) * HD].astype(BF16)
            sqs_ref[hh] = (proj[:, o + hh * HD:o + (hh + 1) * HD] * (SCALE_E * LOG2E)).astype(BF16)
            sk_ref[hh] = proj[:, o + 512 + hh * HD:o + 512 + (hh + 1) * HD].astype(BF16)
            sv_ref[hh] = proj[:, o + 1024 + hh * HD:o + 1024 + (hh + 1) * HD].astype(BF16)
        gate_ref[...] = proj[:, o + 1536:o + 2560].astype(BF16)

    hs = jax.ShapeDtypeStruct((NH, s, HD), BF16)
    return pl.pallas_call(
        body, name="even_in_fwd",
        grid=(s // TM,),
        in_specs=[_rows(TM, D_MODEL), _whole((1, D_MODEL)), _whole((D_MODEL, EVEN_PACK)), _whole((1, LANES)),
                  _whole((1, 512)), _whole((1, 512))],
        out_specs=[_rows(TM, D_MODEL)] + [_heads(TM)] * 5 + [_rows(TM, LANES)] + [_heads(TM)] * 3 + [_rows(TM, EVEN_W)],
        out_shape=[jax.ShapeDtypeStruct((s, D_MODEL), BF16)] + [hs] * 5 + [jax.ShapeDtypeStruct((s, LANES), F32)]
        + [hs] * 3 + [jax.ShapeDtypeStruct((s, EVEN_W), BF16)],
        compiler_params=_cparams(("parallel",), 52),
    )(x, gnorm, w_pack, bf_pad, gq, gk)


def _prefix_matrices(r):
    a = lax.broadcasted_iota(jnp.int32, (LANES, LANES), 0)
    b = lax.broadcasted_iota(jnp.int32, (LANES, LANES), 1)
    ra = lax.broadcasted_iota(jnp.int32, (r, r), 0)
    rb = lax.broadcasted_iota(jnp.int32, (r, r), 1)
    return a, b, ra, rb


def _dot3_right(x, m):
    a, b, c = _split3(x)
    return _dotf(a, m) + _dotf(b, m) + _dotf(c, m)


def _dot3_left(m, x):
    a, b, c = _split3(x)
    return _dotf(m, a) + _dotf(m, b) + _dotf(m, c)


def fox_cum(flog4):
    nh, r, _ = flog4.shape

    def body(f_ref, nc_ref):
        z = f_ref[...]
        lf = jnp.minimum(z, 0.0) - jnp.log(1.0 + jnp.exp(-jnp.abs(z)))
        a, b, ra, rb = _prefix_matrices(r)
        within = _dot3_right(lf, (a <= b).astype(BF16))
        tot = jnp.broadcast_to(within[:, LANES - 1:LANES], (r, LANES))
        nc_ref[...] = (within + _dot3_left((rb < ra).astype(BF16), tot)) * (-LOG2E)

    return pl.pallas_call(
        body, name="fox_cum", grid=(nh,),
        in_specs=[pl.BlockSpec((None, r, LANES), lambda h: (h, 0, 0))],
        out_specs=pl.BlockSpec((None, r, LANES), lambda h: (h, 0, 0)),
        out_shape=jax.ShapeDtypeStruct((nh, r, LANES), F32),
        compiler_params=_cparams(("parallel",), 16),
    )(flog4)


def fox_cum_bwd(dcum4, flog4):
    nh, r, _ = flog4.shape

    def body(d_ref, f_ref, o_ref):
        a, b, ra, rb = _prefix_matrices(r)
        dc = d_ref[...]
        within = _dot3_right(dc, (a >= b).astype(BF16))
        tot = jnp.broadcast_to(within[:, 0:1], (r, LANES))
        dlf = within + _dot3_left((rb > ra).astype(BF16), tot)
        o_ref[...] = dlf / (1.0 + jnp.exp(f_ref[...]))

    spec = pl.BlockSpec((None, r, LANES), lambda h: (h, 0, 0))
    return pl.pallas_call(
        body, name="fox_cum_bwd", grid=(nh,),
        in_specs=[spec, spec], out_specs=spec,
        out_shape=jax.ShapeDtypeStruct((nh, r, LANES), F32),
        compiler_params=_cparams(("parallel",), 16),
    )(dcum4, flog4)


def even_out_fwd(fo, so, gate, x, w_out):
    s = x.shape[0]
    tm = 2 * TM

    def body(fo_ref, so_ref, g_ref, x_ref, w_ref, y_ref):
        sg = _silu(g_ref[...].astype(F32))
        acc = x_ref[...]
        for hh in range(NH):
            mf = (fo_ref[hh].astype(F32) * sg[:, hh * HD:(hh + 1) * HD]).astype(BF16)
            ms = (so_ref[hh] * sg[:, 512 + hh * HD:512 + (hh + 1) * HD]).astype(BF16)
            acc = acc + _dotf(mf, w_ref[hh * HD:(hh + 1) * HD, :]) + _dotf(ms, w_ref[512 + hh * HD:512 + (hh + 1) * HD, :])
        y_ref[...] = acc

    return pl.pallas_call(
        body, name="even_out_fwd", grid=(s // tm,),
        in_specs=[_heads(tm), _heads(tm), _rows(tm, EVEN_W), _rows(tm, D_MODEL), _whole((EVEN_W, D_MODEL))],
        out_specs=_rows(tm, D_MODEL),
        out_shape=jax.ShapeDtypeStruct((s, D_MODEL), F32),
        compiler_params=_cparams(("parallel",), 40),
    )(fo, so, gate, x, w_out)


def odd_in_fwd(y1, gnorm, w2, gq, gk):
    s = y1.shape[0]

    def body(x_ref, g_ref, w_ref, gq_ref, gk_ref, h_ref, qs_ref, kn_ref, v_ref, qr_ref, kr_ref, gate_ref):
        xt = x_ref[...]
        h = (xt * _rms(xt) * g_ref[...]).astype(BF16)
        h_ref[...] = h
        proj = _dotf(h, w_ref[...])
        for g in range(NG):
            q = proj[:, g * ODD_W:(g + 1) * ODD_W]
            k = proj[:, 1536 + g * ODD_W:1536 + (g + 1) * ODD_W]
            qs_ref[g] = q * _seg_rms(q, HD2) * (gq_ref[...] * SCALE_O)
            kn_ref[g] = k * _seg_rms(k, HD2) * gk_ref[...]
            qr_ref[g] = q.astype(BF16)
            kr_ref[g] = k.astype(BF16)
            v_ref[g] = proj[:, 3072 + g * ODD_W:3072 + (g + 1) * ODD_W]
        gate_ref[...] = proj[:, 4608:5120].astype(BF16)

    gs = lambda dt: jax.ShapeDtypeStruct((NG, s, ODD_W), dt)
    return pl.pallas_call(
        body, name="odd_in_fwd", grid=(s // TM,),
        in_specs=[_rows(TM, D_MODEL), _whole((1, D_MODEL)), _whole((D_MODEL, ODD_IN)), _whole((1, ODD_W)), _whole((1, ODD_W))],
        out_specs=[_rows(TM, D_MODEL)] + [_groups(TM)] * 5 + [_rows(TM, ODD_W)],
        out_shape=[jax.ShapeDtypeStruct((s, D_MODEL), BF16), gs(F32), gs(F32), gs(F32), gs(BF16), gs(BF16),
                   jax.ShapeDtypeStruct((s, ODD_W), BF16)],
        compiler_params=_cparams(("parallel",), 52),
    )(y1, gnorm, w2, gq, gk)


def odd_out_fwd(o0, l0, o1, l1, o2, l2, gate2, y1, target, w_out2):
    s = y1.shape[0]
    tm = 2 * TM
    nt = s // tm

    def body(o0_ref, l0_ref, o1_ref, l1_ref, o2_ref, l2_ref, g_ref, y1_ref, t_ref, w_ref,
             att_ref, lse_ref, dy_ref, loss_ref):
        l0t, l1t, l2t = l0_ref[...], l1_ref[...], l2_ref[...]
        m = jnp.maximum(jnp.maximum(l0t, l1t), l2t)
        e0, e1, e2 = jnp.exp(l0t - m), jnp.exp(l1t - m), jnp.exp(l2t - m)
        den = e0 + e1 + e2
        att = (e0 * o0_ref[...] + e1 * o1_ref[...] + e2 * o2_ref[...]) / den
        att_ref[...] = att.astype(BF16)
        lse_ref[...] = m + jnp.log(den)
        mixed = (att * _silu(g_ref[...].astype(F32))).astype(BF16)
        diff = y1_ref[...] + _dotf(mixed, w_ref[...]) - t_ref[...]
        dy_ref[...] = diff * (1.0 / D_MODEL)
        loss_ref[...] = jnp.full((1, 1, LANES), 0.5 / D_MODEL, F32) * jnp.sum(diff * diff)

    big = jax.ShapeDtypeStruct((s, ODD_W), F32)
    return pl.pallas_call(
        body, name="odd_out_fwd", grid=(nt,),
        in_specs=[_rows(tm, ODD_W)] * 7 + [_rows(tm, D_MODEL), _rows(tm, D_MODEL), _whole((ODD_W, D_MODEL))],
        out_specs=[_rows(tm, ODD_W), _rows(tm, ODD_W), _rows(tm, D_MODEL), pl.BlockSpec((1, 1, LANES), lambda i: (i, 0, 0))],
        out_shape=[jax.ShapeDtypeStruct((s, ODD_W), BF16), big, jax.ShapeDtypeStruct((s, D_MODEL), F32),
                   jax.ShapeDtypeStruct((nt, 1, LANES), F32)],
        compiler_params=_cparams(("parallel",), 40),
    )(o0, l0, o1, l1, o2, l2, gate2, y1, target, w_out2)


def odd_out_bwd(dy2, w_out2_t, att, gate2):
    s = dy2.shape[0]
    tm = 2 * TM

    def body(dy_ref, wt_ref, att_ref, g_ref, datt_ref, dgate_ref, delta_ref, dw_ref):
        @pl.when(pl.program_id(0) == 0)
        def _():
            dw_ref[...] = jnp.zeros_like(dw_ref)

        dyb = dy_ref[...].astype(BF16)
        dmixed = _dotf(dyb, wt_ref[...])
        g = g_ref[...].astype(F32)
        att_t = att_ref[...].astype(F32)
        sg = _silu(g)
        datt = (dmixed * sg).astype(BF16).astype(F32)
        datt_ref[...] = datt
        dgate_ref[...] = (dmixed * att_t * _dsilu(g)).astype(BF16)
        delta_ref[...] = _segsum(datt * att_t, HD2)
        dw_ref[...] += _dot_tn((att_t * sg).astype(BF16), dyb)

    return pl.pallas_call(
        body, name="odd_out_bwd", grid=(s // tm,),
        in_specs=[_rows(tm, D_MODEL), _whole((D_MODEL, ODD_W)), _rows(tm, ODD_W), _rows(tm, ODD_W)],
        out_specs=[_rows(tm, ODD_W), _rows(tm, ODD_W), _rows(tm, ODD_W), _whole((ODD_W, D_MODEL))],
        out_shape=[jax.ShapeDtypeStruct((s, ODD_W), F32), jax.ShapeDtypeStruct((s, ODD_W), BF16),
                   jax.ShapeDtypeStruct((s, ODD_W), F32), jax.ShapeDtypeStruct((ODD_W, D_MODEL), F32)],
        compiler_params=_cparams(("arbitrary",), 40),
    )(dy2, w_out2_t, att, gate2)


def odd_in_bwd(dqs, dks, dvs, dgate2, q2r, k2r, gq, gk, w2_t, y1, dy2, gnorm):
    s = y1.shape[0]

    def body(dq0, dq1, dq2, dk0, dk1, dk2, dv0, dv1, dv2, dg_ref, qr_ref, kr_ref, gq_ref, gk_ref, wt_ref, y1_ref, dy_ref, gn_ref,
             dproj_ref, dy1_ref, dgn_ref, dgain_ref):
        @pl.when(pl.program_id(0) == 0)
        def _():
            dgn_ref[...] = jnp.zeros_like(dgn_ref)
            dgain_ref[...] = jnp.zeros_like(dgain_ref)

        for g, (dq_ref, dk_ref, dv_ref) in enumerate(((dq0, dk0, dv0), (dq1, dk1, dv1), (dq2, dk2, dv2))):
            dq, gq_row = _seg_rms_bwd(qr_ref[g], dq_ref[...], gq_ref[...], SCALE_O, HD2)
            dk, gk_row = _seg_rms_bwd(kr_ref[g], dk_ref[...], gk_ref[...], 1.0, HD2)
            dproj_ref[:, g * ODD_W:(g + 1) * ODD_W] = dq.astype(BF16)
            dproj_ref[:, 1536 + g * ODD_W:1536 + (g + 1) * ODD_W] = dk.astype(BF16)
            dproj_ref[:, 3072 + g * ODD_W:3072 + (g + 1) * ODD_W] = dv_ref[...].astype(BF16)
            dgain_ref[g:g + 1, :] += gq_row
            dgain_ref[NG + g:NG + g + 1, :] += gk_row
        dproj_ref[:, 4608:5120] = dg_ref[...]
        dh = _dotf(dproj_ref[...], wt_ref[...])
        dx, gn_row = _rms_bwd(y1_ref[...], dh, gn_ref[...])
        dy1_ref[...] = dy_ref[...] + dx
        dgn_ref[...] += gn_row

    f32r, bf16r = _rows(TM, ODD_W), _rows(TM, ODD_W)
    return pl.pallas_call(
        body, name="odd_in_bwd", grid=(s // TM,),
        in_specs=[f32r] * 6 + [bf16r] * 4 + [_groups(TM), _groups(TM), _whole((1, ODD_W)), _whole((1, ODD_W)),
                                             _whole((ODD_IN, D_MODEL)), _rows(TM, D_MODEL), _rows(TM, D_MODEL), _whole((1, D_MODEL))],
        out_specs=[_rows(TM, ODD_IN), _rows(TM, D_MODEL), _whole((1, D_MODEL)), _whole((8, ODD_W))],
        out_shape=[jax.ShapeDtypeStruct((s, ODD_IN), BF16), jax.ShapeDtypeStruct((s, D_MODEL), F32),
                   jax.ShapeDtypeStruct((1, D_MODEL), F32), jax.ShapeDtypeStruct((8, ODD_W), F32)],
        compiler_params=_cparams(("arbitrary",), 52),
    )(*dqs, *dks, *dvs, dgate2, q2r, k2r, gq, gk, w2_t, y1, dy2, gnorm)


def even_out_bwd(dy1, w_out_t, fo, so, gate):
    s = dy1.shape[0]
    tm = 2 * TM

    def body(dy_ref, wt_ref, fo_ref, so_ref, g_ref, dfo_ref, dso_ref, dgate_ref, delf_ref, dels_ref, dw_ref):
        @pl.when(pl.program_id(0) == 0)
        def _():
            dw_ref[...] = jnp.zeros_like(dw_ref)

        dyb = dy_ref[...].astype(BF16)
        dmixed = _dotf(dyb, wt_ref[...])
        g = g_ref[...].astype(F32)
        sg, dsg = _silu(g), _dsilu(g)
        for hh in range(NH):
            for base, o_ref, do_ref, del_ref in ((0, fo_ref, dfo_ref, delf_ref), (512, so_ref, dso_ref, dels_ref)):
                sl = slice(base + hh * HD, base + (hh + 1) * HD)
                o = o_ref[hh].astype(F32)
                do = (dmixed[:, sl] * sg[:, sl]).astype(BF16)
                do_ref[hh] = do
                del_ref[hh] = jnp.sum(do.astype(F32) * o, axis=1, keepdims=True)
                dgate_ref[:, sl] = (dmixed[:, sl] * o * dsg[:, sl]).astype(BF16)
                dw_ref[sl, :] += _dot_tn((o * sg[:, sl]).astype(BF16), dyb)

    cols = pl.BlockSpec((NH, tm, 1), lambda i: (0, i, 0))
    hs = jax.ShapeDtypeStruct((NH, s, HD), BF16)
    cs = jax.ShapeDtypeStruct((NH, s, 1), F32)
    return pl.pallas_call(
        body, name="even_out_bwd", grid=(s // tm,),
        in_specs=[_rows(tm, D_MODEL), _whole((D_MODEL, EVEN_W)), _heads(tm), _heads(tm), _rows(tm, EVEN_W)],
        out_specs=[_heads(tm), _heads(tm), _rows(tm, EVEN_W), cols, cols, _whole((EVEN_W, D_MODEL))],
        out_shape=[hs, hs, jax.ShapeDtypeStruct((s, EVEN_W), BF16), cs, cs, jax.ShapeDtypeStruct((EVEN_W, D_MODEL), F32)],
        compiler_params=_cparams(("arbitrary",), 48),
    )(dy1, w_out_t, fo, so, gate)


def even_in_bwd(dfqs, dfkn, dfv, dsqs, dsk, dsv, dgate, dflog, fqr, fkr, gq, gk, w_pack_t, x, dy1, gnorm):
    s = x.shape[0]

    def body(dfq_ref, dfk_ref, dfv_ref, dsq_ref, dsk_ref, dsv_ref, dg_ref, dfl_ref, qr_ref, kr_ref, gq_ref, gk_ref,
             wt_ref, x_ref, dy_ref, gn_ref, dproj_ref, dx_ref, dgn_ref, dgain_ref, dbf_ref):
        @pl.when(pl.program_id(0) == 0)
        def _():
            dgn_ref[...] = jnp.zeros_like(dgn_ref)
            dgain_ref[...] = jnp.zeros_like(dgain_ref)
            dbf_ref[...] = jnp.zeros_like(dbf_ref)

        o = FL_OFF + LANES
        for hh in range(NH):
            sl = slice(hh * HD, (hh + 1) * HD)
            dq, gq_row = _seg_rms_bwd(qr_ref[hh], dfq_ref[hh], gq_ref[:, sl], SCALE_E, HD)
            dk, gk_row = _seg_rms_bwd(kr_ref[hh], dfk_ref[hh] * LN2, gk_ref[:, sl], 1.0, HD)
            dproj_ref[:, sl] = dq.astype(BF16)
            dproj_ref[:, 512 + hh * HD:512 + (hh + 1) * HD] = dk.astype(BF16)
            dproj_ref[:, 1024 + hh * HD:1024 + (hh + 1) * HD] = dfv_ref[hh].astype(BF16)
            dproj_ref[:, o + hh * HD:o + (hh + 1) * HD] = (dsq_ref[hh] * SCALE_E).astype(BF16)
            dproj_ref[:, o + 512 + hh * HD:o + 512 + (hh + 1) * HD] = (dsk_ref[hh] * LN2).astype(BF16)
            dproj_ref[:, o + 1024 + hh * HD:o + 1024 + (hh + 1) * HD] = dsv_ref[hh].astype(BF16)
            dgain_ref[0:1, sl] += gq_row
            dgain_ref[1:2, sl] += gk_row
        dfl = dfl_ref[...]
        dproj_ref[:, FL_OFF:FL_OFF + LANES] = dfl.astype(BF16)
        dbf_ref[...] += jnp.sum(dfl, axis=0, keepdims=True)
        dproj_ref[:, o + 1536:o + 2560] = dg_ref[...]
        dh = _dotf(dproj_ref[...], wt_ref[...])
        dx, gn_row = _rms_bwd(x_ref[...], dh, gn_ref[...])
        dx_ref[...] = dy_ref[...] + dx
        dgn_ref[...] += gn_row

    return pl.pallas_call(
        body, name="even_in_bwd", grid=(s // TM,),
        in_specs=[_heads(TM)] * 6 + [_rows(TM, EVEN_W), _rows(TM, LANES), _heads(TM), _heads(TM), _whole((1, 512)), _whole((1, 512)),
                                     _whole((EVEN_PACK, D_MODEL)), _rows(TM, D_MODEL), _rows(TM, D_MODEL), _whole((1, D_MODEL))],
        out_specs=[_rows(TM, EVEN_PACK), _rows(TM, D_MODEL), _whole((1, D_MODEL)), _whole((8, 512)), _whole((1, LANES))],
        out_shape=[jax.ShapeDtypeStruct((s, EVEN_PACK), BF16), jax.ShapeDtypeStruct((s, D_MODEL), F32),
                   jax.ShapeDtypeStruct((1, D_MODEL), F32), jax.ShapeDtypeStruct((8, 512), F32), jax.ShapeDtypeStruct((1, LANES), F32)],
        compiler_params=_cparams(("arbitrary",), 52),
    )(dfqs, dfkn, dfv, dsqs, dsk, dsv, dgate, dflog, fqr, fkr, gq, gk, w_pack_t, x, dy1, gnorm)


def matmul_tn(a, b, tm, name):
    s, m = a.shape
    n = b.shape[1]
    tk = 2 * TM
    nk = s // tk

    def body(a_ref, b_ref, o_ref):
        @pl.when(pl.program_id(1) == 0)
        def _():
            o_ref[...] = jnp.zeros_like(o_ref)

        o_ref[...] += _dot_tn(a_ref[...], b_ref[...])

    return pl.pallas_call(
        body, name=name, grid=(m // tm, nk),
        in_specs=[pl.BlockSpec((tk, tm), lambda j, k: (k, j)), pl.BlockSpec((tk, n), lambda j, k: (k, 0))],
        out_specs=pl.BlockSpec((tm, n), lambda j, k: (j, 0)),
        out_shape=jax.ShapeDtypeStruct((m, n), F32),
        compiler_params=_cparams(("parallel", "arbitrary"), 32),
    )(a, b)


def _tile_gain(g, reps):
    return jnp.tile(g.reshape(1, -1), (1, reps))


def local_step(x, target, w_in_e_t, b_f, gq_e, gk_e, gn_e, w_out_e, gn_o, w_in_o_t, gq_o, gk_o, w_out_o, fox_blocks, sb_blocks):
    s = x.shape[0]
    r = s // LANES
    w_pack_t = jnp.concatenate([w_in_e_t[:FL_OFF + NH], jnp.zeros((LANES - NH, D_MODEL), BF16), w_in_e_t[FL_OFF + NH:]], axis=0)
    w_pack, w_in_o = w_pack_t.T, w_in_o_t.T
    bf_pad = jnp.pad(b_f.reshape(1, NH), ((0, 0), (0, LANES - NH)))
    gq512, gk512 = _tile_gain(gq_e, NH), _tile_gain(gk_e, NH)
    gq2, gk2 = _tile_gain(gq_o, NH2), _tile_gain(gk_o, NH2)
    gn_e, gn_o = gn_e.reshape(1, D_MODEL), gn_o.reshape(1, D_MODEL)

    h, fqs, fkn, fv, fqr, fkr, flog, sqs, sk, sv, gate = even_in_fwd(x, gn_e, w_pack, bf_pad, gq512, gk512)
    flog4 = flog[:, :NH].T.reshape(NH, r, LANES)
    nc = fox_cum(flog4).reshape(NH, 1, s)
    fo, lse = fox_fwd(fqs, fkn, fv, nc, *fox_blocks)
    so = sb_fwd(sqs, sk, sv, *sb_blocks)
    y1 = even_out_fwd(fo, so, gate, x, w_out_e)
    h2, q2s, k2n, v2, q2r, k2r, gate2 = odd_in_fwd(y1, gn_o, w_in_o, gq2, gk2)
    ol = [dil_fwd(q2s, k2n, v2, g) for g in range(NG)]
    att, lse2, dy2, loss_parts = odd_out_fwd(ol[0][0], ol[0][1], ol[1][0], ol[1][1], ol[2][0], ol[2][1], gate2, y1, target, w_out_o)
    loss = jnp.sum(loss_parts[:, 0, 0])
    datt, dgate2, delta2, d_w_out_o = odd_out_bwd(dy2, w_out_o.T, att, gate2)
    dqkv = [dil_bwd(q2s, k2n, v2, datt, lse2, delta2, g) for g in range(NG)]
    dproj2, dy1, d_gn_o, dgain_o = odd_in_bwd([t[0] for t in dqkv], [t[1] for t in dqkv], [t[2] for t in dqkv], dgate2,
                                              q2r, k2r, gq2, gk2, w_in_o_t, y1, dy2, gn_o)
    d_w_in_o_t = matmul_tn(dproj2, h2, ODD_IN // 4, "dw_in_odd")
    dfo, dso, dgate, delta_f, delta_s, d_w_out_e = even_out_bwd(dy1, w_out_e.T, fo, so, gate)
    dfqs, dfkn, dfv, dnc, drow = fox_bwd(fqs, fkn, fv, nc, lse, dfo, delta_f, fox_blocks[0], fox_blocks[1] // 2)
    dsqs, dsk, dsv = sb_bwd(sqs, sk, sv, dso, delta_s, *sb_blocks)
    dcum4 = (drow.reshape(NH, s) - dnc.reshape(NH, s)).reshape(NH, r, LANES)
    dflog4 = fox_cum_bwd(dcum4, flog4)
    dflog = jnp.pad(dflog4.reshape(NH, s).T, ((0, 0), (0, LANES - NH)))
    dproj, grad_x, d_gn_e, dgain_e, d_bf = even_in_bwd(dfqs, dfkn, dfv, dsqs, dsk, dsv, dgate, dflog, fqr, fkr, gq512, gk512,
                                                       w_pack_t, x, dy1, gn_e)
    d_w_pack_t = matmul_tn(dproj, h, EVEN_PACK // 3, "dw_in_even")
    d_w_in_e_t = jnp.concatenate([d_w_pack_t[:FL_OFF + NH], d_w_pack_t[FL_OFF + LANES:]], axis=0)
    grads = dict(
        even_norm=d_gn_e.reshape(-1), even_w_in_t=d_w_in_e_t, even_b_f=d_bf[0, :NH],
        even_q_gain=dgain_e[0].reshape(NH, HD).sum(0), even_k_gain=dgain_e[1].reshape(NH, HD).sum(0),
        even_w_out=d_w_out_e, odd_norm=d_gn_o.reshape(-1), odd_w_in_t=d_w_in_o_t,
        odd_q_gain=dgain_o[:NG].reshape(NG * NH2, HD2).sum(0), odd_k_gain=dgain_o[NG:2 * NG].reshape(NG * NH2, HD2).sum(0),
        odd_w_out=d_w_out_o)
    return loss, grad_x, grads


BIG = (("even_w_in_t", (EVEN_IN // 4, D_MODEL)), ("even_w_out", (EVEN_W // 4, D_MODEL)),
       ("odd_w_in_t", (ODD_IN // 4, D_MODEL)), ("odd_w_out", (ODD_W, D_MODEL // 4)))
VECTORS = (("odd_norm", D_MODEL // 4), ("odd_norm_lo", D_MODEL // 4), ("even_norm", D_MODEL), ("even_b_f", NH),
           ("even_q_gain", HD), ("even_k_gain", HD), ("odd_q_gain", HD2), ("odd_k_gain", HD2))
TILE_ROWS = 16


def _block_rows(shape):
    return -(-(shape[0] * shape[1] // D_MODEL) // TILE_ROWS) * TILE_ROWS


PACK_ROWS = sum(_block_rows(shape) for _, shape in BIG) + TILE_ROWS
HALF = PACK_ROWS // 2
assert HALF % TILE_ROWS == 0
HBM = pl.BlockSpec(memory_space=pl.ANY)


GRAD_VECTORS = (("odd_norm", D_MODEL),) + VECTORS[2:]


def _vector_block(parts, layout=VECTORS):
    rows = [jnp.pad(parts[n].reshape(-1), (0, D_MODEL - size)) for n, size in layout]
    return jnp.pad(jnp.stack(rows), ((0, TILE_ROWS - len(layout)), (0, 0)))


def _vectors_of(block, layout=VECTORS):
    return {n: block[i, :size] for i, (n, size) in enumerate(layout)}


def _pack(parts):
    blocks = []
    for n, shape in BIG:
        t = parts[n].reshape(-1, D_MODEL)
        blocks.append(jnp.pad(t, ((0, _block_rows(shape) - t.shape[0]), (0, 0))))
    blocks.append(_vector_block(parts).astype(blocks[0].dtype))
    return jnp.concatenate(blocks, axis=0)


def _unpack(buf):
    out, off = {}, 0
    for n, shape in BIG:
        rows = shape[0] * shape[1] // D_MODEL
        out[n] = buf[off:off + rows].reshape(shape)
        off += _block_rows(shape)
    out.update(_vectors_of(buf[off:off + TILE_ROWS]))
    return out


def _place():
    x, y, c = lax.axis_index("x"), lax.axis_index("y"), lax.axis_index("c")
    return x, y, c, [(1 - x, y), (x, 1 - y), (1 - x, 1 - y)]


def all_gather_shards(mine):
    def body(src_ref, out_ref, send_sems, recv_sems):
        x, y, c, chips = _place()
        me = 2 * x + y
        half = lambda cc: pl.ds(cc * HALF, HALF)

        def copy(k, j, cc, to, src=None):
            dst = out_ref.at[j, half(cc)]
            return pltpu.make_async_remote_copy(src_ref=dst if src is None else src, dst_ref=dst,
                                                send_sem=send_sems.at[k], recv_sem=recv_sems.at[k],
                                                device_id=to, device_id_type=MESH)

        first = [copy(k, me, c, (cx, cy, c), src=src_ref.at[half(c)]) for k, (cx, cy) in enumerate(chips)]
        for cp in first:
            cp.start()
        passed = [copy(3 + k, 2 * cx + cy, c, (x, y, 1 - c)) for k, (cx, cy) in enumerate(chips)]
        for k, (cx, cy) in enumerate(chips):
            copy(k, 2 * cx + cy, c, (x, y, c)).wait_recv()
            passed[k].start()
        for k, (cx, cy) in enumerate(chips):
            copy(3 + k, 2 * cx + cy, 1 - c, (x, y, c)).wait_recv()
        for cp in first + passed:
            cp.wait_send()

    return pl.pallas_call(
        body, name="all_gather_shards",
        in_specs=[HBM], out_specs=HBM,
        out_shape=jax.ShapeDtypeStruct((4, PACK_ROWS, D_MODEL), mine.dtype),
        scratch_shapes=[pltpu.SemaphoreType.DMA((6,)), pltpu.SemaphoreType.DMA((6,))],
    )(mine)


def sibling_swap_halves(g):
    def body(g_ref, a_ref, send_sem, recv_sem):
        x, y, c, _ = _place()
        cp = pltpu.make_async_remote_copy(src_ref=g_ref.at[:, pl.ds((1 - c) * HALF, HALF)], dst_ref=a_ref,
                                          send_sem=send_sem, recv_sem=recv_sem, device_id=(x, y, 1 - c), device_id_type=MESH)
        cp.start()
        cp.wait()

    return pl.pallas_call(
        body, name="sibling_swap_halves",
        in_specs=[HBM], out_specs=HBM,
        out_shape=jax.ShapeDtypeStruct((4, HALF, D_MODEL), g.dtype),
        scratch_shapes=[pltpu.SemaphoreType.DMA, pltpu.SemaphoreType.DMA],
    )(g)


def chip_exchange(p):
    def body(p_ref, b_ref, send_sems, recv_sems):
        x, y, c, chips = _place()
        me = 2 * x + y
        sends = [pltpu.make_async_remote_copy(src_ref=p_ref.at[2 * cx + cy], dst_ref=b_ref.at[me],
                                              send_sem=send_sems.at[k], recv_sem=recv_sems.at[k],
                                              device_id=(cx, cy, c), device_id_type=MESH)
                 for k, (cx, cy) in enumerate(chips)]
        for cp in sends:
            cp.start()
        for k, (cx, cy) in enumerate(chips):
            pltpu.make_async_remote_copy(src_ref=p_ref.at[me], dst_ref=b_ref.at[2 * cx + cy],
                                         send_sem=send_sems.at[k], recv_sem=recv_sems.at[k],
                                         device_id=(cx, cy, c), device_id_type=MESH).wait_recv()
        for cp in sends:
            cp.wait_send()

    return pl.pallas_call(
        body, name="chip_exchange",
        in_specs=[HBM], out_specs=HBM,
        out_shape=jax.ShapeDtypeStruct((4, HALF, D_MODEL), p.dtype),
        scratch_shapes=[pltpu.SemaphoreType.DMA((3,)), pltpu.SemaphoreType.DMA((3,))],
    )(p)


def sibling_swap(mine):
    def body(h_ref, out_ref, send_sem, recv_sem):
        x, y, c, _ = _place()
        cp = pltpu.make_async_remote_copy(src_ref=h_ref, dst_ref=out_ref, send_sem=send_sem, recv_sem=recv_sem,
                                          device_id=(x, y, 1 - c), device_id_type=MESH)
        cp.start()
        cp.wait()

    return pl.pallas_call(
        body, name="sibling_swap",
        in_specs=[HBM], out_specs=HBM,
        out_shape=jax.ShapeDtypeStruct(mine.shape, mine.dtype),
        scratch_shapes=[pltpu.SemaphoreType.DMA, pltpu.SemaphoreType.DMA],
    )(mine)


def _sum_call(name, arrays, rows, out_dtype):
    tr = rows // 5 if rows % (5 * TILE_ROWS) == 0 else rows

    def body(*refs):
        acc = refs[0][...].astype(F32)
        for r in refs[1:-1]:
            acc = acc + r[...].astype(F32)
        refs[-1][...] = acc.astype(out_dtype)

    spec = pl.BlockSpec((tr, D_MODEL), lambda i: (i, 0))
    return pl.pallas_call(
        body, name=name, grid=(rows // tr,),
        in_specs=[spec] * len(arrays), out_specs=spec,
        out_shape=jax.ShapeDtypeStruct((rows, D_MODEL), out_dtype),
        compiler_params=_cparams(("parallel",), 40),
    )(*arrays)


def allreduce_rows(block):
    n_dev = 8

    def body(src_ref, out_ref, slots, send_sems, recv_sems):
        x, y, c = lax.axis_index("x"), lax.axis_index("y"), lax.axis_index("c")
        flip = lambda v, bit: 1 - v if bit else v
        index = lambda px, py, pc: 4 * px + 2 * py + pc
        slots[index(x, y, c)] = src_ref[...]
        peers = [(flip(x, k >> 2 & 1), flip(y, k >> 1 & 1), flip(c, k & 1)) for k in range(1, n_dev)]

        def copy(k, slot, peer):
            return pltpu.make_async_remote_copy(src_ref=src_ref, dst_ref=slots.at[slot], send_sem=send_sems.at[k],
                                                recv_sem=recv_sems.at[k], device_id=peer, device_id_type=MESH)

        sends = [copy(k, index(x, y, c), peer) for k, peer in enumerate(peers)]
        for cp in sends:
            cp.start()
        for k, peer in enumerate(peers):
            copy(k, index(*peer), peer).wait_recv()
        for cp in sends:
            cp.wait_send()
        acc = slots[0]
        for dev in range(1, n_dev):
            acc = acc + slots[dev]
        out_ref[...] = acc

    vmem = pl.BlockSpec(memory_space=pltpu.VMEM)
    return pl.pallas_call(
        body, name="allreduce_rows",
        in_specs=[vmem], out_specs=vmem,
        out_shape=jax.ShapeDtypeStruct(block.shape, F32),
        scratch_shapes=[pltpu.VMEM((n_dev,) + block.shape, F32), pltpu.SemaphoreType.DMA((n_dev - 1,)),
                        pltpu.SemaphoreType.DMA((n_dev - 1,))],
    )(block)


def adamw(w, g, m, v, name):
    rows, cols = w.shape
    tr = min(rows, TM)

    def body(w_ref, g_ref, m_ref, v_ref, d_ref, nm_ref, nv_ref):
        gt = g_ref[...]
        nm = ADAM_B1 * m_ref[...] + (1.0 - ADAM_B1) * gt
        nv = ADAM_B2 * v_ref[...] + (1.0 - ADAM_B2) * (gt * gt)
        m_hat = nm / (1.0 - ADAM_B1 ** ADAM_STEP)
        v_hat = nv / (1.0 - ADAM_B2 ** ADAM_STEP)
        d_ref[...] = -ADAM_LR * (m_hat / (jnp.sqrt(v_hat) + ADAM_EPS) + ADAM_WD * w_ref[...])
        nm_ref[...] = nm
        nv_ref[...] = nv

    spec = pl.BlockSpec((tr, cols), lambda i: (i, 0))
    shape = jax.ShapeDtypeStruct((rows, cols), F32)
    return pl.pallas_call(
        body, name=name, grid=(rows // tr,),
        in_specs=[spec] * 4, out_specs=[spec] * 3, out_shape=[shape] * 3,
        compiler_params=_cparams(("parallel",), 40),
    )(w, g, m, v)


def kernel(x, even_norm, even_w_in, even_b_f, even_q_gain, even_k_gain, even_w_out, odd_norm, odd_w_in, odd_q_gain, odd_k_gain, odd_w_out, loss_target, m_even_norm, m_even_w_in, m_even_b_f, m_even_q_gain, m_even_k_gain, m_even_w_out, m_odd_norm, m_odd_w_in, m_odd_q_gain, m_odd_k_gain, m_odd_w_out, v_even_norm, v_even_w_in, v_even_b_f, v_even_q_gain, v_even_k_gain, v_even_w_out, v_odd_norm, v_odd_w_in, v_odd_q_gain, v_odd_k_gain, v_odd_w_out):
    w = dict(even_norm=even_norm, even_w_in=even_w_in, even_b_f=even_b_f, even_q_gain=even_q_gain, even_k_gain=even_k_gain,
             even_w_out=even_w_out, odd_norm=odd_norm, odd_w_in=odd_w_in, odd_q_gain=odd_q_gain, odd_k_gain=odd_k_gain,
             odd_w_out=odd_w_out)
    m = dict(even_norm=m_even_norm, even_w_in=m_even_w_in, even_b_f=m_even_b_f, even_q_gain=m_even_q_gain,
             even_k_gain=m_even_k_gain, even_w_out=m_even_w_out, odd_norm=m_odd_norm, odd_w_in=m_odd_w_in,
             odd_q_gain=m_odd_q_gain, odd_k_gain=m_odd_k_gain, odd_w_out=m_odd_w_out)
    v = dict(even_norm=v_even_norm, even_w_in=v_even_w_in, even_b_f=v_even_b_f, even_q_gain=v_even_q_gain,
             even_k_gain=v_even_k_gain, even_w_out=v_even_w_out, odd_norm=v_odd_norm, odd_w_in=v_odd_w_in,
             odd_q_gain=v_odd_q_gain, odd_k_gain=v_odd_k_gain, odd_w_out=v_odd_w_out)
    spare = jnp.zeros((D_MODEL // 4,), F32)
    vector_names = [n for n, _ in VECTORS if n != "odd_norm_lo"]
    flat = lambda d: {**{n: d[n].reshape(-1) for n in vector_names}, "odd_norm_lo": spare}

    on = odd_norm.reshape(-1)
    on_hi = on.astype(BF16)
    wire = {n: t.astype(BF16) for n, t in flat(w).items()}
    wire.update(odd_norm=on_hi, odd_norm_lo=(on - on_hi.astype(F32)).astype(BF16),
                even_w_in_t=even_w_in[0].T.astype(BF16), even_w_out=even_w_out[0].astype(BF16),
                odd_w_in_t=odd_w_in[0].T.astype(BF16), odd_w_out=odd_w_out[0].astype(BF16))
    chip, c = 2 * lax.axis_index("x") + lax.axis_index("y"), lax.axis_index("c")
    packed = _pack(wire)
    gathered = lax.dynamic_update_index_in_dim(all_gather_shards(packed), packed, chip, 0)
    sh = [_unpack(gathered[j]) for j in range(4)]
    cat = lambda n, axis: jnp.concatenate([t[n] for t in sh], axis=axis)
    gn_o = cat("odd_norm", 0).astype(F32) + cat("odd_norm_lo", 0).astype(F32)

    s = x.shape[1]
    loss_local, grad_x, g = local_step(
        x[0], loss_target[0], cat("even_w_in_t", 0), even_b_f[0], even_q_gain[0], even_k_gain[0], even_norm[0],
        cat("even_w_out", 0), gn_o, cat("odd_w_in_t", 0), odd_q_gain[0], odd_k_gain[0], cat("odd_w_out", 1),
        (min(1024, s), min(512, s)), (min(1024, s), min(256, s)))
    loss = lax.psum(loss_local, ("x", "y", "c"))

    g_vec = _vectors_of(allreduce_rows(_vector_block(g, GRAD_VECTORS)), GRAD_VECTORS)
    g_vec["odd_norm"] = lax.dynamic_slice_in_dim(g_vec["odd_norm"], chip * (D_MODEL // 4), D_MODEL // 4)

    zero_vectors = {n: jnp.zeros((size,), F32) for n, size in VECTORS}

    def grad_parts(j):
        parts = dict(zero_vectors)
        parts["even_w_in_t"] = g["even_w_in_t"][j * (EVEN_IN // 4):(j + 1) * (EVEN_IN // 4)]
        parts["even_w_out"] = g["even_w_out"][j * (EVEN_W // 4):(j + 1) * (EVEN_W // 4)]
        parts["odd_w_in_t"] = g["odd_w_in_t"][j * (ODD_IN // 4):(j + 1) * (ODD_IN // 4)]
        parts["odd_w_out"] = g["odd_w_out"][:, j * (D_MODEL // 4):(j + 1) * (D_MODEL // 4)]
        return parts

    g_all = jnp.stack([_pack(grad_parts(j)) for j in range(4)])
    from_sibling = sibling_swap_halves(g_all)
    g_mine = lax.dynamic_slice_in_dim(g_all, c * HALF, HALF, axis=1)
    pair = _sum_call("pair_sum", [g_mine.reshape(4 * HALF, D_MODEL), from_sibling.reshape(4 * HALF, D_MODEL)], 4 * HALF, BF16)
    pair = pair.reshape(4, HALF, D_MODEL)
    by_chip = lax.dynamic_update_index_in_dim(chip_exchange(pair), lax.dynamic_index_in_dim(pair, chip, 0), chip, 0)
    half_sum = _sum_call("chip_sum", [by_chip[0], by_chip[1], by_chip[2], by_chip[3]], HALF, F32)
    other_half = sibling_swap(half_sum)
    g_buf = jnp.concatenate([jnp.where(c == 0, half_sum, other_half), jnp.where(c == 0, other_half, half_sum)], axis=0)

    gp = _unpack(g_buf)
    grad = dict(g_vec)
    grad.update(even_w_in=gp["even_w_in_t"].T, even_w_out=gp["even_w_out"], odd_w_in=gp["odd_w_in_t"].T, odd_w_out=gp["odd_w_out"])
    results = [grad, {}, {}, {}]
    for n in ("even_w_in", "even_w_out", "odd_w_in", "odd_w_out"):
        stepped = adamw(w[n][0], grad[n], m[n][0], v[n][0], "adamw_" + n)
        for res, t in zip(results[1:], stepped):
            res[n] = t
    block = lambda d: _vector_block(flat(d))
    stepped = adamw(block(w), _vector_block({**grad, "odd_norm_lo": spare}), block(m), block(v), "adamw_vectors")
    for res, t in zip(results[1:], stepped):
        res.update(_vectors_of(t))
    outs = [loss.reshape(()), grad_x.reshape(x.shape)]
    order = ["even_norm", "even_w_in", "even_b_f", "even_q_gain", "even_k_gain", "even_w_out", "odd_norm", "odd_w_in",
             "odd_q_gain", "odd_k_gain", "odd_w_out"]
    for res in results:
        outs += [res[n].reshape(w[n].shape) for n in order]
    return tuple(outs)
```

```python
import jax
import jax.numpy as jnp
from jax import lax
from jax.experimental import pallas as pl
from jax.experimental.pallas import tpu as pltpu

F32 = jnp.float32
BF16 = jnp.bfloat16

D_MODEL = 1024
HD = 128
NH = 4
HD2 = 64
NG = 3
NH2 = 8
DILATIONS = (1, 4, 16)
SPAN = 128
EVEN_W = 1024
ODD_W = 512
EVEN_IN = 4100
EVEN_PACK = 4224
FL_OFF = 1536
ODD_IN = 5120
RMS_EPS = 1e-6
SCALE_E = HD ** -0.5
SCALE_O = HD2 ** -0.5
ADAM_LR, ADAM_B1, ADAM_B2, ADAM_EPS, ADAM_WD, ADAM_STEP = 0.001, 0.9, 0.999, 1e-08, 0.01, 10

VMEM_CAP = 64 * 1024 * 1024
LANES = 128
MESH = pl.DeviceIdType.MESH


def _cparams(sem, vmem_mb):
    return pltpu.CompilerParams(dimension_semantics=sem, vmem_limit_bytes=min(vmem_mb << 20, VMEM_CAP - (6 << 20)))


def _silu(g):
    return g / (1.0 + jnp.exp(-g))


def _dsilu(g):
    s = 1.0 / (1.0 + jnp.exp(-g))
    return s * (1.0 + g * (1.0 - s))


def _split2(x):
    hi = x.astype(BF16)
    lo = (x - hi.astype(F32)).astype(BF16)
    return hi, lo


def _split3(x):
    hi = x.astype(BF16)
    r = x - hi.astype(F32)
    mid = r.astype(BF16)
    lo = (r - mid.astype(F32)).astype(BF16)
    return hi, mid, lo


def _dotf(a, b):
    return jnp.dot(a, b, preferred_element_type=F32)


def _dot_nt(a, b):
    return lax.dot_general(a, b, (((1,), (1,)), ((), ())), preferred_element_type=F32)


def _dot_tn(a, b):
    return lax.dot_general(a, b, (((0,), (0,)), ((), ())), preferred_element_type=F32)


def _segsum(x, hd):
    r = lax.broadcasted_iota(jnp.int32, (LANES, LANES), 0) // hd
    c = lax.broadcasted_iota(jnp.int32, (LANES, LANES), 1) // hd
    ones = (r == c).astype(BF16)
    ones2 = jnp.concatenate([ones, ones], axis=0)
    outs = []
    for ch in range(x.shape[1] // LANES):
        hi, lo = _split2(x[:, ch * LANES:(ch + 1) * LANES])
        outs.append(_dotf(jnp.concatenate([hi, lo], axis=1), ones2))
    return outs[0] if len(outs) == 1 else jnp.concatenate(outs, axis=1)


def _suffix_matrix(n):
    r = lax.broadcasted_iota(jnp.int32, (n, n), 0)
    c = lax.broadcasted_iota(jnp.int32, (n, n), 1)
    return (r >= c).astype(BF16)


def _walk_up_staged(i, per, stages):
    assert per % 2 == 0
    n_full = i * per

    def trip(states, masked):
        for stage in stages:
            for st in states:
                stage(st, masked)

    def full_trip(t, c):
        trip([{"j": 2 * t + u, "band": 0} for u in range(2)], False)
        return c

    lax.fori_loop(0, n_full // 2, full_trip, 0)
    for first in range(0, per, 2):
        trip([{"j": n_full + b, "band": b} for b in (first, first + 1)], True)


def fox_fwd(qs, kn, v, nc, bq, bk):
    nh, s, _ = qs.shape
    nq = s // bq
    per = bq // bk

    def body(q_ref, k_ref, v_ref, nc_ref, o_ref, lse_ref):
        i = pl.program_id(1)
        q = q_ref[...]

        def logits(j, masked):
            off = pl.multiple_of(j * bk, bk)
            sc = _dot_nt(q, k_ref[pl.ds(off, bk), :]) + nc_ref[:, pl.ds(off, bk)]
            if masked:
                row = i * bq + lax.broadcasted_iota(jnp.int32, (bq, bk), 0)
                col = off + lax.broadcasted_iota(jnp.int32, (bq, bk), 1)
                sc = jnp.where(col <= row, sc, -jnp.inf)
            return off, sc

        def update(carry, off, sc):
            m, l, acc = carry
            m_new = jnp.maximum(m, jnp.max(sc, axis=1, keepdims=True))
            alpha = jnp.exp2(m - m_new)
            p = jnp.exp2(sc - m_new)
            l = alpha * l + jnp.sum(p, axis=1, keepdims=True)
            acc = alpha * acc + _dotf(p.astype(BF16), v_ref[pl.ds(off, bk), :])
            return m_new, l, acc

        def pair(first, carry, masked):
            a, b = logits(first, masked), logits(first + 1, masked)
            return update(update(carry, *a), *b)

        assert per % 2 == 0
        n_full = i * per
        init = (jnp.full((bq, 1), -jnp.inf, F32), jnp.zeros((bq, 1), F32), jnp.zeros((bq, HD), F32))
        carry = lax.fori_loop(0, n_full // 2, lambda t, c: pair(2 * t, c, False), init)
        m, l, acc = lax.fori_loop(0, per // 2, lambda t, c: pair(n_full + 2 * t, c, True), carry)
        o_ref[...] = (acc / l).astype(o_ref.dtype)
        lse_ref[...] = m + jnp.log2(l)

    return pl.pallas_call(
        body, name="fox_fwd",
        grid=(nh, nq),
        in_specs=[pl.BlockSpec((None, bq, HD), lambda h, i: (h, i, 0)),
                  pl.BlockSpec((None, s, HD), lambda h, i: (h, 0, 0)),
                  pl.BlockSpec((None, s, HD), lambda h, i: (h, 0, 0)),
                  pl.BlockSpec((None, 1, s), lambda h, i: (h, 0, 0))],
        out_specs=[pl.BlockSpec((None, bq, HD), lambda h, i: (h, i, 0)),
                   pl.BlockSpec((None, bq, 1), lambda h, i: (h, i, 0))],
        out_shape=[jax.ShapeDtypeStruct((nh, s, HD), BF16), jax.ShapeDtypeStruct((nh, s, 1), F32)],
        compiler_params=_cparams(("arbitrary", "arbitrary"), 40),
    )(qs, kn, v, nc)


LOG2E = 1.4426950408889634
LN2 = 0.6931471805599453


def _neg_abs(z):
    sign = jnp.uint32(0x80000000)
    return lax.bitcast_convert_type(lax.bitcast_convert_type(z, jnp.uint32) | sign, F32)


def _sb_softplus2(z, row0, col0, masked):
    u = jnp.maximum(z, 0.0) + jnp.log2(1.0 + jnp.exp2(_neg_abs(z)))
    strict = None
    if masked:
        row = row0 + lax.broadcasted_iota(jnp.int32, z.shape, 0)
        col = col0 + lax.broadcasted_iota(jnp.int32, z.shape, 1)
        strict = col < row
        u = jnp.where(strict, u, 0.0)
    return u, strict


def _walk_down_staged(i, per, stages, group=2, unroll_band=True):
    assert per % group == 0
    n_full = i * per

    def trip(states, masked):
        for stage in stages:
            for st in states:
                stage(st, masked)

    def tiles_down(top, band):
        return [{"j": top - u, "band": 0 if band is None else band - u} for u in range(group)]

    def band_trip(t, c):
        trip(tiles_down(n_full + per - 1 - group * t, None), True)
        return c

    def full_trip(t, c):
        trip(tiles_down(n_full - 1 - group * t, None), False)
        return c

    if unroll_band:
        for first in range(per - 1, -1, -group):
            trip(tiles_down(n_full + first, first), True)
    else:
        lax.fori_loop(0, per // group, band_trip, 0)
    lax.fori_loop(0, n_full // group, full_trip, 0)


def _suffix2(x, m2):
    hi, lo = _split2(x)
    return _dotf(jnp.concatenate([hi, lo], axis=1), m2)


def _lanes(col, n):
    return jnp.broadcast_to(col, (col.shape[0], n))


def sb_fwd(qs, k, v, bq, bk):
    nh, s, _ = qs.shape
    nq = s // bq
    per = bq // bk

    def body(q_ref, k_ref, v_ref, o_ref, lrun_ref):
        i = pl.program_id(1)
        q = q_ref[...]
        tri = _suffix_matrix(bk)
        o_ref[...] = jnp.zeros_like(o_ref)
        lrun_ref[...] = jnp.zeros_like(lrun_ref)

        def logits(st, masked):
            st["off"] = pl.multiple_of(st["j"] * bk, bk)
            st["r0"] = st["band"] * bk
            st["z"] = _dot_nt(q[st["r0"]:], k_ref[pl.ds(st["off"], bk), :])

        def suffix(st, masked):
            u, st["strict"] = _sb_softplus2(st["z"], i * bq + st["r0"], st["off"], masked)
            st["incl"] = _dotf(u.astype(BF16), tri)

        def weigh(st, masked):
            rows = slice(st["r0"], bq)
            lrun = lrun_ref[rows, :]
            w = jnp.exp2(st["z"] - st["incl"] + jnp.tile(lrun, (1, bk // LANES)))
            if masked:
                w = jnp.where(st["strict"], w, 0.0)
            o_ref[rows, :] += _dotf(w.astype(BF16), v_ref[pl.ds(st["off"], bk), :])
            lrun_ref[rows, :] = lrun - _lanes(st["incl"][:, 0:1], LANES)

        _walk_down_staged(i, per, [logits, suffix, weigh], group=4 if per % 4 == 0 else 2)

    return pl.pallas_call(
        body, name="sb_fwd",
        grid=(nh, nq),
        in_specs=[pl.BlockSpec((None, bq, HD), lambda h, i: (h, i, 0)),
                  pl.BlockSpec((None, s, HD), lambda h, i: (h, 0, 0)),
                  pl.BlockSpec((None, s, HD), lambda h, i: (h, 0, 0))],
        out_specs=pl.BlockSpec((None, bq, HD), lambda h, i: (h, i, 0)),
        out_shape=jax.ShapeDtypeStruct((nh, s, HD), F32),
        scratch_shapes=[pltpu.VMEM((bq, LANES), F32)],
        compiler_params=_cparams(("arbitrary", "arbitrary"), 40),
    )(qs, k, v)


def _attn_bwd_call(name, body, s, bq, ins, in_specs, extra_out_specs, extra_out_shapes, extra_scratch=()):
    nh = NH
    nq = s // bq
    return pl.pallas_call(
        body, name=name,
        grid=(nh, nq),
        in_specs=in_specs,
        out_specs=[pl.BlockSpec((None, bq, HD), lambda h, i: (h, i, 0)),
                   pl.BlockSpec(memory_space=pl.ANY), pl.BlockSpec(memory_space=pl.ANY)] + extra_out_specs,
        out_shape=[jax.ShapeDtypeStruct((nh, s, HD), F32), jax.ShapeDtypeStruct((nh, s, HD), F32),
                   jax.ShapeDtypeStruct((nh, s, HD), F32)] + extra_out_shapes,
        scratch_shapes=[pltpu.VMEM((s, HD), F32), pltpu.VMEM((s, HD), F32), pltpu.SemaphoreType.DMA((2,))] + list(extra_scratch),
        compiler_params=_cparams(("arbitrary", "arbitrary"), 52),
    )(*ins)


def _flush_dkv(i, nq, h, dk_acc, dv_acc, dk_hbm, dv_hbm, sems):
    @pl.when(i == nq - 1)
    def _():
        ck = pltpu.make_async_copy(dk_acc, dk_hbm.at[h], sems.at[0])
        cv = pltpu.make_async_copy(dv_acc, dv_hbm.at[h], sems.at[1])
        ck.start()
        cv.start()
        ck.wait()
        cv.wait()


def fox_bwd(qs, kn, v, nc, lse, do, delta, bq, bk):
    nh, s, _ = qs.shape
    nq = s // bq
    per = bq // bk

    def body(q_ref, k_ref, v_ref, nc_ref, lse_ref, do_ref, dl_ref, dq_ref, dk_hbm, dv_hbm, dnc_ref, drow_ref, dk_acc, dv_acc, sems):
        h, i = pl.program_id(0), pl.program_id(1)

        @pl.when(i == 0)
        def _():
            dk_acc[...] = jnp.zeros_like(dk_acc)
            dv_acc[...] = jnp.zeros_like(dv_acc)
            dnc_ref[...] = jnp.zeros_like(dnc_ref)

        q = q_ref[...]
        do_t = do_ref[...]
        lse_t = lse_ref[...]
        dl_t = dl_ref[...]

        dq_ref[...] = jnp.zeros_like(dq_ref)
        drow_ref[...] = jnp.zeros_like(drow_ref)

        def logits(st, masked):
            st["off"] = pl.multiple_of(st["j"] * bk, bk)
            st["r0"] = st["band"] * bk
            st["sc"] = _dot_nt(q[st["r0"]:], k_ref[pl.ds(st["off"], bk), :]) + nc_ref[:, pl.ds(st["off"], bk)]
            st["dp"] = _dot_nt(do_t[st["r0"]:], v_ref[pl.ds(st["off"], bk), :])

        def grads(st, masked):
            off, r0 = st["off"], st["r0"]
            rows = slice(r0, bq)
            p = jnp.exp2(st["sc"] - lse_t[rows])
            if masked:
                row = i * bq + r0 + lax.broadcasted_iota(jnp.int32, p.shape, 0)
                col = off + lax.broadcasted_iota(jnp.int32, p.shape, 1)
                p = jnp.where(col <= row, p, 0.0)
            ds = p * (st["dp"] - dl_t[rows])
            dsb = ds.astype(BF16)
            dq_ref[rows, :] += _dotf(dsb, k_ref[pl.ds(off, bk), :])
            dk_acc[pl.ds(off, bk), :] += _dot_tn(dsb, q[rows])
            dv_acc[pl.ds(off, bk), :] += _dot_tn(p.astype(BF16), do_t[rows])
            dnc_ref[:, pl.ds(off, bk)] += jnp.sum(ds, axis=0, keepdims=True)
            drow_ref[rows, :] += jnp.sum(ds, axis=1, keepdims=True)

        _walk_up_staged(i, per, [logits, grads])
        _flush_dkv(i, nq, h, dk_acc, dv_acc, dk_hbm, dv_hbm, sems)

    tile_spec = pl.BlockSpec((None, bq, HD), lambda h, i: (h, i, 0))
    col_spec = pl.BlockSpec((None, bq, 1), lambda h, i: (h, i, 0))
    full_spec = pl.BlockSpec((None, s, HD), lambda h, i: (h, 0, 0))
    row_spec = pl.BlockSpec((None, 1, s), lambda h, i: (h, 0, 0))
    return _attn_bwd_call("fox_bwd", body, s, bq, (qs, kn, v, nc, lse, do, delta),
                          [tile_spec, full_spec, full_spec, row_spec, col_spec, tile_spec, col_spec],
                          [row_spec, col_spec],
                          [jax.ShapeDtypeStruct((nh, 1, s), F32), jax.ShapeDtypeStruct((nh, s, 1), F32)])


def sb_bwd(qs, k, v, do, delta, bq, bk):
    nh, s, _ = qs.shape
    nq = s // bq
    per = bq // bk

    def body(q_ref, k_ref, v_ref, do_ref, dl_ref, dq_ref, dk_hbm, dv_hbm, dk_acc, dv_acc, sems, lrun_ref, crun_ref):
        h, i = pl.program_id(0), pl.program_id(1)

        @pl.when(i == 0)
        def _():
            dk_acc[...] = jnp.zeros_like(dk_acc)
            dv_acc[...] = jnp.zeros_like(dv_acc)

        q = q_ref[...]
        do_t = do_ref[...]
        tri = _suffix_matrix(bk)
        tri2 = jnp.concatenate([tri, tri], axis=0)
        dq_ref[...] = jnp.zeros_like(dq_ref)
        lrun_ref[...] = jnp.zeros_like(lrun_ref)
        crun_ref[...] = _lanes(dl_ref[...], LANES)

        def logits(st, masked):
            st["off"] = pl.multiple_of(st["j"] * bk, bk)
            st["rows"] = slice(st["band"] * bk, bq)
            st["z"] = _dot_nt(q[st["rows"]], k_ref[pl.ds(st["off"], bk), :])
            st["dw"] = _dot_nt(do_t[st["rows"]], v_ref[pl.ds(st["off"], bk), :])

        def suffix(st, masked):
            st["u"], st["strict"] = _sb_softplus2(st["z"], i * bq + st["rows"].start, st["off"], masked)
            st["incl"] = _dotf(st["u"].astype(BF16), tri)

        def weigh(st, masked):
            lrun = lrun_ref[st["rows"], :]
            w = jnp.exp2(st["z"] - st["incl"] + jnp.tile(lrun, (1, bk // LANES)))
            if masked:
                w = jnp.where(st["strict"], w, 0.0)
            st["wb"] = w.astype(BF16)
            st["e"] = st["dw"] * st["wb"].astype(F32)
            st["einc"] = _suffix2(st["e"], tri2)
            lrun_ref[st["rows"], :] = lrun - _lanes(st["incl"][:, 0:1], LANES)

        def grads(st, masked):
            rows = st["rows"]
            crun = crun_ref[rows, :]
            prefix = jnp.tile(crun, (1, bk // LANES)) - st["einc"]
            dz = st["e"] - jnp.exp2(st["z"] - st["u"]) * (st["e"] + prefix)
            if masked:
                dz = jnp.where(st["strict"], dz, 0.0)
            dzb = dz.astype(BF16)
            dq_ref[rows, :] += _dotf(dzb, k_ref[pl.ds(st["off"], bk), :])
            dk_acc[pl.ds(st["off"], bk), :] += _dot_tn(dzb, q[rows])
            dv_acc[pl.ds(st["off"], bk), :] += _dot_tn(st["wb"], do_t[rows])
            crun_ref[rows, :] = crun - _lanes(st["einc"][:, 0:1], LANES)

        _walk_down_staged(i, per, [logits, suffix, weigh, grads])
        _flush_dkv(i, nq, h, dk_acc, dv_acc, dk_hbm, dv_hbm, sems)

    tile_spec = pl.BlockSpec((None, bq, HD), lambda h, i: (h, i, 0))
    col_spec = pl.BlockSpec((None, bq, 1), lambda h, i: (h, i, 0))
    full_spec = pl.BlockSpec((None, s, HD), lambda h, i: (h, 0, 0), pipeline_mode=pl.Buffered(1))
    return _attn_bwd_call("sb_bwd", body, s, bq, (qs, k, v, do, delta),
                          [tile_spec, full_spec, full_spec, tile_spec, col_spec], [], [],
                          [pltpu.VMEM((bq, LANES), F32), pltpu.VMEM((bq, LANES), F32)])


BI = SPAN
PAIR = 2 * HD2


def _slopes(g):
    return [float(2.0 ** (-8.0 * (g * NH2 + h + 1) / (NG * NH2))) for h in range(NH2)]


def _head_lanes(hh):
    return (lax.broadcasted_iota(jnp.int32, (1, PAIR), 1) // HD2) == hh


def _dil_masks(n, d):
    a = lax.broadcasted_iota(jnp.int32, (BI, 2 * BI), 0)
    c = lax.broadcasted_iota(jnp.int32, (BI, 2 * BI), 1)
    dist = a - c + BI
    valid = (dist >= 0) & (dist <= SPAN) & ((c >= BI) | (n > 0))
    return valid, (dist * d).astype(F32)


def _dil_pair_fwd(qp, kcat, vcat, valid, distf, slopes2):
    o_pair = jnp.zeros((BI, PAIR), F32)
    lse_pair = jnp.zeros((BI, PAIR), F32)
    for hh in range(2):
        lm = _head_lanes(hh)
        qm = jnp.where(lm, qp, jnp.zeros_like(qp))
        logits = jnp.where(valid, _dot_nt(qm, kcat) - slopes2[hh] * distf, -jnp.inf)
        m = jnp.max(logits, axis=1, keepdims=True)
        p = jnp.exp(logits - m)
        den = jnp.sum(p, axis=1, keepdims=True)
        o_pair = jnp.where(lm, _dotf(p.astype(BF16), vcat) / den, o_pair)
        lse_pair = jnp.where(lm, m + jnp.log(den), lse_pair)
    return o_pair, lse_pair


def _dil_pair_bwd(qc, kcat, vcat, doc, lse_c, dl_c, valid, distf, slopes2):
    dq_pair = jnp.zeros((BI, PAIR), F32)
    dk_cat = jnp.zeros((2 * BI, PAIR), F32)
    dv_cat = jnp.zeros((2 * BI, PAIR), F32)
    for hh in range(2):
        col = slice(hh * HD2, hh * HD2 + 1)
        lm = _head_lanes(hh)
        zq = jnp.zeros_like(qc)
        qm = jnp.where(lm, qc, zq)
        dom = jnp.where(lm, doc, zq)
        logits = _dot_nt(qm, kcat) - slopes2[hh] * distf
        p = jnp.exp(jnp.where(valid, logits, -jnp.inf) - lse_c[:, col])
        ds = (p * (_dot_nt(dom, vcat) - dl_c[:, col])).astype(BF16)
        dq_pair = jnp.where(lm, _dotf(ds, kcat), dq_pair)
        dk_cat = dk_cat + _dot_tn(ds, qm)
        dv_cat = dv_cat + _dot_tn(p.astype(BF16), dom)
    return dq_pair, dk_cat, dv_cat


def _pair_slopes(slopes, hp):
    out = []
    for hh in range(2):
        acc = jnp.float32(slopes[hh])
        for t in range(1, NH2 // 2):
            acc = jnp.where(hp == t, jnp.float32(slopes[2 * t + hh]), acc)
        out.append(acc)
    return out


def _for_residues(d, residue):
    per_trip = min(d, 4)

    def trip(t, carry):
        for u in range(per_trip):
            residue(t * per_trip + u)
        return carry

    if d == per_trip:
        trip(0, 0)
    else:
        lax.fori_loop(0, d // per_trip, trip, 0)


def _dil_tiling(d):
    return NH2 // 2 if d == 1 else 1


def dil_fwd(q, k, v, g):
    d = DILATIONS[g]
    s = q.shape[1]
    rows_per = BI * d
    nblk = s // rows_per
    tiles = _dil_tiling(d)
    slopes = _slopes(g)

    def body(q_ref, kc_ref, kp_ref, vc_ref, vp_ref, o_ref, lse_ref):
        n = pl.program_id(0)
        valid, distf = _dil_masks(n, d)
        for t in range(tiles):
            sl = slice(t * PAIR, (t + 1) * PAIR)
            slopes2 = _pair_slopes(slopes, pl.program_id(1) * tiles + t)

            def residue(r):
                rows = pl.ds(r, BI, stride=d)
                kcat = jnp.concatenate([kp_ref[rows, sl], kc_ref[rows, sl]], axis=0).astype(BF16)
                vcat = jnp.concatenate([vp_ref[rows, sl], vc_ref[rows, sl]], axis=0).astype(BF16)
                o_ref[rows, sl], lse_ref[rows, sl] = _dil_pair_fwd(q_ref[rows, sl].astype(BF16), kcat, vcat, valid, distf, slopes2)

            _for_residues(d, residue)

    width = tiles * PAIR
    cur = pl.BlockSpec((None, rows_per, width), lambda n, hp: (g, n, hp))
    prev = pl.BlockSpec((None, rows_per, width), lambda n, hp: (g, jnp.maximum(n - 1, 0), hp))
    out = pl.BlockSpec((rows_per, width), lambda n, hp: (n, hp))
    return pl.pallas_call(
        body, name=f"dil_fwd_{g}",
        grid=(nblk, ODD_W // width),
        in_specs=[cur, cur, prev, cur, prev],
        out_specs=[out, out],
        out_shape=[jax.ShapeDtypeStruct((s, ODD_W), F32), jax.ShapeDtypeStruct((s, ODD_W), F32)],
        compiler_params=_cparams(("parallel", "parallel"), 40),
    )(q, k, k, v, v)


def dil_bwd(q, k, v, do, lse, delta, g):
    d = DILATIONS[g]
    s = q.shape[1]
    rows_per = BI * d
    nblk = s // rows_per
    tiles = _dil_tiling(d)
    slopes = _slopes(g)

    def body(q_ref, kc_ref, kp_ref, vc_ref, vp_ref, do_ref, l_ref, d_ref, dq_ref, dk_ref, dv_ref, dk_carry, dv_carry):
        first_tile, n = pl.program_id(0) * tiles, pl.program_id(1)

        @pl.when(n == 0)
        def _():
            dk_carry[...] = jnp.zeros_like(dk_carry)
            dv_carry[...] = jnp.zeros_like(dv_carry)

        @pl.when(n < nblk)
        def _():
            valid, distf = _dil_masks(n, d)
            for t in range(tiles):
                sl = slice(t * PAIR, (t + 1) * PAIR)
                slopes2 = _pair_slopes(slopes, first_tile + t)

                def residue(r):
                    rows = pl.ds(r, BI, stride=d)
                    kcat = jnp.concatenate([kp_ref[rows, sl], kc_ref[rows, sl]], axis=0).astype(BF16)
                    vcat = jnp.concatenate([vp_ref[rows, sl], vc_ref[rows, sl]], axis=0).astype(BF16)
                    dq, dk_cat, dv_cat = _dil_pair_bwd(q_ref[rows, sl].astype(BF16), kcat, vcat, do_ref[rows, sl].astype(BF16),
                                                       l_ref[rows, sl], d_ref[rows, sl], valid, distf, slopes2)
                    dq_ref[rows, sl] = dq
                    dk_ref[rows, sl] = dk_carry[rows, sl] + dk_cat[:BI]
                    dv_ref[rows, sl] = dv_carry[rows, sl] + dv_cat[:BI]
                    dk_carry[rows, sl] = dk_cat[BI:]
                    dv_carry[rows, sl] = dv_cat[BI:]

                _for_residues(d, residue)

        @pl.when(n == nblk)
        def _():
            dk_ref[...] = dk_carry[...]
            dv_ref[...] = dv_carry[...]

    width = tiles * PAIR
    cur_idx = lambda n: jnp.minimum(n, nblk - 1)
    prv_idx = lambda n: jnp.maximum(n - 1, 0)
    cur3 = pl.BlockSpec((None, rows_per, width), lambda hp, n: (g, cur_idx(n), hp))
    prv3 = pl.BlockSpec((None, rows_per, width), lambda hp, n: (g, prv_idx(n), hp))
    cur2 = pl.BlockSpec((rows_per, width), lambda hp, n: (cur_idx(n), hp))
    prv2 = pl.BlockSpec((rows_per, width), lambda hp, n: (prv_idx(n), hp))
    shape = jax.ShapeDtypeStruct((s, ODD_W), F32)
    return pl.pallas_call(
        body, name=f"dil_bwd_{g}",
        grid=(ODD_W // width, nblk + 1),
        in_specs=[cur3, cur3, prv3, cur3, prv3, cur2, cur2, cur2],
        out_specs=[cur2, prv2, prv2],
        out_shape=[shape, shape, shape],
        scratch_shapes=[pltpu.VMEM((rows_per, width), F32), pltpu.VMEM((rows_per, width), F32)],
        compiler_params=_cparams(("parallel", "arbitrary"), 48),
    )(q, k, k, v, v, do, lse, delta)


TM = 256


def _rows(tm, w):
    return pl.BlockSpec((tm, w), lambda i: (i, 0))


def _whole(shape):
    return pl.BlockSpec(shape, lambda i: (0,) * len(shape))


def _heads(tm):
    return pl.BlockSpec((NH, tm, HD), lambda i: (0, i, 0))


def _groups(tm):
    return pl.BlockSpec((NG, tm, ODD_W), lambda i: (0, i, 0))


def _rms(x):
    return lax.rsqrt(jnp.mean(x * x, axis=1, keepdims=True) + RMS_EPS)


def _seg_rms(q, hd):
    return lax.rsqrt(_segsum(q * q, hd) * (1.0 / hd) + RMS_EPS)


def _seg_rms_bwd(q_raw, dqs, gain, scale, hd):
    q = q_raw.astype(F32)
    r = _seg_rms(q, hd)
    qhat = q * r
    u = dqs * (gain * scale)
    dq = r * (u - qhat * (_segsum(u * qhat, hd) * (1.0 / hd)))
    return dq, jnp.sum(dqs * qhat, axis=0, keepdims=True) * scale


def _rms_bwd(x, dh, gain):
    r = _rms(x)
    xhat = x * r
    u = dh * gain
    dx = r * (u - xhat * jnp.mean(u * xhat, axis=1, keepdims=True))
    return dx, jnp.sum(dh * xhat, axis=0, keepdims=True)


def even_in_fwd(x, gnorm, w_pack, bf_pad, gq, gk):
    s = x.shape[0]

    def body(x_ref, g_ref, w_ref, bf_ref, gq_ref, gk_ref,
             h_ref, fqs_ref, fkn_ref, fv_ref, fqr_ref, fkr_ref, flog_ref, sqs_ref, sk_ref, sv_ref, gate_ref):
        xt = x_ref[...]
        h = (xt * _rms(xt) * g_ref[...]).astype(BF16)
        h_ref[...] = h
        proj = _dotf(h, w_ref[...])
        fq = proj[:, 0:512]
        fk = proj[:, 512:1024]
        fqs = fq * _seg_rms(fq, HD) * (gq_ref[...] * (SCALE_E * LOG2E))
        fkn = fk * _seg_rms(fk, HD) * gk_ref[...]
        flog_ref[...] = proj[:, FL_OFF:FL_OFF + LANES] + bf_ref[...]
        o = FL_OFF + LANES
        for hh in range(NH):
            sl = slice(hh * HD, (hh + 1) * HD)
            fqs_ref[hh] = fqs[:, sl].astype(BF16)
            fkn_ref[hh] = fkn[:, sl].astype(BF16)
            fqr_ref[hh] = fq[:, sl].astype(BF16)
            fkr_ref[hh] = fk[:, sl].astype(BF16)
            fv_ref[hh] = proj[:, 1024 + hh * HD:1024 + (hh + 1) * HD].astype(BF16)
            sqs_ref[hh] = (proj[:, o + hh * HD:o + (hh + 1) * HD] * (SCALE_E * LOG2E)).astype(BF16)
            sk_ref[hh] = proj[:, o + 512 + hh * HD:o + 512 + (hh + 1) * HD].astype(BF16)
            sv_ref[hh] = proj[:, o + 1024 + hh * HD:o + 1024 + (hh + 1) * HD].astype(BF16)
        gate_ref[...] = proj[:, o + 1536:o + 2560].astype(BF16)

    hs = jax.ShapeDtypeStruct((NH, s, HD), BF16)
    return pl.pallas_call(
        body, name="even_in_fwd",
        grid=(s // TM,),
        in_specs=[_rows(TM, D_MODEL), _whole((1, D_MODEL)), _whole((D_MODEL, EVEN_PACK)), _whole((1, LANES)),
                  _whole((1, 512)), _whole((1, 512))],
        out_specs=[_rows(TM, D_MODEL)] + [_heads(TM)] * 5 + [_rows(TM, LANES)] + [_heads(TM)] * 3 + [_rows(TM, EVEN_W)],
        out_shape=[jax.ShapeDtypeStruct((s, D_MODEL), BF16)] + [hs] * 5 + [jax.ShapeDtypeStruct((s, LANES), F32)]
        + [hs] * 3 + [jax.ShapeDtypeStruct((s, EVEN_W), BF16)],
        compiler_params=_cparams(("parallel",), 52),
    )(x, gnorm, w_pack, bf_pad, gq, gk)


def _prefix_matrices(r):
    a = lax.broadcasted_iota(jnp.int32, (LANES, LANES), 0)
    b = lax.broadcasted_iota(jnp.int32, (LANES, LANES), 1)
    ra = lax.broadcasted_iota(jnp.int32, (r, r), 0)
    rb = lax.broadcasted_iota(jnp.int32, (r, r), 1)
    return a, b, ra, rb


def _dot3_right(x, m):
    a, b, c = _split3(x)
    return _dotf(a, m) + _dotf(b, m) + _dotf(c, m)


def _dot3_left(m, x):
    a, b, c = _split3(x)
    return _dotf(m, a) + _dotf(m, b) + _dotf(m, c)


def fox_cum(flog4):
    nh, r, _ = flog4.shape

    def body(f_ref, nc_ref):
        z = f_ref[...]
        lf = jnp.minimum(z, 0.0) - jnp.log(1.0 + jnp.exp(-jnp.abs(z)))
        a, b, ra, rb = _prefix_matrices(r)
        within = _dot3_right(lf, (a <= b).astype(BF16))
        tot = jnp.broadcast_to(within[:, LANES - 1:LANES], (r, LANES))
        nc_ref[...] = (within + _dot3_left((rb < ra).astype(BF16), tot)) * (-LOG2E)

    return pl.pallas_call(
        body, name="fox_cum", grid=(nh,),
        in_specs=[pl.BlockSpec((None, r, LANES), lambda h: (h, 0, 0))],
        out_specs=pl.BlockSpec((None, r, LANES), lambda h: (h, 0, 0)),
        out_shape=jax.ShapeDtypeStruct((nh, r, LANES), F32),
        compiler_params=_cparams(("parallel",), 16),
    )(flog4)


def fox_cum_bwd(dcum4, flog4):
    nh, r, _ = flog4.shape

    def body(d_ref, f_ref, o_ref):
        a, b, ra, rb = _prefix_matrices(r)
        dc = d_ref[...]
        within = _dot3_right(dc, (a >= b).astype(BF16))
        tot = jnp.broadcast_to(within[:, 0:1], (r, LANES))
        dlf = within + _dot3_left((rb > ra).astype(BF16), tot)
        o_ref[...] = dlf / (1.0 + jnp.exp(f_ref[...]))

    spec = pl.BlockSpec((None, r, LANES), lambda h: (h, 0, 0))
    return pl.pallas_call(
        body, name="fox_cum_bwd", grid=(nh,),
        in_specs=[spec, spec], out_specs=spec,
        out_shape=jax.ShapeDtypeStruct((nh, r, LANES), F32),
        compiler_params=_cparams(("parallel",), 16),
    )(dcum4, flog4)


def even_out_fwd(fo, so, gate, x, w_out):
    s = x.shape[0]
    tm = 2 * TM

    def body(fo_ref, so_ref, g_ref, x_ref, w_ref, y_ref):
        sg = _silu(g_ref[...].astype(F32))
        acc = x_ref[...]
        for hh in range(NH):
            mf = (fo_ref[hh].astype(F32) * sg[:, hh * HD:(hh + 1) * HD]).astype(BF16)
            ms = (so_ref[hh] * sg[:, 512 + hh * HD:512 + (hh + 1) * HD]).astype(BF16)
            acc = acc + _dotf(mf, w_ref[hh * HD:(hh + 1) * HD, :]) + _dotf(ms, w_ref[512 + hh * HD:512 + (hh + 1) * HD, :])
        y_ref[...] = acc

    return pl.pallas_call(
        body, name="even_out_fwd", grid=(s // tm,),
        in_specs=[_heads(tm), _heads(tm), _rows(tm, EVEN_W), _rows(tm, D_MODEL), _whole((EVEN_W, D_MODEL))],
        out_specs=_rows(tm, D_MODEL),
        out_shape=jax.ShapeDtypeStruct((s, D_MODEL), F32),
        compiler_params=_cparams(("parallel",), 40),
    )(fo, so, gate, x, w_out)


def odd_in_fwd(y1, gnorm, w2, gq, gk):
    s = y1.shape[0]

    def body(x_ref, g_ref, w_ref, gq_ref, gk_ref, h_ref, qs_ref, kn_ref, v_ref, qr_ref, kr_ref, gate_ref):
        xt = x_ref[...]
        h = (xt * _rms(xt) * g_ref[...]).astype(BF16)
        h_ref[...] = h
        proj = _dotf(h, w_ref[...])
        for g in range(NG):
            q = proj[:, g * ODD_W:(g + 1) * ODD_W]
            k = proj[:, 1536 + g * ODD_W:1536 + (g + 1) * ODD_W]
            qs_ref[g] = q * _seg_rms(q, HD2) * (gq_ref[...] * SCALE_O)
            kn_ref[g] = k * _seg_rms(k, HD2) * gk_ref[...]
            qr_ref[g] = q.astype(BF16)
            kr_ref[g] = k.astype(BF16)
            v_ref[g] = proj[:, 3072 + g * ODD_W:3072 + (g + 1) * ODD_W]
        gate_ref[...] = proj[:, 4608:5120].astype(BF16)

    gs = lambda dt: jax.ShapeDtypeStruct((NG, s, ODD_W), dt)
    return pl.pallas_call(
        body, name="odd_in_fwd", grid=(s // TM,),
        in_specs=[_rows(TM, D_MODEL), _whole((1, D_MODEL)), _whole((D_MODEL, ODD_IN)), _whole((1, ODD_W)), _whole((1, ODD_W))],
        out_specs=[_rows(TM, D_MODEL)] + [_groups(TM)] * 5 + [_rows(TM, ODD_W)],
        out_shape=[jax.ShapeDtypeStruct((s, D_MODEL), BF16), gs(F32), gs(F32), gs(F32), gs(BF16), gs(BF16),
                   jax.ShapeDtypeStruct((s, ODD_W), BF16)],
        compiler_params=_cparams(("parallel",), 52),
    )(y1, gnorm, w2, gq, gk)


def odd_out_fwd(o0, l0, o1, l1, o2, l2, gate2, y1, target, w_out2):
    s = y1.shape[0]
    tm = 2 * TM
    nt = s // tm

    def body(o0_ref, l0_ref, o1_ref, l1_ref, o2_ref, l2_ref, g_ref, y1_ref, t_ref, w_ref,
             att_ref, lse_ref, dy_ref, loss_ref):
        l0t, l1t, l2t = l0_ref[...], l1_ref[...], l2_ref[...]
        m = jnp.maximum(jnp.maximum(l0t, l1t), l2t)
        e0, e1, e2 = jnp.exp(l0t - m), jnp.exp(l1t - m), jnp.exp(l2t - m)
        den = e0 + e1 + e2
        att = (e0 * o0_ref[...] + e1 * o1_ref[...] + e2 * o2_ref[...]) / den
        att_ref[...] = att.astype(BF16)
        lse_ref[...] = m + jnp.log(den)
        mixed = (att * _silu(g_ref[...].astype(F32))).astype(BF16)
        diff = y1_ref[...] + _dotf(mixed, w_ref[...]) - t_ref[...]
        dy_ref[...] = diff * (1.0 / D_MODEL)
        loss_ref[...] = jnp.full((1, 1, LANES), 0.5 / D_MODEL, F32) * jnp.sum(diff * diff)

    big = jax.ShapeDtypeStruct((s, ODD_W), F32)
    return pl.pallas_call(
        body, name="odd_out_fwd", grid=(nt,),
        in_specs=[_rows(tm, ODD_W)] * 7 + [_rows(tm, D_MODEL), _rows(tm, D_MODEL), _whole((ODD_W, D_MODEL))],
        out_specs=[_rows(tm, ODD_W), _rows(tm, ODD_W), _rows(tm, D_MODEL), pl.BlockSpec((1, 1, LANES), lambda i: (i, 0, 0))],
        out_shape=[jax.ShapeDtypeStruct((s, ODD_W), BF16), big, jax.ShapeDtypeStruct((s, D_MODEL), F32),
                   jax.ShapeDtypeStruct((nt, 1, LANES), F32)],
        compiler_params=_cparams(("parallel",), 40),
    )(o0, l0, o1, l1, o2, l2, gate2, y1, target, w_out2)


def odd_out_bwd(dy2, w_out2_t, att, gate2):
    s = dy2.shape[0]
    tm = 2 * TM

    def body(dy_ref, wt_ref, att_ref, g_ref, datt_ref, dgate_ref, delta_ref, dw_ref):
        @pl.when(pl.program_id(0) == 0)
        def _():
            dw_ref[...] = jnp.zeros_like(dw_ref)

        dyb = dy_ref[...].astype(BF16)
        dmixed = _dotf(dyb, wt_ref[...])
        g = g_ref[...].astype(F32)
        att_t = att_ref[...].astype(F32)
        sg = _silu(g)
        datt = (dmixed * sg).astype(BF16).astype(F32)
        datt_ref[...] = datt
        dgate_ref[...] = (dmixed * att_t * _dsilu(g)).astype(BF16)
        delta_ref[...] = _segsum(datt * att_t, HD2)
        dw_ref[...] += _dot_tn((att_t * sg).astype(BF16), dyb)

    return pl.pallas_call(
        body, name="odd_out_bwd", grid=(s // tm,),
        in_specs=[_rows(tm, D_MODEL), _whole((D_MODEL, ODD_W)), _rows(tm, ODD_W), _rows(tm, ODD_W)],
        out_specs=[_rows(tm, ODD_W), _rows(tm, ODD_W), _rows(tm, ODD_W), _whole((ODD_W, D_MODEL))],
        out_shape=[jax.ShapeDtypeStruct((s, ODD_W), F32), jax.ShapeDtypeStruct((s, ODD_W), BF16),
                   jax.ShapeDtypeStruct((s, ODD_W), F32), jax.ShapeDtypeStruct((ODD_W, D_MODEL), F32)],
        compiler_params=_cparams(("arbitrary",), 40),
    )(dy2, w_out2_t, att, gate2)


def odd_in_bwd(dqs, dks, dvs, dgate2, q2r, k2r, gq, gk, w2_t, y1, dy2, gnorm):
    s = y1.shape[0]

    def body(dq0, dq1, dq2, dk0, dk1, dk2, dv0, dv1, dv2, dg_ref, qr_ref, kr_ref, gq_ref, gk_ref, wt_ref, y1_ref, dy_ref, gn_ref,
             dproj_ref, dy1_ref, dgn_ref, dgain_ref):
        @pl.when(pl.program_id(0) == 0)
        def _():
            dgn_ref[...] = jnp.zeros_like(dgn_ref)
            dgain_ref[...] = jnp.zeros_like(dgain_ref)

        for g, (dq_ref, dk_ref, dv_ref) in enumerate(((dq0, dk0, dv0), (dq1, dk1, dv1), (dq2, dk2, dv2))):
            dq, gq_row = _seg_rms_bwd(qr_ref[g], dq_ref[...], gq_ref[...], SCALE_O, HD2)
            dk, gk_row = _seg_rms_bwd(kr_ref[g], dk_ref[...], gk_ref[...], 1.0, HD2)
            dproj_ref[:, g * ODD_W:(g + 1) * ODD_W] = dq.astype(BF16)
            dproj_ref[:, 1536 + g * ODD_W:1536 + (g + 1) * ODD_W] = dk.astype(BF16)
            dproj_ref[:, 3072 + g * ODD_W:3072 + (g + 1) * ODD_W] = dv_ref[...].astype(BF16)
            dgain_ref[g:g + 1, :] += gq_row
            dgain_ref[NG + g:NG + g + 1, :] += gk_row
        dproj_ref[:, 4608:5120] = dg_ref[...]
        dh = _dotf(dproj_ref[...], wt_ref[...])
        dx, gn_row = _rms_bwd(y1_ref[...], dh, gn_ref[...])
        dy1_ref[...] = dy_ref[...] + dx
        dgn_ref[...] += gn_row

    f32r, bf16r = _rows(TM, ODD_W), _rows(TM, ODD_W)
    return pl.pallas_call(
        body, name="odd_in_bwd", grid=(s // TM,),
        in_specs=[f32r] * 6 + [bf16r] * 4 + [_groups(TM), _groups(TM), _whole((1, ODD_W)), _whole((1, ODD_W)),
                                             _whole((ODD_IN, D_MODEL)), _rows(TM, D_MODEL), _rows(TM, D_MODEL), _whole((1, D_MODEL))],
        out_specs=[_rows(TM, ODD_IN), _rows(TM, D_MODEL), _whole((1, D_MODEL)), _whole((8, ODD_W))],
        out_shape=[jax.ShapeDtypeStruct((s, ODD_IN), BF16), jax.ShapeDtypeStruct((s, D_MODEL), F32),
                   jax.ShapeDtypeStruct((1, D_MODEL), F32), jax.ShapeDtypeStruct((8, ODD_W), F32)],
        compiler_params=_cparams(("arbitrary",), 52),
    )(*dqs, *dks, *dvs, dgate2, q2r, k2r, gq, gk, w2_t, y1, dy2, gnorm)


def even_out_bwd(dy1, w_out_t, fo, so, gate):
    s = dy1.shape[0]
    tm = 2 * TM

    def body(dy_ref, wt_ref, fo_ref, so_ref, g_ref, dfo_ref, dso_ref, dgate_ref, delf_ref, dels_ref, dw_ref):
        @pl.when(pl.program_id(0) == 0)
        def _():
            dw_ref[...] = jnp.zeros_like(dw_ref)

        dyb = dy_ref[...].astype(BF16)
        dmixed = _dotf(dyb, wt_ref[...])
        g = g_ref[...].astype(F32)
        sg, dsg = _silu(g), _dsilu(g)
        for hh in range(NH):
            for base, o_ref, do_ref, del_ref in ((0, fo_ref, dfo_ref, delf_ref), (512, so_ref, dso_ref, dels_ref)):
                sl = slice(base + hh * HD, base + (hh + 1) * HD)
                o = o_ref[hh].astype(F32)
                do = (dmixed[:, sl] * sg[:, sl]).astype(BF16)
                do_ref[hh] = do
                del_ref[hh] = jnp.sum(do.astype(F32) * o, axis=1, keepdims=True)
                dgate_ref[:, sl] = (dmixed[:, sl] * o * dsg[:, sl]).astype(BF16)
                dw_ref[sl, :] += _dot_tn((o * sg[:, sl]).astype(BF16), dyb)

    cols = pl.BlockSpec((NH, tm, 1), lambda i: (0, i, 0))
    hs = jax.ShapeDtypeStruct((NH, s, HD), BF16)
    cs = jax.ShapeDtypeStruct((NH, s, 1), F32)
    return pl.pallas_call(
        body, name="even_out_bwd", grid=(s // tm,),
        in_specs=[_rows(tm, D_MODEL), _whole((D_MODEL, EVEN_W)), _heads(tm), _heads(tm), _rows(tm, EVEN_W)],
        out_specs=[_heads(tm), _heads(tm), _rows(tm, EVEN_W), cols, cols, _whole((EVEN_W, D_MODEL))],
        out_shape=[hs, hs, jax.ShapeDtypeStruct((s, EVEN_W), BF16), cs, cs, jax.ShapeDtypeStruct((EVEN_W, D_MODEL), F32)],
        compiler_params=_cparams(("arbitrary",), 48),
    )(dy1, w_out_t, fo, so, gate)


def even_in_bwd(dfqs, dfkn, dfv, dsqs, dsk, dsv, dgate, dflog, fqr, fkr, gq, gk, w_pack_t, x, dy1, gnorm):
    s = x.shape[0]

    def body(dfq_ref, dfk_ref, dfv_ref, dsq_ref, dsk_ref, dsv_ref, dg_ref, dfl_ref, qr_ref, kr_ref, gq_ref, gk_ref,
             wt_ref, x_ref, dy_ref, gn_ref, dproj_ref, dx_ref, dgn_ref, dgain_ref, dbf_ref):
        @pl.when(pl.program_id(0) == 0)
        def _():
            dgn_ref[...] = jnp.zeros_like(dgn_ref)
            dgain_ref[...] = jnp.zeros_like(dgain_ref)
            dbf_ref[...] = jnp.zeros_like(dbf_ref)

        o = FL_OFF + LANES
        for hh in range(NH):
            sl = slice(hh * HD, (hh + 1) * HD)
            dq, gq_row = _seg_rms_bwd(qr_ref[hh], dfq_ref[hh], gq_ref[:, sl], SCALE_E, HD)
            dk, gk_row = _seg_rms_bwd(kr_ref[hh], dfk_ref[hh] * LN2, gk_ref[:, sl], 1.0, HD)
            dproj_ref[:, sl] = dq.astype(BF16)
            dproj_ref[:, 512 + hh * HD:512 + (hh + 1) * HD] = dk.astype(BF16)
            dproj_ref[:, 1024 + hh * HD:1024 + (hh + 1) * HD] = dfv_ref[hh].astype(BF16)
            dproj_ref[:, o + hh * HD:o + (hh + 1) * HD] = (dsq_ref[hh] * SCALE_E).astype(BF16)
            dproj_ref[:, o + 512 + hh * HD:o + 512 + (hh + 1) * HD] = (dsk_ref[hh] * LN2).astype(BF16)
            dproj_ref[:, o + 1024 + hh * HD:o + 1024 + (hh + 1) * HD] = dsv_ref[hh].astype(BF16)
            dgain_ref[0:1, sl] += gq_row
            dgain_ref[1:2, sl] += gk_row
        dfl = dfl_ref[...]
        dproj_ref[:, FL_OFF:FL_OFF + LANES] = dfl.astype(BF16)
        dbf_ref[...] += jnp.sum(dfl, axis=0, keepdims=True)
        dproj_ref[:, o + 1536:o + 2560] = dg_ref[...]
        dh = _dotf(dproj_ref[...], wt_ref[...])
        dx, gn_row = _rms_bwd(x_ref[...], dh, gn_ref[...])
        dx_ref[...] = dy_ref[...] + dx
        dgn_ref[...] += gn_row

    return pl.pallas_call(
        body, name="even_in_bwd", grid=(s // TM,),
        in_specs=[_heads(TM)] * 6 + [_rows(TM, EVEN_W), _rows(TM, LANES), _heads(TM), _heads(TM), _whole((1, 512)), _whole((1, 512)),
                                     _whole((EVEN_PACK, D_MODEL)), _rows(TM, D_MODEL), _rows(TM, D_MODEL), _whole((1, D_MODEL))],
        out_specs=[_rows(TM, EVEN_PACK), _rows(TM, D_MODEL), _whole((1, D_MODEL)), _whole((8, 512)), _whole((1, LANES))],
        out_shape=[jax.ShapeDtypeStruct((s, EVEN_PACK), BF16), jax.ShapeDtypeStruct((s, D_MODEL), F32),
                   jax.ShapeDtypeStruct((1, D_MODEL), F32), jax.ShapeDtypeStruct((8, 512), F32), jax.ShapeDtypeStruct((1, LANES), F32)],
        compiler_params=_cparams(("arbitrary",), 52),
    )(dfqs, dfkn, dfv, dsqs, dsk, dsv, dgate, dflog, fqr, fkr, gq, gk, w_pack_t, x, dy1, gnorm)


def matmul_tn(a, b, tm, name):
    s, m = a.shape
    n = b.shape[1]
    tk = 2 * TM
    nk = s // tk

    def body(a_ref, b_ref, o_ref):
        @pl.when(pl.program_id(1) == 0)
        def _():
            o_ref[...] = jnp.zeros_like(o_ref)

        o_ref[...] += _dot_tn(a_ref[...], b_ref[...])

    return pl.pallas_call(
        body, name=name, grid=(m // tm, nk),
        in_specs=[pl.BlockSpec((tk, tm), lambda j, k: (k, j)), pl.BlockSpec((tk, n), lambda j, k: (k, 0))],
        out_specs=pl.BlockSpec((tm, n), lambda j, k: (j, 0)),
        out_shape=jax.ShapeDtypeStruct((m, n), F32),
        compiler_params=_cparams(("parallel", "arbitrary"), 32),
    )(a, b)


def _tile_gain(g, reps):
    return jnp.tile(g.reshape(1, -1), (1, reps))


def local_step(x, target, w_in_e_t, b_f, gq_e, gk_e, gn_e, w_out_e, gn_o, w_in_o_t, gq_o, gk_o, w_out_o, fox_blocks, sb_blocks):
    s = x.shape[0]
    r = s // LANES
    w_pack_t = jnp.concatenate([w_in_e_t[:FL_OFF + NH], jnp.zeros((LANES - NH, D_MODEL), BF16), w_in_e_t[FL_OFF + NH:]], axis=0)
    w_pack, w_in_o = w_pack_t.T, w_in_o_t.T
    bf_pad = jnp.pad(b_f.reshape(1, NH), ((0, 0), (0, LANES - NH)))
    gq512, gk512 = _tile_gain(gq_e, NH), _tile_gain(gk_e, NH)
    gq2, gk2 = _tile_gain(gq_o, NH2), _tile_gain(gk_o, NH2)
    gn_e, gn_o = gn_e.reshape(1, D_MODEL), gn_o.reshape(1, D_MODEL)

    h, fqs, fkn, fv, fqr, fkr, flog, sqs, sk, sv, gate = even_in_fwd(x, gn_e, w_pack, bf_pad, gq512, gk512)
    flog4 = flog[:, :NH].T.reshape(NH, r, LANES)
    nc = fox_cum(flog4).reshape(NH, 1, s)
    fo, lse = fox_fwd(fqs, fkn, fv, nc, *fox_blocks)
    so = sb_fwd(sqs, sk, sv, *sb_blocks)
    y1 = even_out_fwd(fo, so, gate, x, w_out_e)
    h2, q2s, k2n, v2, q2r, k2r, gate2 = odd_in_fwd(y1, gn_o, w_in_o, gq2, gk2)
    ol = [dil_fwd(q2s, k2n, v2, g) for g in range(NG)]
    att, lse2, dy2, loss_parts = odd_out_fwd(ol[0][0], ol[0][1], ol[1][0], ol[1][1], ol[2][0], ol[2][1], gate2, y1, target, w_out_o)
    loss = jnp.sum(loss_parts[:, 0, 0])
    datt, dgate2, delta2, d_w_out_o = odd_out_bwd(dy2, w_out_o.T, att, gate2)
    dqkv = [dil_bwd(q2s, k2n, v2, datt, lse2, delta2, g) for g in range(NG)]
    dproj2, dy1, d_gn_o, dgain_o = odd_in_bwd([t[0] for t in dqkv], [t[1] for t in dqkv], [t[2] for t in dqkv], dgate2,
                                              q2r, k2r, gq2, gk2, w_in_o_t, y1, dy2, gn_o)
    d_w_in_o_t = matmul_tn(dproj2, h2, ODD_IN // 4, "dw_in_odd")
    dfo, dso, dgate, delta_f, delta_s, d_w_out_e = even_out_bwd(dy1, w_out_e.T, fo, so, gate)
    dfqs, dfkn, dfv, dnc, drow = fox_bwd(fqs, fkn, fv, nc, lse, dfo, delta_f, fox_blocks[0], fox_blocks[1] // 2)
    dsqs, dsk, dsv = sb_bwd(sqs, sk, sv, dso, delta_s, *sb_blocks)
    dcum4 = (drow.reshape(NH, s) - dnc.reshape(NH, s)).reshape(NH, r, LANES)
    dflog4 = fox_cum_bwd(dcum4, flog4)
    dflog = jnp.pad(dflog4.reshape(NH, s).T, ((0, 0), (0, LANES - NH)))
    dproj, grad_x, d_gn_e, dgain_e, d_bf = even_in_bwd(dfqs, dfkn, dfv, dsqs, dsk, dsv, dgate, dflog, fqr, fkr, gq512, gk512,
                                                       w_pack_t, x, dy1, gn_e)
    d_w_pack_t = matmul_tn(dproj, h, EVEN_PACK // 3, "dw_in_even")
    d_w_in_e_t = jnp.concatenate([d_w_pack_t[:FL_OFF + NH], d_w_pack_t[FL_OFF + LANES:]], axis=0)
    grads = dict(
        even_norm=d_gn_e.reshape(-1), even_w_in_t=d_w_in_e_t, even_b_f=d_bf[0, :NH],
        even_q_gain=dgain_e[0].reshape(NH, HD).sum(0), even_k_gain=dgain_e[1].reshape(NH, HD).sum(0),
        even_w_out=d_w_out_e, odd_norm=d_gn_o.reshape(-1), odd_w_in_t=d_w_in_o_t,
        odd_q_gain=dgain_o[:NG].reshape(NG * NH2, HD2).sum(0), odd_k_gain=dgain_o[NG:2 * NG].reshape(NG * NH2, HD2).sum(0),
        odd_w_out=d_w_out_o)
    return loss, grad_x, grads


BIG = (("even_w_in_t", (EVEN_IN // 4, D_MODEL)), ("even_w_out", (EVEN_W // 4, D_MODEL)),
       ("odd_w_in_t", (ODD_IN // 4, D_MODEL)), ("odd_w_out", (ODD_W, D_MODEL // 4)))
VECTORS = (("odd_norm", D_MODEL // 4), ("odd_norm_lo", D_MODEL // 4), ("even_norm", D_MODEL), ("even_b_f", NH),
           ("even_q_gain", HD), ("even_k_gain", HD), ("odd_q_gain", HD2), ("odd_k_gain", HD2))
TILE_ROWS = 16


def _block_rows(shape):
    return -(-(shape[0] * shape[1] // D_MODEL) // TILE_ROWS) * TILE_ROWS


PACK_ROWS = sum(_block_rows(shape) for _, shape in BIG) + TILE_ROWS
HALF = PACK_ROWS // 2
assert HALF % TILE_ROWS == 0
HBM = pl.BlockSpec(memory_space=pl.ANY)


GRAD_VECTORS = (("odd_norm", D_MODEL),) + VECTORS[2:]


def _vector_block(parts, layout=VECTORS):
    rows = [jnp.pad(parts[n].reshape(-1), (0, D_MODEL - size)) for n, size in layout]
    return jnp.pad(jnp.stack(rows), ((0, TILE_ROWS - len(layout)), (0, 0)))


def _vectors_of(block, layout=VECTORS):
    return {n: block[i, :size] for i, (n, size) in enumerate(layout)}


def _pack(parts):
    blocks = []
    for n, shape in BIG:
        t = parts[n].reshape(-1, D_MODEL)
        blocks.append(jnp.pad(t, ((0, _block_rows(shape) - t.shape[0]), (0, 0))))
    blocks.append(_vector_block(parts).astype(blocks[0].dtype))
    return jnp.concatenate(blocks, axis=0)


def _unpack(buf):
    out, off = {}, 0
    for n, shape in BIG:
        rows = shape[0] * shape[1] // D_MODEL
        out[n] = buf[off:off + rows].reshape(shape)
        off += _block_rows(shape)
    out.update(_vectors_of(buf[off:off + TILE_ROWS]))
    return out


def _place():
    x, y, c = lax.axis_index("x"), lax.axis_index("y"), lax.axis_index("c")
    return x, y, c, [(1 - x, y), (x, 1 - y), (1 - x, 1 - y)]


def all_gather_shards(mine):
    def body(src_ref, out_ref, send_sems, recv_sems):
        x, y, c, chips = _place()
        me = 2 * x + y
        half = lambda cc: pl.ds(cc * HALF, HALF)

        def copy(k, j, cc, to, src=None):
            dst = out_ref.at[j, half(cc)]
            return pltpu.make_async_remote_copy(src_ref=dst if src is None else src, dst_ref=dst,
                                                send_sem=send_sems.at[k], recv_sem=recv_sems.at[k],
                                                device_id=to, device_id_type=MESH)

        first = [copy(k, me, c, (cx, cy, c), src=src_ref.at[half(c)]) for k, (cx, cy) in enumerate(chips)]
        for cp in first:
            cp.start()
        passed = [copy(3 + k, 2 * cx + cy, c, (x, y, 1 - c)) for k, (cx, cy) in enumerate(chips)]
        for k, (cx, cy) in enumerate(chips):
            copy(k, 2 * cx + cy, c, (x, y, c)).wait_recv()
            passed[k].start()
        for k, (cx, cy) in enumerate(chips):
            copy(3 + k, 2 * cx + cy, 1 - c, (x, y, c)).wait_recv()
        for cp in first + passed:
            cp.wait_send()

    return pl.pallas_call(
        body, name="all_gather_shards",
        in_specs=[HBM], out_specs=HBM,
        out_shape=jax.ShapeDtypeStruct((4, PACK_ROWS, D_MODEL), mine.dtype),
        scratch_shapes=[pltpu.SemaphoreType.DMA((6,)), pltpu.SemaphoreType.DMA((6,))],
    )(mine)


def sibling_swap_halves(g):
    def body(g_ref, a_ref, send_sem, recv_sem):
        x, y, c, _ = _place()
        cp = pltpu.make_async_remote_copy(src_ref=g_ref.at[:, pl.ds((1 - c) * HALF, HALF)], dst_ref=a_ref,
                                          send_sem=send_sem, recv_sem=recv_sem, device_id=(x, y, 1 - c), device_id_type=MESH)
        cp.start()
        cp.wait()

    return pl.pallas_call(
        body, name="sibling_swap_halves",
        in_specs=[HBM], out_specs=HBM,
        out_shape=jax.ShapeDtypeStruct((4, HALF, D_MODEL), g.dtype),
        scratch_shapes=[pltpu.SemaphoreType.DMA, pltpu.SemaphoreType.DMA],
    )(g)


def chip_exchange(p):
    def body(p_ref, b_ref, send_sems, recv_sems):
        x, y, c, chips = _place()
        me = 2 * x + y
        sends = [pltpu.make_async_remote_copy(src_ref=p_ref.at[2 * cx + cy], dst_ref=b_ref.at[me],
                                              send_sem=send_sems.at[k], recv_sem=recv_sems.at[k],
                                              device_id=(cx, cy, c), device_id_type=MESH)
                 for k, (cx, cy) in enumerate(chips)]
        for cp in sends:
            cp.start()
        for k, (cx, cy) in enumerate(chips):
            pltpu.make_async_remote_copy(src_ref=p_ref.at[me], dst_ref=b_ref.at[2 * cx + cy],
                                         send_sem=send_sems.at[k], recv_sem=recv_sems.at[k],
                                         device_id=(cx, cy, c), device_id_type=MESH).wait_recv()
        for cp in sends:
            cp.wait_send()

    return pl.pallas_call(
        body, name="chip_exchange",
        in_specs=[HBM], out_specs=HBM,
        out_shape=jax.ShapeDtypeStruct((4, HALF, D_MODEL), p.dtype),
        scratch_shapes=[pltpu.SemaphoreType.DMA((3,)), pltpu.SemaphoreType.DMA((3,))],
    )(p)


def sibling_swap(mine):
    def body(h_ref, out_ref, send_sem, recv_sem):
        x, y, c, _ = _place()
        cp = pltpu.make_async_remote_copy(src_ref=h_ref, dst_ref=out_ref, send_sem=send_sem, recv_sem=recv_sem,
                                          device_id=(x, y, 1 - c), device_id_type=MESH)
        cp.start()
        cp.wait()

    return pl.pallas_call(
        body, name="sibling_swap",
        in_specs=[HBM], out_specs=HBM,
        out_shape=jax.ShapeDtypeStruct(mine.shape, mine.dtype),
        scratch_shapes=[pltpu.SemaphoreType.DMA, pltpu.SemaphoreType.DMA],
    )(mine)


def _sum_call(name, arrays, rows, out_dtype):
    tr = rows // 5 if rows % (5 * TILE_ROWS) == 0 else rows

    def body(*refs):
        acc = refs[0][...].astype(F32)
        for r in refs[1:-1]:
            acc = acc + r[...].astype(F32)
        refs[-1][...] = acc.astype(out_dtype)

    spec = pl.BlockSpec((tr, D_MODEL), lambda i: (i, 0))
    return pl.pallas_call(
        body, name=name, grid=(rows // tr,),
        in_specs=[spec] * len(arrays), out_specs=spec,
        out_shape=jax.ShapeDtypeStruct((rows, D_MODEL), out_dtype),
        compiler_params=_cparams(("parallel",), 40),
    )(*arrays)


def allreduce_rows(block):
    n_dev = 8

    def body(src_ref, out_ref, slots, send_sems, recv_sems):
        x, y, c = lax.axis_index("x"), lax.axis_index("y"), lax.axis_index("c")
        flip = lambda v, bit: 1 - v if bit else v
        index = lambda px, py, pc: 4 * px + 2 * py + pc
        slots[index(x, y, c)] = src_ref[...]
        peers = [(flip(x, k >> 2 & 1), flip(y, k >> 1 & 1), flip(c, k & 1)) for k in range(1, n_dev)]

        def copy(k, slot, peer):
            return pltpu.make_async_remote_copy(src_ref=src_ref, dst_ref=slots.at[slot], send_sem=send_sems.at[k],
                                                recv_sem=recv_sems.at[k], device_id=peer, device_id_type=MESH)

        sends = [copy(k, index(x, y, c), peer) for k, peer in enumerate(peers)]
        for cp in sends:
            cp.start()
        for k, peer in enumerate(peers):
            copy(k, index(*peer), peer).wait_recv()
        for cp in sends:
            cp.wait_send()
        acc = slots[0]
        for dev in range(1, n_dev):
            acc = acc + slots[dev]
        out_ref[...] = acc

    vmem = pl.BlockSpec(memory_space=pltpu.VMEM)
    return pl.pallas_call(
        body, name="allreduce_rows",
        in_specs=[vmem], out_specs=vmem,
        out_shape=jax.ShapeDtypeStruct(block.shape, F32),
        scratch_shapes=[pltpu.VMEM((n_dev,) + block.shape, F32), pltpu.SemaphoreType.DMA((n_dev - 1,)),
                        pltpu.SemaphoreType.DMA((n_dev - 1,))],
    )(block)


def adamw(w, g, m, v, name):
    rows, cols = w.shape
    tr = min(rows, TM)

    def body(w_ref, g_ref, m_ref, v_ref, d_ref, nm_ref, nv_ref):
        gt = g_ref[...]
        nm = ADAM_B1 * m_ref[...] + (1.0 - ADAM_B1) * gt
        nv = ADAM_B2 * v_ref[...] + (1.0 - ADAM_B2) * (gt * gt)
        m_hat = nm / (1.0 - ADAM_B1 ** ADAM_STEP)
        v_hat = nv / (1.0 - ADAM_B2 ** ADAM_STEP)
        d_ref[...] = -ADAM_LR * (m_hat / (jnp.sqrt(v_hat) + ADAM_EPS) + ADAM_WD * w_ref[...])
        nm_ref[...] = nm
        nv_ref[...] = nv

    spec = pl.BlockSpec((tr, cols), lambda i: (i, 0))
    shape = jax.ShapeDtypeStruct((rows, cols), F32)
    return pl.pallas_call(
        body, name=name, grid=(rows // tr,),
        in_specs=[spec] * 4, out_specs=[spec] * 3, out_shape=[shape] * 3,
        compiler_params=_cparams(("parallel",), 40),
    )(w, g, m, v)


def kernel(x, even_norm, even_w_in, even_b_f, even_q_gain, even_k_gain, even_w_out, odd_norm, odd_w_in, odd_q_gain, odd_k_gain, odd_w_out, loss_target, m_even_norm, m_even_w_in, m_even_b_f, m_even_q_gain, m_even_k_gain, m_even_w_out, m_odd_norm, m_odd_w_in, m_odd_q_gain, m_odd_k_gain, m_odd_w_out, v_even_norm, v_even_w_in, v_even_b_f, v_even_q_gain, v_even_k_gain, v_even_w_out, v_odd_norm, v_odd_w_in, v_odd_q_gain, v_odd_k_gain, v_odd_w_out):
    w = dict(even_norm=even_norm, even_w_in=even_w_in, even_b_f=even_b_f, even_q_gain=even_q_gain, even_k_gain=even_k_gain,
             even_w_out=even_w_out, odd_norm=odd_norm, odd_w_in=odd_w_in, odd_q_gain=odd_q_gain, odd_k_gain=odd_k_gain,
             odd_w_out=odd_w_out)
    m = dict(even_norm=m_even_norm, even_w_in=m_even_w_in, even_b_f=m_even_b_f, even_q_gain=m_even_q_gain,
             even_k_gain=m_even_k_gain, even_w_out=m_even_w_out, odd_norm=m_odd_norm, odd_w_in=m_odd_w_in,
             odd_q_gain=m_odd_q_gain, odd_k_gain=m_odd_k_gain, odd_w_out=m_odd_w_out)
    v = dict(even_norm=v_even_norm, even_w_in=v_even_w_in, even_b_f=v_even_b_f, even_q_gain=v_even_q_gain,
             even_k_gain=v_even_k_gain, even_w_out=v_even_w_out, odd_norm=v_odd_norm, odd_w_in=v_odd_w_in,
             odd_q_gain=v_odd_q_gain, odd_k_gain=v_odd_k_gain, odd_w_out=v_odd_w_out)
    spare = jnp.zeros((D_MODEL // 4,), F32)
    vector_names = [n for n, _ in VECTORS if n != "odd_norm_lo"]
    flat = lambda d: {**{n: d[n].reshape(-1) for n in vector_names}, "odd_norm_lo": spare}

    on = odd_norm.reshape(-1)
    on_hi = on.astype(BF16)
    wire = {n: t.astype(BF16) for n, t in flat(w).items()}
    wire.update(odd_norm=on_hi, odd_norm_lo=(on - on_hi.astype(F32)).astype(BF16),
                even_w_in_t=even_w_in[0].T.astype(BF16), even_w_out=even_w_out[0].astype(BF16),
                odd_w_in_t=odd_w_in[0].T.astype(BF16), odd_w_out=odd_w_out[0].astype(BF16))
    chip, c = 2 * lax.axis_index("x") + lax.axis_index("y"), lax.axis_index("c")
    packed = _pack(wire)
    gathered = lax.dynamic_update_index_in_dim(all_gather_shards(packed), packed, chip, 0)
    sh = [_unpack(gathered[j]) for j in range(4)]
    cat = lambda n, axis: jnp.concatenate([t[n] for t in sh], axis=axis)
    gn_o = cat("odd_norm", 0).astype(F32) + cat("odd_norm_lo", 0).astype(F32)

    s = x.shape[1]
    loss_local, grad_x, g = local_step(
        x[0], loss_target[0], cat("even_w_in_t", 0), even_b_f[0], even_q_gain[0], even_k_gain[0], even_norm[0],
        cat("even_w_out", 0), gn_o, cat("odd_w_in_t", 0), odd_q_gain[0], odd_k_gain[0], cat("odd_w_out", 1),
        (min(1024, s), min(512, s)), (min(1024, s), min(256, s)))
    loss = lax.psum(loss_local, ("x", "y", "c"))

    g_vec = _vectors_of(allreduce_rows(_vector_block(g, GRAD_VECTORS)), GRAD_VECTORS)
    g_vec["odd_norm"] = lax.dynamic_slice_in_dim(g_vec["odd_norm"], chip * (D_MODEL // 4), D_MODEL // 4)

    zero_vectors = {n: jnp.zeros((size,), F32) for n, size in VECTORS}

    def grad_parts(j):
        parts = dict(zero_vectors)
        parts["even_w_in_t"] = g["even_w_in_t"][j * (EVEN_IN // 4):(j + 1) * (EVEN_IN // 4)]
        parts["even_w_out"] = g["even_w_out"][j * (EVEN_W // 4):(j + 1) * (EVEN_W // 4)]
        parts["odd_w_in_t"] = g["odd_w_in_t"][j * (ODD_IN // 4):(j + 1) * (ODD_IN // 4)]
        parts["odd_w_out"] = g["odd_w_out"][:, j * (D_MODEL // 4):(j + 1) * (D_MODEL // 4)]
        return parts

    g_all = jnp.stack([_pack(grad_parts(j)) for j in range(4)])
    from_sibling = sibling_swap_halves(g_all)
    g_mine = lax.dynamic_slice_in_dim(g_all, c * HALF, HALF, axis=1)
    pair = _sum_call("pair_sum", [g_mine.reshape(4 * HALF, D_MODEL), from_sibling.reshape(4 * HALF, D_MODEL)], 4 * HALF, BF16)
    pair = pair.reshape(4, HALF, D_MODEL)
    by_chip = lax.dynamic_update_index_in_dim(chip_exchange(pair), lax.dynamic_index_in_dim(pair, chip, 0), chip, 0)
    half_sum = _sum_call("chip_sum", [by_chip[0], by_chip[1], by_chip[2], by_chip[3]], HALF, F32)
    other_half = sibling_swap(half_sum)
    g_buf = jnp.concatenate([jnp.where(c == 0, half_sum, other_half), jnp.where(c == 0, other_half, half_sum)], axis=0)

    gp = _unpack(g_buf)
    grad = dict(g_vec)
    grad.update(even_w_in=gp["even_w_in_t"].T, even_w_out=gp["even_w_out"], odd_w_in=gp["odd_w_in_t"].T, odd_w_out=gp["odd_w_out"])
    results = [grad, {}, {}, {}]
    for n in ("even_w_in", "even_w_out", "odd_w_in", "odd_w_out"):
        stepped = adamw(w[n][0], grad[n], m[n][0], v[n][0], "adamw_" + n)
        for res, t in zip(results[1:], stepped):
            res[n] = t
    block = lambda d: _vector_block(flat(d))
    stepped = adamw(block(w), _vector_block({**grad, "odd_norm_lo": spare}), block(m), block(v), "adamw_vectors")
    for res, t in zip(results[1:], stepped):
        res.update(_vectors_of(t))
    outs = [loss.reshape(()), grad_x.reshape(x.shape)]
    order = ["even_norm", "even_w_in", "even_b_f", "even_q_gain", "even_k_gain", "even_w_out", "odd_norm", "odd_w_in",
             "odd_q_gain", "odd_k_gain", "odd_w_out"]
    for res in results:
        outs += [res[n].reshape(w[n].shape) for n in order]
    return tuple(outs)
```
